```python
import math
import jax, jax.numpy as jnp
from jax import lax
import numpy as np

D_MODEL = 1024
BATCH = 8
SEQ = 2048
DEPTH = 2
DEC_BATCH = 128
DEC_SEQ = 1
PAST_LEN = 16384
PAGE_SIZE = 128

GLA_HEADS = 4
GLA_DK = D_MODEL // 8
GLA_DV = D_MODEL // 4
GLA_QK = GLA_HEADS * GLA_DK
GLA_V = GLA_HEADS * GLA_DV
GLA_GATE_RANK = 16
GLA_TAU = 16.0
GLA_CHUNK = 64
ML_HEADS = 4
ML_DH = D_MODEL // 4
ML_W = ML_HEADS * ML_DH
ML_CHUNK = 64
RW_HEAD = 64
RW_W = D_MODEL
RW_HEADS = RW_W // RW_HEAD
RW_DECAY_RANK = 64
RW_A_RANK = 64
RW_G_RANK = 128
RW_COLS = 3 * RW_W + RW_DECAY_RANK + RW_A_RANK + RW_G_RANK
D_FF = 4 * D_MODEL
N_BRANCH = 3
MAIN_WIDTHS = (GLA_QK, GLA_QK, GLA_V, GLA_GATE_RANK, GLA_V,
               ML_W, ML_W, ML_W, ML_HEADS, ML_HEADS, ML_W,
               N_BRANCH * D_MODEL)
RW_WIDTHS = (RW_W, RW_W, RW_W, RW_DECAY_RANK, RW_A_RANK, RW_G_RANK)
D_MAIN = 2 * GLA_QK + 2 * GLA_V + GLA_GATE_RANK + 4 * ML_W + 2 * ML_HEADS + N_BRANCH * D_MODEL
D_IN = D_MAIN + RW_COLS
EPS = 1e-6
NEG = -1e30

kernel_name = "hybrid_gla_mlstm_rwkv7_decode_step"


def _split(z, widths):
    out, off = [], 0
    for w in widths:
        out.append(z[..., off:off + w])
        off += w
    return out


def _rmsnorm(x, g):
    xf = x.astype(jnp.float32)
    y = xf * lax.rsqrt(jnp.mean(xf * xf, axis=-1, keepdims=True) + EPS)
    return (y * g.astype(jnp.float32)).astype(x.dtype)


def _head_norm(y, g, center):
    if center:
        y = y - jnp.mean(y, axis=-1, keepdims=True)
    y = y * lax.rsqrt(jnp.mean(y * y, axis=-1, keepdims=True) + EPS)
    return y * g.astype(jnp.float32)


def _to_chunks(a, c, n, pad, fill):
    B = a.shape[0]
    widths = ((0, 0), (0, pad)) + ((0, 0),) * (a.ndim - 2)
    a = jnp.pad(a, widths, constant_values=fill)
    a = a.reshape((B, n, c) + a.shape[2:])
    perm = (1, 0, 3, 2) + tuple(range(4, a.ndim))
    return a.transpose(perm)


def _from_chunks(o, T):
    n, B, H, c, d = o.shape
    return o.transpose(1, 0, 3, 2, 4).reshape(B, n * c, H, d)[:, :T]


def _gla_chunked(q, k, v, log_a, s0):
    T = q.shape[1]
    c = min(GLA_CHUNK, T)
    n = -(-T // c)
    pad = n * c - T
    qs, ks, vs, las = (_to_chunks(a, c, n, pad, 0.0) for a in (q, k, v, log_a))
    mask = jnp.tril(jnp.ones((c, c), dtype=bool))

    def step(S, inp):
        qc, kc, vc, la = inp
        b = jnp.cumsum(la, axis=2)
        o_inter = jnp.einsum('bhtd,bhdv->bhtv', qc * jnp.exp(b), S)
        diff = b[:, :, :, None, :] - b[:, :, None, :, :]
        dec = jnp.exp(jnp.where(mask[:, :, None], diff, -jnp.inf))
        att = jnp.einsum('bhtsd,bhsd->bhts', qc[:, :, :, None, :] * dec, kc)
        o = o_inter + jnp.einsum('bhts,bhsv->bhtv', att, vc)
        b_last = b[:, :, -1:, :]
        S = jnp.exp(b_last[:, :, 0, :])[..., None] * S + jnp.einsum('bhsd,bhsv->bhdv', kc * jnp.exp(b_last - b), vc)
        return S, o

    S, o = lax.scan(step, s0, (qs, ks, vs, las))
    return _from_chunks(o, T), S


def _mlstm_chunked(q, k, v, i_pre, logf, c0, n0, m0):
    T = q.shape[1]
    c = min(ML_CHUNK, T)
    n = -(-T // c)
    pad = n * c - T
    qs, ks, vs = (_to_chunks(a, c, n, pad, 0.0) for a in (q, k, v))
    is_ = _to_chunks(i_pre, c, n, pad, NEG)
    fs = _to_chunks(logf, c, n, pad, 0.0)
    mask = jnp.tril(jnp.ones((c, c), dtype=bool))

    def step(carry, inp):
        C, nv, m = carry
        qc, kc, vc, ic, fc = inp
        F = jnp.cumsum(fc, axis=-1)
        mt = jnp.maximum(F + m[..., None], F + lax.cummax(ic - F, axis=2))
        w_inter = jnp.exp(F + m[..., None] - mt)
        logd = F[..., :, None] - F[..., None, :] + ic[..., None, :] - mt[..., :, None]
        d = jnp.exp(jnp.where(mask, logd, -jnp.inf))
        qk = jnp.einsum('bhtd,bhsd->bhts', qc, kc) * d
        num = w_inter[..., None] * jnp.einsum('bhtd,bhdv->bhtv', qc, C) + jnp.einsum('bhts,bhsv->bhtv', qk, vc)
        den = w_inter * jnp.einsum('bhtd,bhd->bht', qc, nv) + jnp.sum(qk, axis=-1)
        h = num / jnp.maximum(jnp.abs(den), jnp.exp(-mt))[..., None]
        m_new = mt[..., -1]
        w_end = jnp.exp(F[..., -1] + m - m_new)
        ws = jnp.exp(F[..., -1:] - F + ic - m_new[..., None])
        C = w_end[..., None, None] * C + jnp.einsum('bhsd,bhsv->bhdv', kc * ws[..., None], vc)
        nv = w_end[..., None] * nv + jnp.einsum('bhs,bhsd->bhd', ws, kc)
        return (C, nv, m_new), h

    (C, nv, m), h = lax.scan(step, (c0, n0, m0), (qs, ks, vs, is_, fs))
    return _from_chunks(h, T), C, nv, m


def _rwkv7_scan(r, log_w, k, v, kk, b, s0):
    def step(S, inp):
        rt, wt, kt, vt, kkt, bt = inp
        sa = jnp.einsum('bhij,bhj->bhi', S, -kkt)
        S = S * jnp.exp(wt)[:, :, None, :] + sa[..., None] * bt[:, :, None, :] + vt[..., None] * kt[:, :, None, :]
        return S, jnp.einsum('bhij,bhj->bhi', S, rt)

    xs = tuple(a.transpose(1, 0, 2, 3) for a in (r, log_w, k, v, kk, b))
    S, y = lax.scan(step, s0, xs)
    return y.transpose(1, 0, 2, 3), S


def _layer(x, gla_s, ml_c, ml_n, ml_m, rw_s, shift,
           norm1_g, w_in, gla_a_up, gla_a_bias, gla_norm_g, ml_i_bias, ml_f_bias, ml_norm_g,
           rw_mu, rw_w0, rw_w_up, rw_a0, rw_a_up, rw_g_up, rw_k_k, rw_k_a, rw_r_k, rw_norm_g,
           w_branch, w_out, norm2_g, w_ff1, w_ff2):
    f32 = jnp.float32
    B, T, _ = x.shape
    h = _rmsnorm(x, norm1_g)
    z = h @ w_in
    gq, gk, gv, ga, gg, mq, mk, mv, mi, mf, mo, gate = _split(z[..., :D_MAIN], MAIN_WIDTHS)

    q = gq.astype(f32).reshape(B, T, GLA_HEADS, GLA_DK) * (GLA_DK ** -0.5)
    k = gk.astype(f32).reshape(B, T, GLA_HEADS, GLA_DK)
    v = gv.astype(f32).reshape(B, T, GLA_HEADS, GLA_DV)
    log_a = jax.nn.log_sigmoid((ga.astype(f32) @ gla_a_up.astype(f32)) + gla_a_bias.astype(f32)) / GLA_TAU
    o, gla_new = _gla_chunked(q, k, v, log_a.reshape(B, T, GLA_HEADS, GLA_DK), gla_s.astype(f32))
    o_gla = (_head_norm(o, gla_norm_g, False) * jax.nn.silu(gg.astype(f32)).reshape(B, T, GLA_HEADS, GLA_DV)).reshape(B, T, GLA_V)

    q = mq.astype(f32).reshape(B, T, ML_HEADS, ML_DH) * (ML_DH ** -0.5)
    k = mk.astype(f32).reshape(B, T, ML_HEADS, ML_DH)
    v = mv.astype(f32).reshape(B, T, ML_HEADS, ML_DH)
    i_pre = mi.astype(f32) + ml_i_bias.astype(f32)
    logf = jax.nn.log_sigmoid(mf.astype(f32) + ml_f_bias.astype(f32))
    hm, c_new, n_new, m_new = _mlstm_chunked(q, k, v, i_pre, logf, ml_c.astype(f32), ml_n.astype(f32), ml_m.astype(f32))
    o_ml = (_head_norm(hm, ml_norm_g, True) * jax.nn.sigmoid(mo.astype(f32)).reshape(B, T, ML_HEADS, ML_DH)).reshape(B, T, ML_W)

    zr = z[..., D_MAIN:]
    zprev = jnp.concatenate([shift[:, None, :].astype(zr.dtype), zr[:, :-1]], axis=1)
    zm = (zr + (zprev - zr) * rw_mu).astype(f32)
    r, k, v, wd, ad, gd = _split(zm, RW_WIDTHS)
    log_w = -jnp.exp(-jax.nn.softplus(-(rw_w0.astype(f32) + jnp.tanh(wd) @ rw_w_up.astype(f32))) - 0.5)
    a = jax.nn.sigmoid(rw_a0.astype(f32) + ad @ rw_a_up.astype(f32))
    g = jax.nn.sigmoid(gd) @ rw_g_up.astype(f32)
    hd = lambda t: t.reshape(B, T, RW_HEADS, RW_HEAD)
    kk = hd(k * rw_k_k.astype(f32))
    kk = kk * lax.rsqrt(jnp.sum(kk * kk, axis=-1, keepdims=True) + 1e-12)
    k = hd(k * (1.0 + (a - 1.0) * rw_k_a.astype(f32)))
    r, v, a = hd(r), hd(v), hd(a)
    y, rw_new = _rwkv7_scan(r, hd(log_w), k, v, kk, kk * a, rw_s.astype(f32))
    y = _head_norm(y, rw_norm_g.reshape(RW_HEADS, RW_HEAD), True) + jnp.sum(r * k * rw_r_k.astype(f32), axis=-1, keepdims=True) * v
    o_rw = y.reshape(B, T, RW_W) * g

    branches = jnp.stack([o_gla, o_ml, o_rw], axis=2).astype(x.dtype)
    proj = jnp.einsum('btcw,cwd->btcd', branches, w_branch)
    gates = jax.nn.sigmoid(gate.astype(f32)).reshape(B, T, N_BRANCH, D_MODEL)
    merged = jnp.sum(proj.astype(f32) * gates, axis=2).astype(x.dtype)
    x = x + merged @ w_out

    h2 = _rmsnorm(x, norm2_g)
    x = x + jnp.square(jax.nn.relu(h2 @ w_ff1)) @ w_ff2
    new = (gla_new.astype(gla_s.dtype), c_new.astype(ml_c.dtype), n_new.astype(ml_n.dtype),
           m_new.astype(ml_m.dtype), rw_new.astype(rw_s.dtype), zr[:, -1].astype(shift.dtype))
    return x, new


def _trunk(x, states, layer_params, final_g):
    new = ([], [], [], [], [], [])
    for l in range(DEPTH):
        x, st = _layer(x, *[s[l] for s in states], *[p[l] for p in layer_params])
        for lst, s in zip(new, st):
            lst.append(s)
    return _rmsnorm(x, final_g), tuple(jnp.stack(s) for s in new)


def setup_inputs(seed: int = 0) -> dict:
    key = jax.random.key(seed)
    ks = iter(jax.random.split(key, 48))
    f32 = jnp.float32

    def nrm(shape, scale):
        return jax.random.normal(next(ks), shape, f32) * scale

    def gain(shape):
        return 1.0 + nrm(shape, 0.02)

    L, DB, D = DEPTH, DEC_BATCH, D_MODEL
    return {
        "x_prompt": nrm((BATCH, SEQ, D), 1.0),
        "x_sample": nrm((DB, DEC_SEQ, D), 1.0),
        "state_gla": nrm((L, DB, GLA_HEADS, GLA_DK, GLA_DV), 0.1),
        "state_mlstm_c": nrm((L, DB, ML_HEADS, ML_DH, ML_DH), 0.02),
        "state_mlstm_n": nrm((L, DB, ML_HEADS, ML_DH), 0.3),
        "state_mlstm_m": nrm((L, DB, ML_HEADS), 1.0),
        "state_rwkv": nrm((L, DB, RW_HEADS, RW_HEAD, RW_HEAD), 0.1),
        "state_shift": nrm((L, DB, RW_COLS), 1.0),
        "norm1_g": gain((L, D)),
        "w_in": nrm((L, D, D_IN), D ** -0.5),
        "gla_a_up": nrm((L, GLA_GATE_RANK, GLA_QK), 0.5 * GLA_GATE_RANK ** -0.5),
        "gla_a_bias": 2.0 + nrm((L, GLA_QK), 0.5),
        "gla_norm_g": gain((L, GLA_DV)),
        "ml_i_bias": nrm((L, ML_HEADS), 0.5),
        "ml_f_bias": jnp.linspace(3.0, 6.0, ML_HEADS, dtype=f32)[None, :] + nrm((L, ML_HEADS), 0.1),
        "ml_norm_g": gain((L, ML_DH)),
        "rw_mu": jax.random.uniform(next(ks), (L, RW_COLS), f32),
        "rw_w0": jax.random.uniform(next(ks), (L, RW_W), f32, minval=-3.0, maxval=1.0),
        "rw_w_up": nrm((L, RW_DECAY_RANK, RW_W), 0.1),
        "rw_a0": nrm((L, RW_W), 0.5),
        "rw_a_up": nrm((L, RW_A_RANK, RW_W), 0.1),
        "rw_g_up": nrm((L, RW_G_RANK, RW_W), RW_G_RANK ** -0.5),
        "rw_k_k": 0.85 + nrm((L, RW_W), 0.05),
        "rw_k_a": 1.0 + nrm((L, RW_W), 0.05),
        "rw_r_k": nrm((L, RW_HEADS, RW_HEAD), 0.1),
        "rw_norm_g": gain((L, RW_W)),
        "w_branch": nrm((L, N_BRANCH, GLA_V, D), GLA_V ** -0.5),
        "w_out": nrm((L, D, D), D ** -0.5),
        "norm2_g": gain((L, D)),
        "w_ff1": nrm((L, D, D_FF), D ** -0.5),
        "w_ff2": nrm((L, D_FF, D), D_FF ** -0.5),
        "final_g": gain((D,)),
    }


def reference(x_prompt, x_sample, state_gla, state_mlstm_c, state_mlstm_n, state_mlstm_m, state_rwkv, state_shift,
              norm1_g, w_in, gla_a_up, gla_a_bias, gla_norm_g, ml_i_bias, ml_f_bias, ml_norm_g,
              rw_mu, rw_w0, rw_w_up, rw_a0, rw_a_up, rw_g_up, rw_k_k, rw_k_a, rw_r_k, rw_norm_g,
              w_branch, w_out, norm2_g, w_ff1, w_ff2, final_g):
    layer_params = (norm1_g, w_in, gla_a_up, gla_a_bias, gla_norm_g, ml_i_bias, ml_f_bias, ml_norm_g,
                    rw_mu, rw_w0, rw_w_up, rw_a0, rw_a_up, rw_g_up, rw_k_k, rw_k_a, rw_r_k, rw_norm_g,
                    w_branch, w_out, norm2_g, w_ff1, w_ff2)
    sample_states = (state_gla, state_mlstm_c, state_mlstm_n, state_mlstm_m, state_rwkv, state_shift)
    B = x_prompt.shape[0]
    prompt_states = tuple(jnp.zeros((s.shape[0], B) + s.shape[2:], s.dtype) for s in sample_states)
    y_prompt, (gla_p, mlc_p, mln_p, mlm_p, rw_p, shift_p) = _trunk(x_prompt, prompt_states, layer_params, final_g)
    y_sample, (gla_s, mlc_s, mln_s, mlm_s, rw_s, shift_s) = _trunk(x_sample, sample_states, layer_params, final_g)
    return (y_prompt, y_sample, gla_p, gla_s, mlc_p, mlc_s, mln_p, mln_s, mlm_p, mlm_s, rw_p, rw_s, shift_p, shift_s)
```

```python
import functools

import jax
import jax.numpy as jnp
from jax import lax
from jax.experimental import pallas as pl
from jax.experimental.pallas import tpu as pltpu

F32 = jnp.float32
BF16 = jnp.bfloat16

D_MODEL = 1024
GLA_HEADS, GLA_DK, GLA_DV = 4, 128, 256
GLA_QK, GLA_V, GLA_RANK, GLA_TAU = 512, 1024, 16, 16.0
ML_HEADS, ML_DH, ML_W = 4, 256, 1024
RW_HEAD, RW_W, RW_HEADS = 64, 1024, 16
RW_PAIRS = RW_HEADS // 2
RW_COLS = 3 * RW_W + 64 + 64 + 128
D_FF = 4 * D_MODEL
EPS = 1e-6
CHUNK = 64

OFF_GQ, OFF_GK, OFF_GV, OFF_GG = 0, 512, 1024, 2048
OFF_MQ, OFF_MK, OFF_MV, OFF_MO = 3072, 4096, 5120, 6144
OFF_GATE = 7168
OFF_RW = 10240
OFF_SM = 13568
NP = 13824
SM_I, SM_F = 16, 20

VMEM_LIMIT = 48 * 1024 * 1024


def _dot(a, b):
    return jnp.dot(a.astype(BF16), b.astype(BF16), preferred_element_type=F32)


def _dot_nt(a, b):
    return lax.dot_general(a.astype(BF16), b.astype(BF16), (((1,), (1,)), ((), ())),
                           preferred_element_type=F32)


def _dot_tn(a, b):
    return lax.dot_general(a.astype(BF16), b.astype(BF16), (((0,), (0,)), ((), ())),
                           preferred_element_type=F32)


def _split(x):
    hi = x.astype(BF16)
    lo = (x - hi.astype(F32)).astype(BF16)
    return hi, lo


def _dot2(x, w_exact):
    hi, lo = _split(x)
    return (jnp.dot(hi, w_exact, preferred_element_type=F32)
            + jnp.dot(lo, w_exact, preferred_element_type=F32))


def _dot2_tn(x, w_exact):
    hi, lo = _split(x)
    dn = (((0,), (0,)), ((), ()))
    return (lax.dot_general(hi, w_exact, dn, preferred_element_type=F32)
            + lax.dot_general(lo, w_exact, dn, preferred_element_type=F32))


def _dot2_left(w_exact, x):
    hi, lo = _split(x)
    return (jnp.dot(w_exact, hi, preferred_element_type=F32)
            + jnp.dot(w_exact, lo, preferred_element_type=F32))


def _logsig(x):
    return jnp.minimum(x, 0.0) - jnp.log1p(jnp.exp(-jnp.abs(x)))


def _sigmoid(x):
    return jax.nn.sigmoid(x)


def _rms(x, g):
    return x * lax.rsqrt(jnp.mean(x * x, axis=-1, keepdims=True) + EPS) * g


def _tri(n, m=None):
    m = n if m is None else m
    r = lax.broadcasted_iota(jnp.int32, (n, m), 0)
    c = lax.broadcasted_iota(jnp.int32, (n, m), 1)
    return r, c


def _proj_in_kernel(x_ref, g_ref, w_ref, o_ref, h_ref):
    @pl.when(pl.program_id(1) == 0)
    def _():
        h_ref[...] = _rms(x_ref[...], g_ref[...]).astype(BF16)

    o_ref[...] = jnp.dot(h_ref[...], w_ref[...], preferred_element_type=F32)


def _proj_in(x, g, w, tm, tn=512):
    m = x.shape[0]
    return pl.pallas_call(
        _proj_in_kernel,
        grid=(m // tm, NP // tn),
        in_specs=[pl.BlockSpec((tm, D_MODEL), lambda i, j: (i, 0)),
                  pl.BlockSpec((1, D_MODEL), lambda i, j: (0, 0)),
                  pl.BlockSpec((D_MODEL, tn), lambda i, j: (0, j))],
        out_specs=pl.BlockSpec((tm, tn), lambda i, j: (i, j)),
        out_shape=jax.ShapeDtypeStruct((m, NP), F32),
        scratch_shapes=[pltpu.VMEM((tm, D_MODEL), BF16)],
        compiler_params=pltpu.CompilerParams(
            dimension_semantics=("parallel", "arbitrary"), vmem_limit_bytes=VMEM_LIMIT),
        name="proj_in",
    )(x, g, w)


def _merge_kernel(og_ref, om_ref, or_ref, g0_ref, g1_ref, g2_ref, x_ref, wb_ref, wo_ref, o_ref):
    acc = jnp.dot(og_ref[...], wb_ref[0], preferred_element_type=F32) * _sigmoid(g0_ref[...])
    acc += jnp.dot(om_ref[...], wb_ref[1], preferred_element_type=F32) * _sigmoid(g1_ref[...])
    acc += jnp.dot(or_ref[...], wb_ref[2], preferred_element_type=F32) * _sigmoid(g2_ref[...])
    o_ref[...] = x_ref[...] + jnp.dot(acc.astype(BF16), wo_ref[...], preferred_element_type=F32)


def _merge(o_gla, o_ml, o_rw, z, x, wb, wo, tm):
    m = x.shape[0]
    row = lambda i: (i, 0)
    gate_blk = OFF_GATE // D_MODEL
    return pl.pallas_call(
        _merge_kernel,
        grid=(m // tm,),
        in_specs=[pl.BlockSpec((tm, D_MODEL), row),
                  pl.BlockSpec((tm, D_MODEL), row),
                  pl.BlockSpec((tm, D_MODEL), row),
                  pl.BlockSpec((tm, D_MODEL), lambda i: (i, gate_blk)),
                  pl.BlockSpec((tm, D_MODEL), lambda i: (i, gate_blk + 1)),
                  pl.BlockSpec((tm, D_MODEL), lambda i: (i, gate_blk + 2)),
                  pl.BlockSpec((tm, D_MODEL), row),
                  pl.BlockSpec((3, D_MODEL, D_MODEL), lambda i: (0, 0, 0)),
                  pl.BlockSpec((D_MODEL, D_MODEL), lambda i: (0, 0))],
        out_specs=pl.BlockSpec((tm, D_MODEL), row),
        out_shape=jax.ShapeDtypeStruct((m, D_MODEL), F32),
        compiler_params=pltpu.CompilerParams(
            dimension_semantics=("parallel",), vmem_limit_bytes=VMEM_LIMIT),
        name="merge",
    )(o_gla, o_ml, o_rw, z, z, z, x, wb, wo)


def _ffn_kernel(x_ref, g_ref, w1_ref, w2_ref, fg_ref, o_ref, h_ref, acc_ref, *, final):
    f = pl.program_id(1)

    @pl.when(f == 0)
    def _():
        h_ref[...] = _rms(x_ref[...], g_ref[...]).astype(BF16)
        acc_ref[...] = jnp.zeros_like(acc_ref)

    a = jnp.dot(h_ref[...], w1_ref[...], preferred_element_type=F32)
    a = jnp.square(jnp.maximum(a, 0.0))
    acc_ref[...] += jnp.dot(a.astype(BF16), w2_ref[...], preferred_element_type=F32)

    @pl.when(f == pl.num_programs(1) - 1)
    def _():
        y = x_ref[...] + acc_ref[...]
        if final:
            y = _rms(y, fg_ref[...])
        o_ref[...] = y


def _ffn(x, g, w1, w2, fg, final, tm, tf=512):
    m = x.shape[0]
    return pl.pallas_call(
        functools.partial(_ffn_kernel, final=final),
        grid=(m // tm, D_FF // tf),
        in_specs=[pl.BlockSpec((tm, D_MODEL), lambda i, f: (i, 0)),
                  pl.BlockSpec((1, D_MODEL), lambda i, f: (0, 0)),
                  pl.BlockSpec((D_MODEL, tf), lambda i, f: (0, f)),
                  pl.BlockSpec((tf, D_MODEL), lambda i, f: (f, 0)),
                  pl.BlockSpec((1, D_MODEL), lambda i, f: (0, 0))],
        out_specs=pl.BlockSpec((tm, D_MODEL), lambda i, f: (i, 0)),
        out_shape=jax.ShapeDtypeStruct((m, D_MODEL), F32),
        scratch_shapes=[pltpu.VMEM((tm, D_MODEL), BF16), pltpu.VMEM((tm, D_MODEL), F32)],
        compiler_params=pltpu.CompilerParams(
            dimension_semantics=("parallel", "arbitrary"), vmem_limit_bytes=VMEM_LIMIT),
        name="ffn",
    )(x, g, w1, w2, fg)


def _gla_prompt_kernel(q_ref, k_ref, v_ref, gg_ref, sm_ref, aup_ref, ab_ref, gn_ref,
                       o_ref, sout_ref, s_ref, *, n_chunks):
    t = pl.program_id(1)

    @pl.when(t == 0)
    def _():
        s_ref[...] = jnp.zeros_like(s_ref)

    c_ = CHUNK
    r, c = _tri(c_)
    incl = r >= c
    tril = jnp.where(incl, 1.0, 0.0).astype(BF16)
    ones_c = jnp.ones((c_, GLA_DK), BF16)

    def chunk(ci, carry):
        rows = pl.ds(pl.multiple_of(ci * c_, c_), c_)
        sm = sm_ref[rows, :]
        for h in range(GLA_HEADS):
            kl = slice(h * GLA_DK, (h + 1) * GLA_DK)
            vl = slice(h * GLA_DV, (h + 1) * GLA_DV)
            x = _dot(sm, aup_ref[:, kl]) + ab_ref[:, kl]
            la = _logsig(x) * (1.0 / GLA_TAU)
            b = _dot2_left(tril, la)
            q = q_ref[rows, kl] * (GLA_DK ** -0.5)
            k = k_ref[rows, kl]
            v = v_ref[rows, vl]
            qe = q * jnp.exp(b)
            ke = k * jnp.exp(-b)
            att = jnp.where(incl, _dot_nt(qe, ke), 0.0)
            s = s_ref[h]
            o = _dot(qe, s) + _dot(att, v)
            bl = b[c_ - 1:c_, :]
            kd = k * jnp.exp(bl - b)
            dec = jnp.exp(_dot2_tn(la, ones_c))
            s_ref[h] = s * jnp.concatenate([dec, dec], axis=1) + _dot_tn(kd, v)
            on = o * lax.rsqrt(jnp.mean(o * o, axis=-1, keepdims=True) + EPS) * gn_ref[...]
            gg = gg_ref[rows, vl]
            o_ref[rows, vl] = (on * (gg * _sigmoid(gg))).astype(BF16)
        return carry

    lax.fori_loop(0, n_chunks, chunk, 0)

    @pl.when(t == pl.num_programs(1) - 1)
    def _():
        sout_ref[0] = s_ref[...]


def _gla_prompt(z, aup_pad, ab, gn, b_, t_, tt):
    nt = t_ // tt
    row = lambda blk: (lambda b, t: (b * nt + t, blk))
    const = lambda b, t: (0, 0)
    return pl.pallas_call(
        functools.partial(_gla_prompt_kernel, n_chunks=tt // CHUNK),
        grid=(b_, nt),
        in_specs=[pl.BlockSpec((tt, GLA_QK), row(OFF_GQ // GLA_QK)),
                  pl.BlockSpec((tt, GLA_QK), row(OFF_GK // GLA_QK)),
                  pl.BlockSpec((tt, GLA_V), row(OFF_GV // GLA_V)),
                  pl.BlockSpec((tt, GLA_V), row(OFF_GG // GLA_V)),
                  pl.BlockSpec((tt, 128), row(OFF_SM // 128)),
                  pl.BlockSpec((128, GLA_QK), const),
                  pl.BlockSpec((1, GLA_QK), const),
                  pl.BlockSpec((1, GLA_DV), const)],
        out_specs=[pl.BlockSpec((tt, GLA_V), lambda b, t: (b * nt + t, 0)),
                   pl.BlockSpec((1, GLA_HEADS, GLA_DK, GLA_DV), lambda b, t: (b, 0, 0, 0))],
        out_shape=[jax.ShapeDtypeStruct((b_ * t_, GLA_V), BF16),
                   jax.ShapeDtypeStruct((b_, GLA_HEADS, GLA_DK, GLA_DV), F32)],
        scratch_shapes=[pltpu.VMEM((GLA_HEADS, GLA_DK, GLA_DV), F32)],
        compiler_params=pltpu.CompilerParams(
            dimension_semantics=("parallel", "arbitrary"), vmem_limit_bytes=VMEM_LIMIT),
        name="gla_prompt",
    )(z, z, z, z, z, aup_pad, ab, gn)


def _mlstm_prompt_kernel(q_ref, k_ref, v_ref, mo_ref, sm_ref, smt_ref, brow_ref, bcol_ref, ng_ref,
                         o_ref, cout_ref, nout_ref, mout_ref, c_ref, n_ref, m_ref, *, n_chunks):
    t = pl.program_id(1)

    @pl.when(t == 0)
    def _():
        c_ref[...] = jnp.zeros_like(c_ref)
        n_ref[...] = jnp.zeros_like(n_ref)
        m_ref[...] = jnp.zeros_like(m_ref)

    c_ = CHUNK
    r, c = _tri(c_)
    incl = r >= c
    tril = jnp.where(incl, 1.0, 0.0).astype(BF16)
    triu = jnp.where(r <= c, 1.0, 0.0).astype(BF16)
    sub8 = lax.broadcasted_iota(jnp.int32, (8, c_), 0)

    def chunk(ci, carry):
        rows = pl.ds(pl.multiple_of(ci * c_, c_), c_)
        sm = sm_ref[rows, :] + brow_ref[...]
        fcol = _dot2_left(tril, _logsig(sm))
        smt = smt_ref[0, ci] + bcol_ref[...]
        frow = _dot2(jnp.where(sub8 >= 4, _logsig(smt), 0.0), triu)
        for h in range(ML_HEADS):
            hl = slice(h * ML_DH, (h + 1) * ML_DH)
            f_t = fcol[:, SM_F + h:SM_F + h + 1]
            i_t = sm[:, SM_I + h:SM_I + h + 1]
            f_s = frow[4 + h:5 + h, :]
            i_s = smt[h:h + 1, :]
            m = m_ref[h][0:1, 0:1]
            gmax = jnp.max(jnp.where(incl, i_s - f_s, -jnp.inf), axis=1, keepdims=True)
            mt = jnp.maximum(f_t + m, f_t + gmax)
            w_inter = jnp.exp(f_t + m - mt)
            logd = (f_t - mt) + (i_s - f_s)
            d = jnp.exp(jnp.where(incl, logd, -jnp.inf))
            q = q_ref[rows, hl] * (ML_DH ** -0.5)
            k = k_ref[rows, hl]
            v = v_ref[rows, hl]
            qk = _dot_nt(q, k) * d
            cm = c_ref[h]
            n8 = n_ref[h]
            num = w_inter * _dot(q, cm) + _dot(qk, v)
            qn = _dot_nt(q, n8)[:, 0:1]
            den = w_inter * qn + jnp.sum(qk, axis=1, keepdims=True)
            hm = num / jnp.maximum(jnp.abs(den), jnp.exp(-mt))
            m_new = mt[c_ - 1:c_, :]
            f_last = f_t[c_ - 1:c_, :]
            w_end = jnp.exp(f_last + m - m_new)
            ws = jnp.exp(f_last - f_t + i_t - m_new)
            kws = k * ws
            c_ref[h] = w_end * cm + _dot_tn(kws, v)
            n_ref[h] = w_end * n8 + jnp.broadcast_to(jnp.sum(kws, axis=0, keepdims=True), (8, ML_DH))
            m_ref[h] = jnp.broadcast_to(m_new, (8, 128))
            y = hm - jnp.mean(hm, axis=-1, keepdims=True)
            y = y * lax.rsqrt(jnp.mean(y * y, axis=-1, keepdims=True) + EPS) * ng_ref[...]
            o_ref[rows, hl] = (y * _sigmoid(mo_ref[rows, hl])).astype(BF16)
        return carry

    lax.fori_loop(0, n_chunks, chunk, 0)

    @pl.when(t == pl.num_programs(1) - 1)
    def _():
        cout_ref[0] = c_ref[...]
        for h in range(ML_HEADS):
            nout_ref[0, h:h + 1, :] = n_ref[h][0:1, :]
            mout_ref[0, h:h + 1, :] = m_ref[h][0:1, :]


def _mlstm_prompt(z, smt, brow, bcol, ng, b_, t_, tt):
    nt = t_ // tt
    nc = tt // CHUNK
    row = lambda blk: (lambda b, t: (b * nt + t, blk))
    const = lambda b, t: (0, 0)
    return pl.pallas_call(
        functools.partial(_mlstm_prompt_kernel, n_chunks=nc),
        grid=(b_, nt),
        in_specs=[pl.BlockSpec((tt, ML_W), row(OFF_MQ // ML_W)),
                  pl.BlockSpec((tt, ML_W), row(OFF_MK // ML_W)),
                  pl.BlockSpec((tt, ML_W), row(OFF_MV // ML_W)),
                  pl.BlockSpec((tt, ML_W), row(OFF_MO // ML_W)),
                  pl.BlockSpec((tt, 128), row(OFF_SM // 128)),
                  pl.BlockSpec((1, nc, 8, CHUNK), lambda b, t: (b, t, 0, 0)),
                  pl.BlockSpec((1, 128), const),
                  pl.BlockSpec((8, CHUNK), const),
                  pl.BlockSpec((1, ML_DH), const)],
        out_specs=[pl.BlockSpec((tt, ML_W), lambda b, t: (b * nt + t, 0)),
                   pl.BlockSpec((1, ML_HEADS, ML_DH, ML_DH), lambda b, t: (b, 0, 0, 0)),
                   pl.BlockSpec((1, ML_HEADS, ML_DH), lambda b, t: (b, 0, 0)),
                   pl.BlockSpec((1, ML_HEADS, 128), lambda b, t: (b, 0, 0))],
        out_shape=[jax.ShapeDtypeStruct((b_ * t_, ML_W), BF16),
                   jax.ShapeDtypeStruct((b_, ML_HEADS, ML_DH, ML_DH), F32),
                   jax.ShapeDtypeStruct((b_, ML_HEADS, ML_DH), F32),
                   jax.ShapeDtypeStruct((b_, ML_HEADS, 128), F32)],
        scratch_shapes=[pltpu.VMEM((ML_HEADS, ML_DH, ML_DH), F32),
                        pltpu.VMEM((ML_HEADS, 8, ML_DH), F32),
                        pltpu.VMEM((ML_HEADS, 8, 128), F32)],
        compiler_params=pltpu.CompilerParams(
            dimension_semantics=("parallel", "arbitrary"), vmem_limit_bytes=VMEM_LIMIT),
        name="mlstm_prompt",
    )(z, z, z, z, z, smt, brow, bcol, ng)


def _rw_tokens(zr, zk, zv, zwa, zg, w0, wup, a0, aup, gup, kkw, kaw):
    u = w0 + _dot(jnp.tanh(zwa), wup)
    lw = -jnp.exp(_logsig(u) - 0.5)
    a = _sigmoid(a0 + _dot(zwa, aup))
    g = _dot(_sigmoid(zg), gup)
    kk0 = zk * kkw
    kmod = zk * (1.0 + (a - 1.0) * kaw)
    return lw, a, g, kk0, kmod


def _ones_bd():
    r, c = _tri(128)
    return jnp.where((r // RW_HEAD) == (c // RW_HEAD), 1.0, 0.0).astype(BF16)


def _seg(x, ones_bd):
    return _dot2(x, ones_bd)


def _rw_epilogue(y, r, kmod, v, g, rk, ng, ones_bd):
    mean = _seg(y, ones_bd) * (1.0 / RW_HEAD)
    yc = y - mean
    var = _seg(yc * yc, ones_bd) * (1.0 / RW_HEAD)
    yn = yc * lax.rsqrt(var + EPS) * ng
    bonus = _seg(r * kmod * rk, ones_bd) * v
    return ((yn + bonus) * g).astype(BF16)


def _rw_prompt_kernel(r_ref, k_ref, v_ref, wa_ref, gd_ref,
                      mur_ref, muk_ref, muv_ref, muwa_ref, mug_ref,
                      w0_ref, wup_ref, a0_ref, aup_ref, gup_ref, kkw_ref, kaw_ref, rk_ref, ng_ref,
                      o_ref, sout_ref,
                      s_ref, zr_s, zk_s, zv_s, zwa_s, zg_s, pr_s, pk_s, pv_s, pwa_s, pg_s,
                      *, n_chunks, tt):
    t = pl.program_id(1)

    @pl.when(t == 0)
    def _():
        s_ref[...] = jnp.zeros_like(s_ref)
        for p_ in (pr_s, pk_s, pv_s, pwa_s, pg_s):
            p_[...] = jnp.zeros_like(p_)

    def mix(x_ref, prev_ref, mu_ref, dst_ref):
        x = x_ref[...]
        rid = lax.broadcasted_iota(jnp.int32, x.shape, 0)
        prev = jnp.where(rid == 0, prev_ref[...], pltpu.roll(x, 1, 0))
        dst_ref[...] = x + (prev - x) * mu_ref[...]
        prev_ref[...] = x[tt - 1:tt, :]

    mix(r_ref, pr_s, mur_ref, zr_s)
    mix(k_ref, pk_s, muk_ref, zk_s)
    mix(v_ref, pv_s, muv_ref, zv_s)
    mix(wa_ref, pwa_s, muwa_ref, zwa_s)
    mix(gd_ref, pg_s, mug_ref, zg_s)

    c_ = CHUNK
    r_, c2 = _tri(c_, 128)
    s_idx = c2 % c_
    strict_pk = s_idx < r_
    incl_pk = s_idx <= r_
    eye_pk = jnp.where(s_idx == r_, 1.0, 0.0)
    rr, cc = _tri(c_)
    tril = jnp.where(rr >= cc, 1.0, 0.0).astype(BF16)
    lane = lax.broadcasted_iota(jnp.int32, (c_, 128), 1)
    lo_half = lane < RW_HEAD
    r128, c128 = _tri(128)
    bdmask = (r128 // RW_HEAD) == (c128 // RW_HEAD)
    ones_bd = jnp.where(bdmask, 1.0, 0.0).astype(BF16)

    def bd(y):
        yb = y.astype(BF16)
        zero = jnp.zeros_like(yb)
        return jnp.concatenate([jnp.where(lo_half, yb, zero), jnp.where(lo_half, zero, yb)], axis=0)

    def pk_mul(x, y):
        return jnp.dot(x.astype(BF16), bd(y), preferred_element_type=F32)

    def chunk(ci, carry):
        rows = pl.ds(pl.multiple_of(ci * c_, c_), c_)
        zr = zr_s[rows, :]
        zk = zk_s[rows, :]
        zv = zv_s[rows, :]
        lw, a, g, kk0, kmod = _rw_tokens(zr, zk, zv, zwa_s[rows, :], zg_s[rows, :],
                                         w0_ref[...], wup_ref[...], a0_ref[...], aup_ref[...],
                                         gup_ref[...], kkw_ref[...], kaw_ref[...])
        for p in range(RW_PAIRS):
            ln = slice(p * 128, (p + 1) * 128)
            kkp = kk0[:, ln]
            kkn = kkp * lax.rsqrt(_seg(kkp * kkp, ones_bd) + 1e-12)
            bp = kkn * a[:, ln]
            rp, kp, vp, lwp = zr[:, ln], kmod[:, ln], zv[:, ln], lw[:, ln]
            cum = _dot2_left(tril, lwp)
            cl = cum[c_ - 1:c_, :]
            e_cum = jnp.exp(cum)
            e_neg = jnp.exp(-cum)
            e_end = jnp.exp(cl - cum)
            rh = rp * e_cum
            kh = kkn * jnp.exp(cum - lwp)
            kt = kp * e_neg
            bt = bp * e_neg
            kg = kp * e_end
            bg = bp * e_end
            kr = jnp.concatenate([kh, rh], axis=0).astype(BF16)
            pk = lax.dot_general(kr, bd(kt), (((1,), (1,)), ((), ())), preferred_element_type=F32)
            pb = lax.dot_general(kr, bd(bt), (((1,), (1,)), ((), ())), preferred_element_type=F32)
            ak = jnp.where(strict_pk, pk[0:c_], 0.0)
            gk = jnp.where(incl_pk, pk[c_:2 * c_], 0.0)
            ab = jnp.where(strict_pk, pb[0:c_], 0.0)
            gb = jnp.where(incl_pk, pb[c_:2 * c_], 0.0)
            pw = -ab
            tm = eye_pk + pw
            for _ in range(5):
                pw = pk_mul(pw, pw)
                tm = tm + pk_mul(tm, pw)
            khp = pk_mul(tm, kh)
            u0 = pk_mul(tm, pk_mul(ak, vp))
            rhp = rh - pk_mul(gb, khp)
            y0 = pk_mul(gk, vp) - pk_mul(gb, u0)
            d = _dot_tn(jnp.concatenate([vp, -u0], axis=0), jnp.concatenate([kg, bg], axis=0))
            s = s_ref[p]
            uy = _dot_nt(jnp.concatenate([khp, rhp], axis=0), s)
            ut = uy[0:c_]
            y = uy[c_:2 * c_] + y0
            s_ref[p] = s * jnp.exp(cl) + jnp.where(bdmask, d - _dot_tn(ut, bg), 0.0)
            o_ref[rows, ln] = _rw_epilogue(y, rp, kp, vp, g[:, ln], rk_ref[:, ln], ng_ref[:, ln], ones_bd)
        return carry

    lax.fori_loop(0, n_chunks, chunk, 0)

    @pl.when(t == pl.num_programs(1) - 1)
    def _():
        for p in range(RW_PAIRS):
            s = s_ref[p]
            sout_ref[0, 2 * p] = s[0:RW_HEAD, 0:RW_HEAD]
            sout_ref[0, 2 * p + 1] = s[RW_HEAD:, RW_HEAD:]


def _rw_prompt(z, mus, params, b_, t_, tt):
    nt = t_ // tt
    row = lambda blk: (lambda b, t: (b * nt + t, blk))
    const = lambda b, t: (0, 0)
    w0, wup, a0, aup, gup, kkw, kaw, rk, ng = params
    full = lambda a: pl.BlockSpec(a.shape, const)
    rw_blk = OFF_RW // RW_W
    return pl.pallas_call(
        functools.partial(_rw_prompt_kernel, n_chunks=tt // CHUNK, tt=tt),
        grid=(b_, nt),
        in_specs=[pl.BlockSpec((tt, RW_W), row(rw_blk)),
                  pl.BlockSpec((tt, RW_W), row(rw_blk + 1)),
                  pl.BlockSpec((tt, RW_W), row(rw_blk + 2)),
                  pl.BlockSpec((tt, 128), row((OFF_RW + 3 * RW_W) // 128)),
                  pl.BlockSpec((tt, 128), row((OFF_RW + 3 * RW_W) // 128 + 1))]
                 + [full(a) for a in mus] + [full(a) for a in params],
        out_specs=[pl.BlockSpec((tt, RW_W), lambda b, t: (b * nt + t, 0)),
                   pl.BlockSpec((1, RW_HEADS, RW_HEAD, RW_HEAD), lambda b, t: (b, 0, 0, 0))],
        out_shape=[jax.ShapeDtypeStruct((b_ * t_, RW_W), BF16),
                   jax.ShapeDtypeStruct((b_, RW_HEADS, RW_HEAD, RW_HEAD), F32)],
        scratch_shapes=[pltpu.VMEM((RW_PAIRS, 128, 128), F32),
                        pltpu.VMEM((tt, RW_W), F32), pltpu.VMEM((tt, RW_W), F32),
                        pltpu.VMEM((tt, RW_W), F32), pltpu.VMEM((tt, 128), F32),
                        pltpu.VMEM((tt, 128), F32),
                        pltpu.VMEM((1, RW_W), F32), pltpu.VMEM((1, RW_W), F32),
                        pltpu.VMEM((1, RW_W), F32), pltpu.VMEM((1, 128), F32),
                        pltpu.VMEM((1, 128), F32)],
        compiler_params=pltpu.CompilerParams(
            dimension_semantics=("parallel", "arbitrary"), vmem_limit_bytes=VMEM_LIMIT),
        name="rwkv_prompt",
    )(z, z, z, z, z, *mus, *params)


def _sample_prep_kernel(sm_ref, aup_ref, ab_ref, mi_ref, mf_ref, m_ref, ib_ref, fb_ref,
                        r_ref, k_ref, v_ref, wa_ref, gd_ref, sh_r, sh_k, sh_v, sh_wa, sh_g,
                        mur_ref, muk_ref, muv_ref, muwa_ref, mug_ref,
                        w0_ref, wup_ref, a0_ref, aup2_ref, gup_ref, kkw_ref, kaw_ref,
                        ela_ref, mnew_ref, wend_ref, ws_ref,
                        rr_ref, rk_ref, rv_ref, rkk_ref, rb_ref, rw_ref, rg_ref):
    la = _logsig(_dot(sm_ref[...], aup_ref[...]) + ab_ref[...]) * (1.0 / GLA_TAU)
    ela_ref[...] = jnp.exp(la)
    i_pre = mi_ref[...] + ib_ref[...]
    logf = _logsig(mf_ref[...] + fb_ref[...])
    m = m_ref[...]
    m_new = jnp.maximum(logf + m, i_pre)
    mnew_ref[...] = m_new
    wend_ref[...] = jnp.exp(logf + m - m_new)
    ws_ref[...] = jnp.exp(i_pre - m_new)
    mixf = lambda x, s, mu: x[...] + (s[...] - x[...]) * mu[...]
    zr = mixf(r_ref, sh_r, mur_ref)
    zk = mixf(k_ref, sh_k, muk_ref)
    zv = mixf(v_ref, sh_v, muv_ref)
    zwa = mixf(wa_ref, sh_wa, muwa_ref)
    zg = mixf(gd_ref, sh_g, mug_ref)
    lw, a, g, kk0, kmod = _rw_tokens(zr, zk, zv, zwa, zg, w0_ref[...], wup_ref[...], a0_ref[...],
                                     aup2_ref[...], gup_ref[...], kkw_ref[...], kaw_ref[...])
    ones_bd = _ones_bd()
    for p in range(RW_PAIRS):
        ln = slice(p * 128, (p + 1) * 128)
        kkp = kk0[:, ln]
        kkn = kkp * lax.rsqrt(_seg(kkp * kkp, ones_bd) + 1e-12)
        rkk_ref[:, ln] = kkn
        rb_ref[:, ln] = kkn * a[:, ln]
    rr_ref[...] = zr
    rk_ref[...] = kmod
    rv_ref[...] = zv
    rw_ref[...] = jnp.exp(lw)
    rg_ref[...] = g


def _gla_step_kernel(s_ref, q_ref, k_ref, e_ref, v_ref, sn_ref, o_ref):
    sn = e_ref[...] * s_ref[...] + k_ref[...] * v_ref[...]
    sn_ref[...] = sn
    o_ref[...] = jnp.sum((q_ref[...] * (GLA_DK ** -0.5)) * sn, axis=2, keepdims=True)


def _mlstm_step_kernel(c_ref, n_ref, qc_ref, kc_ref, qr_ref, kr_ref, v_ref, wend_ref, ws_ref, mnew_ref,
                       cn_ref, nn_ref, h_ref):
    wend = wend_ref[...]
    ws = ws_ref[...]
    cn = wend * c_ref[...] + (kc_ref[...] * ws) * v_ref[...]
    nn = wend * n_ref[...] + ws * kr_ref[...]
    cn_ref[...] = cn
    nn_ref[...] = nn
    scale = ML_DH ** -0.5
    num = jnp.sum((qc_ref[...] * scale) * cn, axis=2, keepdims=True)
    den = jnp.sum((qr_ref[...] * scale) * nn, axis=3, keepdims=True)
    h_ref[...] = num / jnp.maximum(jnp.abs(den), jnp.exp(-mnew_ref[...]))


def _rw_step_kernel(s_ref, kk_ref, b_ref, k_ref, r_ref, w_ref, v_ref, sn_ref, y_ref):
    s = s_ref[...]
    sa = jnp.sum(s * (-kk_ref[...]), axis=3, keepdims=True)
    sn = s * w_ref[...] + sa * b_ref[...] + v_ref[...] * k_ref[...]
    sn_ref[...] = sn
    y_ref[...] = jnp.sum(sn * r_ref[...], axis=3, keepdims=True)


def _sample_post_kernel(og_ref, gg_ref, gn_ref, hm_ref, mo_ref, mn_ref,
                        y_ref, r_ref, k_ref, v_ref, g_ref, rk_ref, ng_ref,
                        ogla_ref, oml_ref, orw_ref):
    for h in range(GLA_HEADS):
        vl = slice(h * GLA_DV, (h + 1) * GLA_DV)
        o = og_ref[:, vl]
        on = o * lax.rsqrt(jnp.mean(o * o, axis=-1, keepdims=True) + EPS) * gn_ref[...]
        gg = gg_ref[:, vl]
        ogla_ref[:, vl] = (on * (gg * _sigmoid(gg))).astype(BF16)
    for h in range(ML_HEADS):
        hl = slice(h * ML_DH, (h + 1) * ML_DH)
        hm = hm_ref[:, hl]
        y = hm - jnp.mean(hm, axis=-1, keepdims=True)
        y = y * lax.rsqrt(jnp.mean(y * y, axis=-1, keepdims=True) + EPS) * mn_ref[...]
        oml_ref[:, hl] = (y * _sigmoid(mo_ref[:, hl])).astype(BF16)
    ones_bd = _ones_bd()
    for p in range(RW_PAIRS):
        ln = slice(p * 128, (p + 1) * 128)
        orw_ref[:, ln] = _rw_epilogue(y_ref[:, ln], r_ref[:, ln], k_ref[:, ln], v_ref[:, ln],
                                      g_ref[:, ln], rk_ref[:, ln], ng_ref[:, ln], ones_bd)


def _full_call(kernel_fn, args, out_shapes, name):
    nd = lambda a: (lambda: (0,) * len(a.shape))
    return pl.pallas_call(
        kernel_fn,
        in_specs=[pl.BlockSpec(a.shape, nd(a)) for a in args],
        out_specs=[pl.BlockSpec(s.shape, nd(s)) for s in out_shapes],
        out_shape=out_shapes,
        compiler_params=pltpu.CompilerParams(vmem_limit_bytes=VMEM_LIMIT),
        name=name,
    )(*args)


def _batched_call(kernel_fn, args, out_shapes, bb, name):
    nb = args[0].shape[0]
    spec = lambda s: pl.BlockSpec((bb,) + tuple(s.shape[1:]),
                                  lambda i, n=len(s.shape): (i,) + (0,) * (n - 1))
    return pl.pallas_call(
        kernel_fn,
        grid=(nb // bb,),
        in_specs=[spec(a) for a in args],
        out_specs=[spec(s) for s in out_shapes],
        out_shape=out_shapes,
        compiler_params=pltpu.CompilerParams(
            dimension_semantics=("parallel",), vmem_limit_bytes=VMEM_LIMIT),
        name=name,
    )(*args)


def _prep_layer_params(l, p):
    w = p["w_in"][l]
    w_in_p = jnp.concatenate(
        [w[:, 0:2048], w[:, 2064:3088], w[:, 3088:6160], w[:, 6168:7192], w[:, 7192:13592],
         w[:, 2048:2064], w[:, 6160:6168], jnp.zeros((D_MODEL, NP - 13592), F32)], axis=1).astype(BF16)
    aup_pad = jnp.zeros((128, GLA_QK), F32).at[0:GLA_RANK].set(p["gla_a_up"][l])
    brow = (jnp.zeros((1, 128), F32).at[0, SM_I:SM_I + 4].set(p["ml_i_bias"][l])
            .at[0, SM_F:SM_F + 4].set(p["ml_f_bias"][l]))
    bcol = jnp.broadcast_to(jnp.concatenate([p["ml_i_bias"][l], p["ml_f_bias"][l]])[:, None], (8, CHUNK))
    mu = p["rw_mu"][l][None, :]
    mus = (mu[:, 0:1024], mu[:, 1024:2048], mu[:, 2048:3072], mu[:, 3072:3200], mu[:, 3200:3328])
    wup_pad = jnp.zeros((128, RW_W), F32).at[0:64].set(p["rw_w_up"][l])
    aup2_pad = jnp.zeros((128, RW_W), F32).at[64:128].set(p["rw_a_up"][l])
    row = lambda a: a.reshape(1, -1)
    rw_params = (row(p["rw_w0"][l]), wup_pad, row(p["rw_a0"][l]), aup2_pad, p["rw_g_up"][l],
                 row(p["rw_k_k"][l]), row(p["rw_k_a"][l]), row(p["rw_r_k"][l]), row(p["rw_norm_g"][l]))
    return dict(
        norm1_g=row(p["norm1_g"][l]), w_in=w_in_p, aup_pad=aup_pad, ab=row(p["gla_a_bias"][l]),
        gla_ng=row(p["gla_norm_g"][l]), brow=brow, bcol=bcol, ib=row(p["ml_i_bias"][l]),
        fb=row(p["ml_f_bias"][l]), ml_ng=row(p["ml_norm_g"][l]), mus=mus, rw=rw_params,
        wb=p["w_branch"][l].astype(BF16), wo=p["w_out"][l].astype(BF16), norm2_g=row(p["norm2_g"][l]),
        w1=p["w_ff1"][l].astype(BF16), w2=p["w_ff2"][l].astype(BF16))


def _prompt_layer(x, lp, fg, final, b_, t_):
    tm = min(512, x.shape[0])
    z = _proj_in(x, lp["norm1_g"], lp["w_in"], tm)
    tt = min(256, t_)
    o_gla, gla_new = _gla_prompt(z, lp["aup_pad"], lp["ab"], lp["gla_ng"], b_, t_, tt)
    sm = z[:, OFF_SM + SM_I:OFF_SM + SM_I + 8]
    smt = sm.reshape(b_, t_ // CHUNK, CHUNK, 8).transpose(0, 1, 3, 2)
    o_ml, c_new, n_new, m_new = _mlstm_prompt(z, smt, lp["brow"], lp["bcol"], lp["ml_ng"], b_, t_, tt)
    o_rw, rw_new = _rw_prompt(z, lp["mus"], lp["rw"], b_, t_, tt)
    x = _merge(o_gla, o_ml, o_rw, z, x, lp["wb"], lp["wo"], tm)
    x = _ffn(x, lp["norm2_g"], lp["w1"], lp["w2"], fg, final, tm)
    shift = z.reshape(b_, t_, NP)[:, t_ - 1, OFF_RW:OFF_RW + RW_COLS]
    return x, (gla_new, c_new, n_new, m_new[:, :, 0], rw_new, shift)


def _sample_layer(x, st, lp, fg, final):
    gla_s, ml_c, ml_n, ml_m, rw_s, shift = st
    nb = x.shape[0]
    z = _proj_in(x, lp["norm1_g"], lp["w_in"], nb)
    zc = lambda off, w: z[:, off:off + w]
    sh = (shift[:, 0:1024], shift[:, 1024:2048], shift[:, 2048:3072], shift[:, 3072:3200], shift[:, 3200:3328])
    w0, wup, a0, aup2, gup, kkw, kaw, rk, ng = lp["rw"]
    sds = lambda *s: jax.ShapeDtypeStruct(s, F32)
    prep_args = [zc(OFF_SM, 128), lp["aup_pad"], lp["ab"],
                 zc(OFF_SM + SM_I, 4), zc(OFF_SM + SM_F, 4), ml_m, lp["ib"], lp["fb"],
                 zc(OFF_RW, 1024), zc(OFF_RW + 1024, 1024), zc(OFF_RW + 2048, 1024),
                 zc(OFF_RW + 3072, 128), zc(OFF_RW + 3200, 128), *sh, *lp["mus"],
                 w0, wup, a0, aup2, gup, kkw, kaw]
    (ela, m_new, wend, ws, rr, rkm, rv, rkk, rb, rw, rg) = _full_call(
        _sample_prep_kernel, prep_args,
        [sds(nb, GLA_QK), sds(nb, 4), sds(nb, 4), sds(nb, 4)] + [sds(nb, RW_W)] * 7, "sample_prep")

    col = lambda a, h, d: a.reshape(nb, h, d, 1)
    rowv = lambda a, h, d: a.reshape(nb, h, 1, d)
    q = zc(OFF_GQ, GLA_QK)
    gla_new, o_g = _batched_call(
        _gla_step_kernel,
        [gla_s, col(q, 4, GLA_DK), col(zc(OFF_GK, GLA_QK), 4, GLA_DK), col(ela, 4, GLA_DK),
         rowv(zc(OFF_GV, GLA_V), 4, GLA_DV)],
        [sds(*gla_s.shape), sds(nb, 4, 1, GLA_DV)], 8, "gla_step")

    mq = zc(OFF_MQ, ML_W)
    mk = zc(OFF_MK, ML_W)
    sc = lambda a: a.reshape(nb, 4, 1, 1)
    c_new, n_new, hm = _batched_call(
        _mlstm_step_kernel,
        [ml_c, rowv(ml_n, 4, ML_DH), col(mq, 4, ML_DH), col(mk, 4, ML_DH), rowv(mq, 4, ML_DH),
         rowv(mk, 4, ML_DH), rowv(zc(OFF_MV, ML_W), 4, ML_DH), sc(wend), sc(ws), sc(m_new)],
        [sds(*ml_c.shape), sds(nb, 4, 1, ML_DH), sds(nb, 4, 1, ML_DH)], 4, "mlstm_step")

    hrow = lambda a: a.reshape(nb, RW_HEADS, 1, RW_HEAD)
    rw_new, y = _batched_call(
        _rw_step_kernel,
        [rw_s, hrow(rkk), hrow(rb), hrow(rkm), hrow(rr), hrow(rw), rv.reshape(nb, RW_HEADS, RW_HEAD, 1)],
        [sds(*rw_s.shape), sds(nb, RW_HEADS, RW_HEAD, 1)], 8, "rwkv_step")

    bf = lambda: jax.ShapeDtypeStruct((nb, D_MODEL), BF16)
    o_gla, o_ml, o_rw = _full_call(
        _sample_post_kernel,
        [o_g.reshape(nb, GLA_V), zc(OFF_GG, GLA_V), lp["gla_ng"],
         hm.reshape(nb, ML_W), zc(OFF_MO, ML_W), lp["ml_ng"],
         y.reshape(nb, RW_W), rr, rkm, rv, rg, rk, ng],
        [bf(), bf(), bf()], "sample_post")
    x = _merge(o_gla, o_ml, o_rw, z, x, lp["wb"], lp["wo"], nb)
    x = _ffn(x, lp["norm2_g"], lp["w1"], lp["w2"], fg, final, nb)
    return x, (gla_new, c_new, n_new.reshape(nb, 4, ML_DH), m_new, rw_new, zc(OFF_RW, RW_COLS))


def kernel(x_prompt, x_sample, state_gla, state_mlstm_c, state_mlstm_n, state_mlstm_m, state_rwkv, state_shift, norm1_g, w_in, gla_a_up, gla_a_bias, gla_norm_g, ml_i_bias, ml_f_bias, ml_norm_g, rw_mu, rw_w0, rw_w_up, rw_a0, rw_a_up, rw_g_up, rw_k_k, rw_k_a, rw_r_k, rw_norm_g, w_branch, w_out, norm2_g, w_ff1, w_ff2, final_g):
    p = dict(norm1_g=norm1_g, w_in=w_in, gla_a_up=gla_a_up, gla_a_bias=gla_a_bias, gla_norm_g=gla_norm_g,
             ml_i_bias=ml_i_bias, ml_f_bias=ml_f_bias, ml_norm_g=ml_norm_g, rw_mu=rw_mu, rw_w0=rw_w0,
             rw_w_up=rw_w_up, rw_a0=rw_a0, rw_a_up=rw_a_up, rw_g_up=rw_g_up, rw_k_k=rw_k_k, rw_k_a=rw_k_a,
             rw_r_k=rw_r_k, rw_norm_g=rw_norm_g, w_branch=w_branch, w_out=w_out, norm2_g=norm2_g,
             w_ff1=w_ff1, w_ff2=w_ff2)
    depth = w_in.shape[0]
    b_, t_, _ = x_prompt.shape
    nb = x_sample.shape[0]
    fg = final_g.reshape(1, D_MODEL)
    xp = x_prompt.reshape(b_ * t_, D_MODEL)
    xs = x_sample.reshape(nb, D_MODEL)
    p_states, s_states = [], []
    for l in range(depth):
        lp = _prep_layer_params(l, p)
        final = l == depth - 1
        xp, stp = _prompt_layer(xp, lp, fg, final, b_, t_)
        st = (state_gla[l], state_mlstm_c[l], state_mlstm_n[l], state_mlstm_m[l], state_rwkv[l], state_shift[l])
        xs, sts = _sample_layer(xs, st, lp, fg, final)
        p_states.append(stp)
        s_states.append(sts)
    ps = [jnp.stack([s[i] for s in p_states]) for i in range(6)]
    ss = [jnp.stack([s[i] for s in s_states]) for i in range(6)]
    return (xp.reshape(b_, t_, D_MODEL), xs.reshape(nb, 1, D_MODEL),
            ps[0], ss[0], ps[1], ss[1], ps[2], ss[2], ps[3], ss[3], ps[4], ss[4], ps[5], ss[5])
```

```python
import functools

import jax
import jax.numpy as jnp
from jax import lax
from jax.experimental import pallas as pl
from jax.experimental.pallas import tpu as pltpu

F32 = jnp.float32
BF16 = jnp.bfloat16

D_MODEL = 1024
GLA_HEADS, GLA_DK, GLA_DV = 4, 128, 256
GLA_QK, GLA_V, GLA_RANK, GLA_TAU = 512, 1024, 16, 16.0
ML_HEADS, ML_DH, ML_W = 4, 256, 1024
RW_HEAD, RW_W, RW_HEADS = 64, 1024, 16
RW_PAIRS = RW_HEADS // 2
RW_COLS = 3 * RW_W + 64 + 64 + 128
D_FF = 4 * D_MODEL
EPS = 1e-6
CHUNK = 64

OFF_GQ, OFF_GK, OFF_GV, OFF_GG = 0, 512, 1024, 2048
OFF_MQ, OFF_MK, OFF_MV, OFF_MO = 3072, 4096, 5120, 6144
OFF_GATE = 7168
OFF_RW = 10240
OFF_SM = 13568
NP = 13824
SM_I, SM_F = 16, 20

VMEM_LIMIT = 48 * 1024 * 1024


def _dot(a, b):
    return jnp.dot(a.astype(BF16), b.astype(BF16), preferred_element_type=F32)


def _dot_nt(a, b):
    return lax.dot_general(a.astype(BF16), b.astype(BF16), (((1,), (1,)), ((), ())),
                           preferred_element_type=F32)


def _dot_tn(a, b):
    return lax.dot_general(a.astype(BF16), b.astype(BF16), (((0,), (0,)), ((), ())),
                           preferred_element_type=F32)


def _split(x):
    hi = x.astype(BF16)
    lo = (x - hi.astype(F32)).astype(BF16)
    return hi, lo


def _dot2(x, w_exact):
    hi, lo = _split(x)
    return (jnp.dot(hi, w_exact, preferred_element_type=F32)
            + jnp.dot(lo, w_exact, preferred_element_type=F32))


def _dot2_tn(x, w_exact):
    hi, lo = _split(x)
    dn = (((0,), (0,)), ((), ()))
    return (lax.dot_general(hi, w_exact, dn, preferred_element_type=F32)
            + lax.dot_general(lo, w_exact, dn, preferred_element_type=F32))


def _dot2_left(w_exact, x):
    hi, lo = _split(x)
    return (jnp.dot(w_exact, hi, preferred_element_type=F32)
            + jnp.dot(w_exact, lo, preferred_element_type=F32))


def _logsig(x):
    return jnp.minimum(x, 0.0) - jnp.log1p(jnp.exp(-jnp.abs(x)))


def _sigmoid(x):
    return jax.nn.sigmoid(x)


def _rms(x, g):
    return x * lax.rsqrt(jnp.mean(x * x, axis=-1, keepdims=True) + EPS) * g


def _tri(n, m=None):
    m = n if m is None else m
    r = lax.broadcasted_iota(jnp.int32, (n, m), 0)
    c = lax.broadcasted_iota(jnp.int32, (n, m), 1)
    return r, c


def _proj_in_kernel(x_ref, g_ref, w_ref, o_ref, h_ref):
    @pl.when(pl.program_id(1) == 0)
    def _():
        h_ref[...] = _rms(x_ref[...], g_ref[...]).astype(BF16)

    o_ref[...] = jnp.dot(h_ref[...], w_ref[...], preferred_element_type=F32)


def _proj_in(x, g, w, tm, tn=1536):
    m = x.shape[0]
    return pl.pallas_call(
        _proj_in_kernel,
        grid=(m // tm, NP // tn),
        in_specs=[pl.BlockSpec((tm, D_MODEL), lambda i, j: (i, 0)),
                  pl.BlockSpec((1, D_MODEL), lambda i, j: (0, 0)),
                  pl.BlockSpec((D_MODEL, tn), lambda i, j: (0, j))],
        out_specs=pl.BlockSpec((tm, tn), lambda i, j: (i, j)),
        out_shape=jax.ShapeDtypeStruct((m, NP), F32),
        scratch_shapes=[pltpu.VMEM((tm, D_MODEL), BF16)],
        compiler_params=pltpu.CompilerParams(
            dimension_semantics=("parallel", "arbitrary"), vmem_limit_bytes=VMEM_LIMIT),
        name="proj_in",
    )(x, g, w)


def _merge_kernel(og_ref, om_ref, or_ref, g0_ref, g1_ref, g2_ref, x_ref, wb_ref, wo_ref, o_ref):
    acc = jnp.dot(og_ref[...], wb_ref[0], preferred_element_type=F32) * _sigmoid(g0_ref[...])
    acc += jnp.dot(om_ref[...], wb_ref[1], preferred_element_type=F32) * _sigmoid(g1_ref[...])
    acc += jnp.dot(or_ref[...], wb_ref[2], preferred_element_type=F32) * _sigmoid(g2_ref[...])
    o_ref[...] = x_ref[...] + jnp.dot(acc.astype(BF16), wo_ref[...], preferred_element_type=F32)


def _merge(o_gla, o_ml, o_rw, z, x, wb, wo, tm):
    m = x.shape[0]
    row = lambda i: (i, 0)
    gate_blk = OFF_GATE // D_MODEL
    return pl.pallas_call(
        _merge_kernel,
        grid=(m // tm,),
        in_specs=[pl.BlockSpec((tm, D_MODEL), row),
                  pl.BlockSpec((tm, D_MODEL), row),
                  pl.BlockSpec((tm, D_MODEL), row),
                  pl.BlockSpec((tm, D_MODEL), lambda i: (i, gate_blk)),
                  pl.BlockSpec((tm, D_MODEL), lambda i: (i, gate_blk + 1)),
                  pl.BlockSpec((tm, D_MODEL), lambda i: (i, gate_blk + 2)),
                  pl.BlockSpec((tm, D_MODEL), row),
                  pl.BlockSpec((3, D_MODEL, D_MODEL), lambda i: (0, 0, 0)),
                  pl.BlockSpec((D_MODEL, D_MODEL), lambda i: (0, 0))],
        out_specs=pl.BlockSpec((tm, D_MODEL), row),
        out_shape=jax.ShapeDtypeStruct((m, D_MODEL), F32),
        compiler_params=pltpu.CompilerParams(
            dimension_semantics=("parallel",), vmem_limit_bytes=VMEM_LIMIT),
        name="merge",
    )(o_gla, o_ml, o_rw, z, z, z, x, wb, wo)


def _ffn_kernel(x_ref, g_ref, w1_ref, w2_ref, fg_ref, o_ref, h_ref, acc_ref, *, final):
    f = pl.program_id(1)

    @pl.when(f == 0)
    def _():
        h_ref[...] = _rms(x_ref[...], g_ref[...]).astype(BF16)
        acc_ref[...] = jnp.zeros_like(acc_ref)

    a = jnp.dot(h_ref[...], w1_ref[...], preferred_element_type=F32)
    a = jnp.square(jnp.maximum(a, 0.0))
    acc_ref[...] += jnp.dot(a.astype(BF16), w2_ref[...], preferred_element_type=F32)

    @pl.when(f == pl.num_programs(1) - 1)
    def _():
        y = x_ref[...] + acc_ref[...]
        if final:
            y = _rms(y, fg_ref[...])
        o_ref[...] = y


def _ffn(x, g, w1, w2, fg, final, tm, tf=512):
    m = x.shape[0]
    return pl.pallas_call(
        functools.partial(_ffn_kernel, final=final),
        grid=(m // tm, D_FF // tf),
        in_specs=[pl.BlockSpec((tm, D_MODEL), lambda i, f: (i, 0)),
                  pl.BlockSpec((1, D_MODEL), lambda i, f: (0, 0)),
                  pl.BlockSpec((D_MODEL, tf), lambda i, f: (0, f)),
                  pl.BlockSpec((tf, D_MODEL), lambda i, f: (f, 0)),
                  pl.BlockSpec((1, D_MODEL), lambda i, f: (0, 0))],
        out_specs=pl.BlockSpec((tm, D_MODEL), lambda i, f: (i, 0)),
        out_shape=jax.ShapeDtypeStruct((m, D_MODEL), F32),
        scratch_shapes=[pltpu.VMEM((tm, D_MODEL), BF16), pltpu.VMEM((tm, D_MODEL), F32)],
        compiler_params=pltpu.CompilerParams(
            dimension_semantics=("parallel", "arbitrary"), vmem_limit_bytes=VMEM_LIMIT),
        name="ffn",
    )(x, g, w1, w2, fg)


def _gla_prompt_kernel(q_ref, k_ref, v_ref, gg_ref, sm_ref, aup_ref, ab_ref, gn_ref,
                       o_ref, sout_ref, s_ref, *, n_chunks):
    t = pl.program_id(1)

    @pl.when(t == 0)
    def _():
        s_ref[...] = jnp.zeros_like(s_ref)

    c_ = CHUNK
    r, c = _tri(c_)
    incl = r >= c
    tril = jnp.where(incl, 1.0, 0.0).astype(BF16)
    ones_c = jnp.ones((c_, GLA_DK), BF16)

    def chunk(ci, carry):
        rows = pl.ds(pl.multiple_of(ci * c_, c_), c_)
        x = _dot(sm_ref[rows, :], aup_ref[...]) + ab_ref[...]
        la = _logsig(x) * (1.0 / GLA_TAU)
        b = _dot2_left(tril, la)
        dec = jnp.exp(_dot2_tn(la, ones_c))
        q = q_ref[rows, :] * (GLA_DK ** -0.5)
        k = k_ref[rows, :]
        qe = q * jnp.exp(b)
        ke = k * jnp.exp(-b)
        kd = k * jnp.exp(b[c_ - 1:c_, :] - b)
        hs = range(GLA_HEADS)
        kls = [slice(h * GLA_DK, (h + 1) * GLA_DK) for h in hs]
        vls = [slice(h * GLA_DV, (h + 1) * GLA_DV) for h in hs]
        v = [v_ref[rows, vl] for vl in vls]
        att = [jnp.where(incl, _dot_nt(qe[:, kl], ke[:, kl]), 0.0) for kl in kls]
        s = [s_ref[h] for h in hs]
        oi = [_dot(qe[:, kls[h]], s[h]) for h in hs]
        kv = [_dot_tn(kd[:, kls[h]], v[h]) for h in hs]
        o = [oi[h] + _dot(att[h], v[h]) for h in hs]
        for h in hs:
            dh = dec[kls[h], :]
            s_ref[h] = s[h] * jnp.concatenate([dh, dh], axis=1) + kv[h]
        for h in hs:
            on = o[h] * lax.rsqrt(jnp.mean(o[h] * o[h], axis=-1, keepdims=True) + EPS) * gn_ref[...]
            gg = gg_ref[rows, vls[h]]
            o_ref[rows, vls[h]] = (on * (gg * _sigmoid(gg))).astype(BF16)
        return carry

    lax.fori_loop(0, n_chunks, chunk, 0)

    @pl.when(t == pl.num_programs(1) - 1)
    def _():
        sout_ref[0] = s_ref[...]


def _gla_prompt(z, aup_pad, ab, gn, b_, t_, tt):
    nt = t_ // tt
    row = lambda blk: (lambda b, t: (b * nt + t, blk))
    const = lambda b, t: (0, 0)
    return pl.pallas_call(
        functools.partial(_gla_prompt_kernel, n_chunks=tt // CHUNK),
        grid=(b_, nt),
        in_specs=[pl.BlockSpec((tt, GLA_QK), row(OFF_GQ // GLA_QK)),
                  pl.BlockSpec((tt, GLA_QK), row(OFF_GK // GLA_QK)),
                  pl.BlockSpec((tt, GLA_V), row(OFF_GV // GLA_V)),
                  pl.BlockSpec((tt, GLA_V), row(OFF_GG // GLA_V)),
                  pl.BlockSpec((tt, 128), row(OFF_SM // 128)),
                  pl.BlockSpec((128, GLA_QK), const),
                  pl.BlockSpec((1, GLA_QK), const),
                  pl.BlockSpec((1, GLA_DV), const)],
        out_specs=[pl.BlockSpec((tt, GLA_V), lambda b, t: (b * nt + t, 0)),
                   pl.BlockSpec((1, GLA_HEADS, GLA_DK, GLA_DV), lambda b, t: (b, 0, 0, 0))],
        out_shape=[jax.ShapeDtypeStruct((b_ * t_, GLA_V), BF16),
                   jax.ShapeDtypeStruct((b_, GLA_HEADS, GLA_DK, GLA_DV), F32)],
        scratch_shapes=[pltpu.VMEM((GLA_HEADS, GLA_DK, GLA_DV), F32)],
        compiler_params=pltpu.CompilerParams(
            dimension_semantics=("parallel", "arbitrary"), vmem_limit_bytes=VMEM_LIMIT),
        name="gla_prompt",
    )(z, z, z, z, z, aup_pad, ab, gn)


def _mlstm_prompt_kernel(q_ref, k_ref, v_ref, mo_ref, sm_ref, smt_ref, brow_ref, bcol_ref, ng_ref,
                         o_ref, cout_ref, nout_ref, mout_ref, c_ref, n_ref, m_ref, *, n_chunks):
    t = pl.program_id(1)

    @pl.when(t == 0)
    def _():
        c_ref[...] = jnp.zeros_like(c_ref)
        n_ref[...] = jnp.zeros_like(n_ref)
        m_ref[...] = jnp.zeros_like(m_ref)

    c_ = CHUNK
    r, c = _tri(c_)
    incl = r >= c
    tril = jnp.where(incl, 1.0, 0.0).astype(BF16)
    triu = jnp.where(r <= c, 1.0, 0.0).astype(BF16)
    sub8 = lax.broadcasted_iota(jnp.int32, (8, c_), 0)

    def chunk(ci, carry):
        rows = pl.ds(pl.multiple_of(ci * c_, c_), c_)
        sm = sm_ref[rows, :] + brow_ref[...]
        fcol = _dot2_left(tril, _logsig(sm))
        smt = smt_ref[0, ci] + bcol_ref[...]
        frow = _dot2(jnp.where(sub8 >= 4, _logsig(smt), 0.0), triu)
        hs = range(ML_HEADS)
        hls = [slice(h * ML_DH, (h + 1) * ML_DH) for h in hs]
        f_t = [fcol[:, SM_F + h:SM_F + h + 1] for h in hs]
        i_t = [sm[:, SM_I + h:SM_I + h + 1] for h in hs]
        f_s = [frow[4 + h:5 + h, :] for h in hs]
        i_s = [smt[h:h + 1, :] for h in hs]
        m = [m_ref[h][0:1, 0:1] for h in hs]
        q = [q_ref[rows, hl] * (ML_DH ** -0.5) for hl in hls]
        k = [k_ref[rows, hl] for hl in hls]
        v = [v_ref[rows, hl] for hl in hls]
        cm = [c_ref[h] for h in hs]
        n8 = [n_ref[h] for h in hs]
        qk0 = [_dot_nt(q[h], k[h]) for h in hs]
        qc = [_dot(q[h], cm[h]) for h in hs]
        qn = [_dot_nt(q[h], n8[h])[:, 0:1] for h in hs]
        gmax = [jnp.max(jnp.where(incl, i_s[h] - f_s[h], -jnp.inf), axis=1, keepdims=True) for h in hs]
        mt = [jnp.maximum(f_t[h] + m[h], f_t[h] + gmax[h]) for h in hs]
        w_inter = [jnp.exp(f_t[h] + m[h] - mt[h]) for h in hs]
        d = [jnp.exp(jnp.where(incl, (f_t[h] - mt[h]) + (i_s[h] - f_s[h]), -jnp.inf)) for h in hs]
        qk = [qk0[h] * d[h] for h in hs]
        qkv = [_dot(qk[h], v[h]) for h in hs]
        m_new = [mt[h][c_ - 1:c_, :] for h in hs]
        f_last = [f_t[h][c_ - 1:c_, :] for h in hs]
        w_end = [jnp.exp(f_last[h] + m[h] - m_new[h]) for h in hs]
        kws = [k[h] * jnp.exp(f_last[h] - f_t[h] + i_t[h] - m_new[h]) for h in hs]
        kwv = [_dot_tn(kws[h], v[h]) for h in hs]
        for h in hs:
            c_ref[h] = w_end[h] * cm[h] + kwv[h]
            n_ref[h] = w_end[h] * n8[h] + jnp.broadcast_to(jnp.sum(kws[h], axis=0, keepdims=True), (8, ML_DH))
            m_ref[h] = jnp.broadcast_to(m_new[h], (8, 128))
        for h in hs:
            num = w_inter[h] * qc[h] + qkv[h]
            den = w_inter[h] * qn[h] + jnp.sum(qk[h], axis=1, keepdims=True)
            hm = num / jnp.maximum(jnp.abs(den), jnp.exp(-mt[h]))
            y = hm - jnp.mean(hm, axis=-1, keepdims=True)
            y = y * lax.rsqrt(jnp.mean(y * y, axis=-1, keepdims=True) + EPS) * ng_ref[...]
            o_ref[rows, hls[h]] = (y * _sigmoid(mo_ref[rows, hls[h]])).astype(BF16)
        return carry

    lax.fori_loop(0, n_chunks, chunk, 0)

    @pl.when(t == pl.num_programs(1) - 1)
    def _():
        cout_ref[0] = c_ref[...]
        for h in range(ML_HEADS):
            nout_ref[0, h:h + 1, :] = n_ref[h][0:1, :]
            mout_ref[0, h:h + 1, :] = m_ref[h][0:1, :]


def _mlstm_prompt(z, smt, brow, bcol, ng, b_, t_, tt):
    nt = t_ // tt
    nc = tt // CHUNK
    row = lambda blk: (lambda b, t: (b * nt + t, blk))
    const = lambda b, t: (0, 0)
    return pl.pallas_call(
        functools.partial(_mlstm_prompt_kernel, n_chunks=nc),
        grid=(b_, nt),
        in_specs=[pl.BlockSpec((tt, ML_W), row(OFF_MQ // ML_W)),
                  pl.BlockSpec((tt, ML_W), row(OFF_MK // ML_W)),
                  pl.BlockSpec((tt, ML_W), row(OFF_MV // ML_W)),
                  pl.BlockSpec((tt, ML_W), row(OFF_MO // ML_W)),
                  pl.BlockSpec((tt, 128), row(OFF_SM // 128)),
                  pl.BlockSpec((1, nc, 8, CHUNK), lambda b, t: (b, t, 0, 0)),
                  pl.BlockSpec((1, 128), const),
                  pl.BlockSpec((8, CHUNK), const),
                  pl.BlockSpec((1, ML_DH), const)],
        out_specs=[pl.BlockSpec((tt, ML_W), lambda b, t: (b * nt + t, 0)),
                   pl.BlockSpec((1, ML_HEADS, ML_DH, ML_DH), lambda b, t: (b, 0, 0, 0)),
                   pl.BlockSpec((1, ML_HEADS, ML_DH), lambda b, t: (b, 0, 0)),
                   pl.BlockSpec((1, ML_HEADS, 128), lambda b, t: (b, 0, 0))],
        out_shape=[jax.ShapeDtypeStruct((b_ * t_, ML_W), BF16),
                   jax.ShapeDtypeStruct((b_, ML_HEADS, ML_DH, ML_DH), F32),
                   jax.ShapeDtypeStruct((b_, ML_HEADS, ML_DH), F32),
                   jax.ShapeDtypeStruct((b_, ML_HEADS, 128), F32)],
        scratch_shapes=[pltpu.VMEM((ML_HEADS, ML_DH, ML_DH), F32),
                        pltpu.VMEM((ML_HEADS, 8, ML_DH), F32),
                        pltpu.VMEM((ML_HEADS, 8, 128), F32)],
        compiler_params=pltpu.CompilerParams(
            dimension_semantics=("parallel", "arbitrary"), vmem_limit_bytes=VMEM_LIMIT),
        name="mlstm_prompt",
    )(z, z, z, z, z, smt, brow, bcol, ng)


def _rw_tokens(zr, zk, zv, zwa, zg, w0, wup, a0, aup, gup, kkw, kaw):
    u = w0 + _dot(jnp.tanh(zwa), wup)
    lw = -jnp.exp(_logsig(u) - 0.5)
    a = _sigmoid(a0 + _dot(zwa, aup))
    g = _dot(_sigmoid(zg), gup)
    kk0 = zk * kkw
    kmod = zk * (1.0 + (a - 1.0) * kaw)
    return lw, a, g, kk0, kmod


def _ones_bd():
    r, c = _tri(128)
    return jnp.where((r // RW_HEAD) == (c // RW_HEAD), 1.0, 0.0).astype(BF16)


def _seg(x, ones_bd):
    return _dot2(x, ones_bd)


def _rw_epilogue(y, r, kmod, v, g, rk, ng, ones_bd):
    mean = _seg(y, ones_bd) * (1.0 / RW_HEAD)
    yc = y - mean
    var = _seg(yc * yc, ones_bd) * (1.0 / RW_HEAD)
    yn = yc * lax.rsqrt(var + EPS) * ng
    bonus = _seg(r * kmod * rk, ones_bd) * v
    return ((yn + bonus) * g).astype(BF16)


def _rw_prompt_kernel(r_ref, k_ref, v_ref, wa_ref, gd_ref,
                      mur_ref, muk_ref, muv_ref, muwa_ref, mug_ref,
                      w0_ref, wup_ref, a0_ref, aup_ref, gup_ref, kkw_ref, kaw_ref, rk_ref, ng_ref,
                      o_ref, sout_ref,
                      s_ref, zr_s, zk_s, zv_s, zwa_s, zg_s, pr_s, pk_s, pv_s, pwa_s, pg_s,
                      *, n_chunks, tt):
    t = pl.program_id(1)

    @pl.when(t == 0)
    def _():
        s_ref[...] = jnp.zeros_like(s_ref)
        for p_ in (pr_s, pk_s, pv_s, pwa_s, pg_s):
            p_[...] = jnp.zeros_like(p_)

    def mix(x_ref, prev_ref, mu_ref, dst_ref):
        x = x_ref[...]
        rid = lax.broadcasted_iota(jnp.int32, x.shape, 0)
        prev = jnp.where(rid == 0, prev_ref[...], pltpu.roll(x, 1, 0))
        dst_ref[...] = x + (prev - x) * mu_ref[...]
        prev_ref[...] = x[tt - 1:tt, :]

    mix(r_ref, pr_s, mur_ref, zr_s)
    mix(k_ref, pk_s, muk_ref, zk_s)
    mix(v_ref, pv_s, muv_ref, zv_s)
    mix(wa_ref, pwa_s, muwa_ref, zwa_s)
    mix(gd_ref, pg_s, mug_ref, zg_s)

    c_ = CHUNK
    r_, c2 = _tri(c_, 128)
    s_idx = c2 % c_
    strict_pk = s_idx < r_
    incl_pk = s_idx <= r_
    eye_pk = jnp.where(s_idx == r_, 1.0, 0.0)
    rr, cc = _tri(c_)
    tril = jnp.where(rr >= cc, 1.0, 0.0).astype(BF16)
    lane = lax.broadcasted_iota(jnp.int32, (c_, 128), 1)
    lo_half = lane < RW_HEAD
    r128, c128 = _tri(128)
    bdmask = (r128 // RW_HEAD) == (c128 // RW_HEAD)
    ones_bd = jnp.where(bdmask, 1.0, 0.0).astype(BF16)

    def bd(y):
        yb = y.astype(BF16)
        zero = jnp.zeros_like(yb)
        return jnp.concatenate([jnp.where(lo_half, yb, zero), jnp.where(lo_half, zero, yb)], axis=0)

    def pk_mul(x, y):
        return jnp.dot(x.astype(BF16), bd(y), preferred_element_type=F32)

    def seg_all(xs):
        st = _dot2(jnp.concatenate(xs, axis=0), ones_bd)
        return [st[i * c_:(i + 1) * c_] for i in range(len(xs))]

    def chunk(ci, carry):
        rows = pl.ds(pl.multiple_of(ci * c_, c_), c_)
        zr = zr_s[rows, :]
        zk = zk_s[rows, :]
        zv = zv_s[rows, :]
        lw, a, g, kk0, kmod = _rw_tokens(zr, zk, zv, zwa_s[rows, :], zg_s[rows, :],
                                         w0_ref[...], wup_ref[...], a0_ref[...], aup_ref[...],
                                         gup_ref[...], kkw_ref[...], kaw_ref[...])
        prs = range(RW_PAIRS)
        lns = [slice(p * 128, (p + 1) * 128) for p in prs]
        kkp = [kk0[:, ln] for ln in lns]
        ssq = seg_all([x * x for x in kkp])
        cum = [_dot2_left(tril, lw[:, ln]) for ln in lns]
        kkn = [kkp[p] * lax.rsqrt(ssq[p] + 1e-12) for p in prs]
        bp = [kkn[p] * a[:, lns[p]] for p in prs]
        rp = [zr[:, ln] for ln in lns]
        kp = [kmod[:, ln] for ln in lns]
        vp = [zv[:, ln] for ln in lns]
        cl = [cum[p][c_ - 1:c_, :] for p in prs]
        e_neg = [jnp.exp(-cum[p]) for p in prs]
        e_end = [jnp.exp(cl[p] - cum[p]) for p in prs]
        rh = [rp[p] * jnp.exp(cum[p]) for p in prs]
        kh = [kkn[p] * jnp.exp(cum[p] - lw[:, lns[p]]) for p in prs]
        kt = [kp[p] * e_neg[p] for p in prs]
        bt = [bp[p] * e_neg[p] for p in prs]
        kg = [kp[p] * e_end[p] for p in prs]
        bg = [bp[p] * e_end[p] for p in prs]
        nt = (((1,), (1,)), ((), ()))
        kr = [jnp.concatenate([kh[p], rh[p]], axis=0).astype(BF16) for p in prs]
        ktb = [jnp.concatenate([bd(kt[p]), bd(bt[p])], axis=0) for p in prs]
        pkb = [lax.dot_general(kr[p], ktb[p], nt, preferred_element_type=F32) for p in prs]
        ak = [jnp.where(strict_pk, pkb[p][0:c_, 0:128], 0.0) for p in prs]
        gk = [jnp.where(incl_pk, pkb[p][c_:2 * c_, 0:128], 0.0) for p in prs]
        gb = [jnp.where(incl_pk, pkb[p][c_:2 * c_, 128:256], 0.0) for p in prs]
        pw = [jnp.where(strict_pk, -pkb[p][0:c_, 128:256], 0.0) for p in prs]
        tm = [eye_pk + pw[p] for p in prs]
        agv = [pk_mul(jnp.concatenate([ak[p], gk[p]], axis=0), vp[p]) for p in prs]
        pw = [pk_mul(pw[p], pw[p]) for p in prs]
        for _ in range(4):
            x = [pk_mul(jnp.concatenate([pw[p], tm[p]], axis=0), pw[p]) for p in prs]
            tm = [tm[p] + x[p][c_:2 * c_] for p in prs]
            pw = [x[p][0:c_] for p in prs]
        tm = [tm[p] + pk_mul(tm[p], pw[p]) for p in prs]
        ku = [jnp.dot(tm[p].astype(BF16), jnp.concatenate([bd(kh[p]), bd(agv[p][0:c_])], axis=1),
                      preferred_element_type=F32) for p in prs]
        gku = [jnp.dot(gb[p].astype(BF16),
                       jnp.concatenate([bd(ku[p][:, 0:128]), bd(ku[p][:, 128:256])], axis=1),
                       preferred_element_type=F32) for p in prs]
        rhp = [rh[p] - gku[p][:, 0:128] for p in prs]
        y0 = [agv[p][c_:2 * c_] - gku[p][:, 128:256] for p in prs]
        d = [_dot_tn(jnp.concatenate([vp[p], -ku[p][:, 128:256]], axis=0),
                     jnp.concatenate([kg[p], bg[p]], axis=0)) for p in prs]
        s = [s_ref[p] for p in prs]
        uy = [_dot_nt(jnp.concatenate([ku[p][:, 0:128], rhp[p]], axis=0), s[p]) for p in prs]
        ub = [_dot_tn(uy[p][0:c_], bg[p]) for p in prs]
        for p in prs:
            s_ref[p] = s[p] * jnp.exp(cl[p]) + jnp.where(bdmask, d[p] - ub[p], 0.0)
        y = [uy[p][c_:2 * c_] + y0[p] for p in prs]
        mean = seg_all(y)
        yc = [y[p] - mean[p] * (1.0 / RW_HEAD) for p in prs]
        var = seg_all([x * x for x in yc])
        bon = seg_all([rp[p] * kp[p] * rk_ref[:, lns[p]] for p in prs])
        for p in prs:
            yn = yc[p] * lax.rsqrt(var[p] * (1.0 / RW_HEAD) + EPS) * ng_ref[:, lns[p]]
            o_ref[rows, lns[p]] = ((yn + bon[p] * vp[p]) * g[:, lns[p]]).astype(BF16)
        return carry

    lax.fori_loop(0, n_chunks, chunk, 0)

    @pl.when(t == pl.num_programs(1) - 1)
    def _():
        for p in range(RW_PAIRS):
            s = s_ref[p]
            sout_ref[0, 2 * p] = s[0:RW_HEAD, 0:RW_HEAD]
            sout_ref[0, 2 * p + 1] = s[RW_HEAD:, RW_HEAD:]


def _rw_prompt(z, mus, params, b_, t_, tt):
    nt = t_ // tt
    row = lambda blk: (lambda b, t: (b * nt + t, blk))
    const = lambda b, t: (0, 0)
    w0, wup, a0, aup, gup, kkw, kaw, rk, ng = params
    full = lambda a: pl.BlockSpec(a.shape, const)
    rw_blk = OFF_RW // RW_W
    return pl.pallas_call(
        functools.partial(_rw_prompt_kernel, n_chunks=tt // CHUNK, tt=tt),
        grid=(b_, nt),
        in_specs=[pl.BlockSpec((tt, RW_W), row(rw_blk)),
                  pl.BlockSpec((tt, RW_W), row(rw_blk + 1)),
                  pl.BlockSpec((tt, RW_W), row(rw_blk + 2)),
                  pl.BlockSpec((tt, 128), row((OFF_RW + 3 * RW_W) // 128)),
                  pl.BlockSpec((tt, 128), row((OFF_RW + 3 * RW_W) // 128 + 1))]
                 + [full(a) for a in mus] + [full(a) for a in params],
        out_specs=[pl.BlockSpec((tt, RW_W), lambda b, t: (b * nt + t, 0)),
                   pl.BlockSpec((1, RW_HEADS, RW_HEAD, RW_HEAD), lambda b, t: (b, 0, 0, 0))],
        out_shape=[jax.ShapeDtypeStruct((b_ * t_, RW_W), BF16),
                   jax.ShapeDtypeStruct((b_, RW_HEADS, RW_HEAD, RW_HEAD), F32)],
        scratch_shapes=[pltpu.VMEM((RW_PAIRS, 128, 128), F32),
                        pltpu.VMEM((tt, RW_W), F32), pltpu.VMEM((tt, RW_W), F32),
                        pltpu.VMEM((tt, RW_W), F32), pltpu.VMEM((tt, 128), F32),
                        pltpu.VMEM((tt, 128), F32),
                        pltpu.VMEM((1, RW_W), F32), pltpu.VMEM((1, RW_W), F32),
                        pltpu.VMEM((1, RW_W), F32), pltpu.VMEM((1, 128), F32),
                        pltpu.VMEM((1, 128), F32)],
        compiler_params=pltpu.CompilerParams(
            dimension_semantics=("parallel", "arbitrary"), vmem_limit_bytes=VMEM_LIMIT),
        name="rwkv_prompt",
    )(z, z, z, z, z, *mus, *params)


def _sample_prep_kernel(sm_ref, aup_ref, ab_ref, mi_ref, mf_ref, m_ref, ib_ref, fb_ref,
                        r_ref, k_ref, v_ref, wa_ref, gd_ref, sh_r, sh_k, sh_v, sh_wa, sh_g,
                        mur_ref, muk_ref, muv_ref, muwa_ref, mug_ref,
                        w0_ref, wup_ref, a0_ref, aup2_ref, gup_ref, kkw_ref, kaw_ref,
                        ela_ref, mnew_ref, wend_ref, ws_ref,
                        rr_ref, rk_ref, rv_ref, rkk_ref, rb_ref, rw_ref, rg_ref):
    la = _logsig(_dot(sm_ref[...], aup_ref[...]) + ab_ref[...]) * (1.0 / GLA_TAU)
    ela_ref[...] = jnp.exp(la)
    i_pre = mi_ref[...] + ib_ref[...]
    logf = _logsig(mf_ref[...] + fb_ref[...])
    m = m_ref[...]
    m_new = jnp.maximum(logf + m, i_pre)
    mnew_ref[...] = m_new
    wend_ref[...] = jnp.exp(logf + m - m_new)
    ws_ref[...] = jnp.exp(i_pre - m_new)
    mixf = lambda x, s, mu: x[...] + (s[...] - x[...]) * mu[...]
    zr = mixf(r_ref, sh_r, mur_ref)
    zk = mixf(k_ref, sh_k, muk_ref)
    zv = mixf(v_ref, sh_v, muv_ref)
    zwa = mixf(wa_ref, sh_wa, muwa_ref)
    zg = mixf(gd_ref, sh_g, mug_ref)
    lw, a, g, kk0, kmod = _rw_tokens(zr, zk, zv, zwa, zg, w0_ref[...], wup_ref[...], a0_ref[...],
                                     aup2_ref[...], gup_ref[...], kkw_ref[...], kaw_ref[...])
    ones_bd = _ones_bd()
    for p in range(RW_PAIRS):
        ln = slice(p * 128, (p + 1) * 128)
        kkp = kk0[:, ln]
        kkn = kkp * lax.rsqrt(_seg(kkp * kkp, ones_bd) + 1e-12)
        rkk_ref[:, ln] = kkn
        rb_ref[:, ln] = kkn * a[:, ln]
    rr_ref[...] = zr
    rk_ref[...] = kmod
    rv_ref[...] = zv
    rw_ref[...] = jnp.exp(lw)
    rg_ref[...] = g


def _gla_step_kernel(s_ref, q_ref, k_ref, e_ref, v_ref, sn_ref, o_ref):
    sn = e_ref[...] * s_ref[...] + k_ref[...] * v_ref[...]
    sn_ref[...] = sn
    o_ref[...] = jnp.sum((q_ref[...] * (GLA_DK ** -0.5)) * sn, axis=2, keepdims=True)


def _mlstm_step_kernel(c_ref, n_ref, qc_ref, kc_ref, qr_ref, kr_ref, v_ref, wend_ref, ws_ref, mnew_ref,
                       cn_ref, nn_ref, h_ref):
    wend = wend_ref[...]
    ws = ws_ref[...]
    cn = wend * c_ref[...] + (kc_ref[...] * ws) * v_ref[...]
    nn = wend * n_ref[...] + ws * kr_ref[...]
    cn_ref[...] = cn
    nn_ref[...] = nn
    scale = ML_DH ** -0.5
    num = jnp.sum((qc_ref[...] * scale) * cn, axis=2, keepdims=True)
    den = jnp.sum((qr_ref[...] * scale) * nn, axis=3, keepdims=True)
    h_ref[...] = num / jnp.maximum(jnp.abs(den), jnp.exp(-mnew_ref[...]))


def _rw_step_kernel(s_ref, kk_ref, b_ref, k_ref, r_ref, w_ref, v_ref, sn_ref, y_ref):
    s = s_ref[...]
    sa = jnp.sum(s * (-kk_ref[...]), axis=3, keepdims=True)
    sn = s * w_ref[...] + sa * b_ref[...] + v_ref[...] * k_ref[...]
    sn_ref[...] = sn
    y_ref[...] = jnp.sum(sn * r_ref[...], axis=3, keepdims=True)


def _sample_post_kernel(og_ref, gg_ref, gn_ref, hm_ref, mo_ref, mn_ref,
                        y_ref, r_ref, k_ref, v_ref, g_ref, rk_ref, ng_ref,
                        ogla_ref, oml_ref, orw_ref):
    for h in range(GLA_HEADS):
        vl = slice(h * GLA_DV, (h + 1) * GLA_DV)
        o = og_ref[:, vl]
        on = o * lax.rsqrt(jnp.mean(o * o, axis=-1, keepdims=True) + EPS) * gn_ref[...]
        gg = gg_ref[:, vl]
        ogla_ref[:, vl] = (on * (gg * _sigmoid(gg))).astype(BF16)
    for h in range(ML_HEADS):
        hl = slice(h * ML_DH, (h + 1) * ML_DH)
        hm = hm_ref[:, hl]
        y = hm - jnp.mean(hm, axis=-1, keepdims=True)
        y = y * lax.rsqrt(jnp.mean(y * y, axis=-1, keepdims=True) + EPS) * mn_ref[...]
        oml_ref[:, hl] = (y * _sigmoid(mo_ref[:, hl])).astype(BF16)
    ones_bd = _ones_bd()
    for p in range(RW_PAIRS):
        ln = slice(p * 128, (p + 1) * 128)
        orw_ref[:, ln] = _rw_epilogue(y_ref[:, ln], r_ref[:, ln], k_ref[:, ln], v_ref[:, ln],
                                      g_ref[:, ln], rk_ref[:, ln], ng_ref[:, ln], ones_bd)


def _full_call(kernel_fn, args, out_shapes, name):
    nd = lambda a: (lambda: (0,) * len(a.shape))
    return pl.pallas_call(
        kernel_fn,
        in_specs=[pl.BlockSpec(a.shape, nd(a)) for a in args],
        out_specs=[pl.BlockSpec(s.shape, nd(s)) for s in out_shapes],
        out_shape=out_shapes,
        compiler_params=pltpu.CompilerParams(vmem_limit_bytes=VMEM_LIMIT),
        name=name,
    )(*args)


def _batched_call(kernel_fn, args, out_shapes, bb, name):
    nb = args[0].shape[0]
    spec = lambda s: pl.BlockSpec((bb,) + tuple(s.shape[1:]),
                                  lambda i, n=len(s.shape): (i,) + (0,) * (n - 1))
    return pl.pallas_call(
        kernel_fn,
        grid=(nb // bb,),
        in_specs=[spec(a) for a in args],
        out_specs=[spec(s) for s in out_shapes],
        out_shape=out_shapes,
        compiler_params=pltpu.CompilerParams(
            dimension_semantics=("parallel",), vmem_limit_bytes=VMEM_LIMIT),
        name=name,
    )(*args)


def _prep_layer_params(l, p):
    w = p["w_in"][l]
    w_in_p = jnp.concatenate(
        [w[:, 0:2048], w[:, 2064:3088], w[:, 3088:6160], w[:, 6168:7192], w[:, 7192:13592],
         w[:, 2048:2064], w[:, 6160:6168], jnp.zeros((D_MODEL, NP - 13592), F32)], axis=1).astype(BF16)
    aup_pad = jnp.zeros((128, GLA_QK), F32).at[0:GLA_RANK].set(p["gla_a_up"][l])
    brow = (jnp.zeros((1, 128), F32).at[0, SM_I:SM_I + 4].set(p["ml_i_bias"][l])
            .at[0, SM_F:SM_F + 4].set(p["ml_f_bias"][l]))
    bcol = jnp.broadcast_to(jnp.concatenate([p["ml_i_bias"][l], p["ml_f_bias"][l]])[:, None], (8, CHUNK))
    mu = p["rw_mu"][l][None, :]
    mus = (mu[:, 0:1024], mu[:, 1024:2048], mu[:, 2048:3072], mu[:, 3072:3200], mu[:, 3200:3328])
    wup_pad = jnp.zeros((128, RW_W), F32).at[0:64].set(p["rw_w_up"][l])
    aup2_pad = jnp.zeros((128, RW_W), F32).at[64:128].set(p["rw_a_up"][l])
    row = lambda a: a.reshape(1, -1)
    rw_params = (row(p["rw_w0"][l]), wup_pad, row(p["rw_a0"][l]), aup2_pad, p["rw_g_up"][l],
                 row(p["rw_k_k"][l]), row(p["rw_k_a"][l]), row(p["rw_r_k"][l]), row(p["rw_norm_g"][l]))
    return dict(
        norm1_g=row(p["norm1_g"][l]), w_in=w_in_p, aup_pad=aup_pad, ab=row(p["gla_a_bias"][l]),
        gla_ng=row(p["gla_norm_g"][l]), brow=brow, bcol=bcol, ib=row(p["ml_i_bias"][l]),
        fb=row(p["ml_f_bias"][l]), ml_ng=row(p["ml_norm_g"][l]), mus=mus, rw=rw_params,
        wb=p["w_branch"][l].astype(BF16), wo=p["w_out"][l].astype(BF16), norm2_g=row(p["norm2_g"][l]),
        w1=p["w_ff1"][l].astype(BF16), w2=p["w_ff2"][l].astype(BF16))


def _prompt_layer(x, lp, fg, final, b_, t_):
    rows = x.shape[0]
    tm = min(512, rows)
    tm_big = min(1024, rows)
    z = _proj_in(x, lp["norm1_g"], lp["w_in"], tm_big)
    tt = min(256, t_)
    o_gla, gla_new = _gla_prompt(z, lp["aup_pad"], lp["ab"], lp["gla_ng"], b_, t_, tt)
    sm = z[:, OFF_SM + SM_I:OFF_SM + SM_I + 8]
    smt = sm.reshape(b_, t_ // CHUNK, CHUNK, 8).transpose(0, 1, 3, 2)
    o_ml, c_new, n_new, m_new = _mlstm_prompt(z, smt, lp["brow"], lp["bcol"], lp["ml_ng"], b_, t_, tt)
    o_rw, rw_new = _rw_prompt(z, lp["mus"], lp["rw"], b_, t_, tt)
    x = _merge(o_gla, o_ml, o_rw, z, x, lp["wb"], lp["wo"], tm)
    x = _ffn(x, lp["norm2_g"], lp["w1"], lp["w2"], fg, final, tm_big)
    shift = z.reshape(b_, t_, NP)[:, t_ - 1, OFF_RW:OFF_RW + RW_COLS]
    return x, (gla_new, c_new, n_new, m_new[:, :, 0], rw_new, shift)


def _sample_layer(x, st, lp, fg, final):
    gla_s, ml_c, ml_n, ml_m, rw_s, shift = st
    nb = x.shape[0]
    z = _proj_in(x, lp["norm1_g"], lp["w_in"], nb)
    zc = lambda off, w: z[:, off:off + w]
    sh = (shift[:, 0:1024], shift[:, 1024:2048], shift[:, 2048:3072], shift[:, 3072:3200], shift[:, 3200:3328])
    w0, wup, a0, aup2, gup, kkw, kaw, rk, ng = lp["rw"]
    sds = lambda *s: jax.ShapeDtypeStruct(s, F32)
    prep_args = [zc(OFF_SM, 128), lp["aup_pad"], lp["ab"],
                 zc(OFF_SM + SM_I, 4), zc(OFF_SM + SM_F, 4), ml_m, lp["ib"], lp["fb"],
                 zc(OFF_RW, 1024), zc(OFF_RW + 1024, 1024), zc(OFF_RW + 2048, 1024),
                 zc(OFF_RW + 3072, 128), zc(OFF_RW + 3200, 128), *sh, *lp["mus"],
                 w0, wup, a0, aup2, gup, kkw, kaw]
    (ela, m_new, wend, ws, rr, rkm, rv, rkk, rb, rw, rg) = _full_call(
        _sample_prep_kernel, prep_args,
        [sds(nb, GLA_QK), sds(nb, 4), sds(nb, 4), sds(nb, 4)] + [sds(nb, RW_W)] * 7, "sample_prep")

    col = lambda a, h, d: a.reshape(nb, h, d, 1)
    rowv = lambda a, h, d: a.reshape(nb, h, 1, d)
    q = zc(OFF_GQ, GLA_QK)
    gla_new, o_g = _batched_call(
        _gla_step_kernel,
        [gla_s, col(q, 4, GLA_DK), col(zc(OFF_GK, GLA_QK), 4, GLA_DK), col(ela, 4, GLA_DK),
         rowv(zc(OFF_GV, GLA_V), 4, GLA_DV)],
        [sds(*gla_s.shape), sds(nb, 4, 1, GLA_DV)], 8, "gla_step")

    mq = zc(OFF_MQ, ML_W)
    mk = zc(OFF_MK, ML_W)
    sc = lambda a: a.reshape(nb, 4, 1, 1)
    c_new, n_new, hm = _batched_call(
        _mlstm_step_kernel,
        [ml_c, rowv(ml_n, 4, ML_DH), col(mq, 4, ML_DH), col(mk, 4, ML_DH), rowv(mq, 4, ML_DH),
         rowv(mk, 4, ML_DH), rowv(zc(OFF_MV, ML_W), 4, ML_DH), sc(wend), sc(ws), sc(m_new)],
        [sds(*ml_c.shape), sds(nb, 4, 1, ML_DH), sds(nb, 4, 1, ML_DH)], 4, "mlstm_step")

    hrow = lambda a: a.reshape(nb, RW_HEADS, 1, RW_HEAD)
    rw_new, y = _batched_call(
        _rw_step_kernel,
        [rw_s, hrow(rkk), hrow(rb), hrow(rkm), hrow(rr), hrow(rw), rv.reshape(nb, RW_HEADS, RW_HEAD, 1)],
        [sds(*rw_s.shape), sds(nb, RW_HEADS, RW_HEAD, 1)], 8, "rwkv_step")

    bf = lambda: jax.ShapeDtypeStruct((nb, D_MODEL), BF16)
    o_gla, o_ml, o_rw = _full_call(
        _sample_post_kernel,
        [o_g.reshape(nb, GLA_V), zc(OFF_GG, GLA_V), lp["gla_ng"],
         hm.reshape(nb, ML_W), zc(OFF_MO, ML_W), lp["ml_ng"],
         y.reshape(nb, RW_W), rr, rkm, rv, rg, rk, ng],
        [bf(), bf(), bf()], "sample_post")
    x = _merge(o_gla, o_ml, o_rw, z, x, lp["wb"], lp["wo"], nb)
    x = _ffn(x, lp["norm2_g"], lp["w1"], lp["w2"], fg, final, nb)
    return x, (gla_new, c_new, n_new.reshape(nb, 4, ML_DH), m_new, rw_new, zc(OFF_RW, RW_COLS))


def kernel(x_prompt, x_sample, state_gla, state_mlstm_c, state_mlstm_n, state_mlstm_m, state_rwkv, state_shift, norm1_g, w_in, gla_a_up, gla_a_bias, gla_norm_g, ml_i_bias, ml_f_bias, ml_norm_g, rw_mu, rw_w0, rw_w_up, rw_a0, rw_a_up, rw_g_up, rw_k_k, rw_k_a, rw_r_k, rw_norm_g, w_branch, w_out, norm2_g, w_ff1, w_ff2, final_g):
    p = dict(norm1_g=norm1_g, w_in=w_in, gla_a_up=gla_a_up, gla_a_bias=gla_a_bias, gla_norm_g=gla_norm_g,
             ml_i_bias=ml_i_bias, ml_f_bias=ml_f_bias, ml_norm_g=ml_norm_g, rw_mu=rw_mu, rw_w0=rw_w0,
             rw_w_up=rw_w_up, rw_a0=rw_a0, rw_a_up=rw_a_up, rw_g_up=rw_g_up, rw_k_k=rw_k_k, rw_k_a=rw_k_a,
             rw_r_k=rw_r_k, rw_norm_g=rw_norm_g, w_branch=w_branch, w_out=w_out, norm2_g=norm2_g,
             w_ff1=w_ff1, w_ff2=w_ff2)
    depth = w_in.shape[0]
    b_, t_, _ = x_prompt.shape
    nb = x_sample.shape[0]
    fg = final_g.reshape(1, D_MODEL)
    xp = x_prompt.reshape(b_ * t_, D_MODEL)
    xs = x_sample.reshape(nb, D_MODEL)
    p_states, s_states = [], []
    for l in range(depth):
        lp = _prep_layer_params(l, p)
        final = l == depth - 1
        xp, stp = _prompt_layer(xp, lp, fg, final, b_, t_)
        st = (state_gla[l], state_mlstm_c[l], state_mlstm_n[l], state_mlstm_m[l], state_rwkv[l], state_shift[l])
        xs, sts = _sample_layer(xs, st, lp, fg, final)
        p_states.append(stp)
        s_states.append(sts)
    ps = [jnp.stack([s[i] for s in p_states]) for i in range(6)]
    ss = [jnp.stack([s[i] for s in s_states]) for i in range(6)]
    return (xp.reshape(b_, t_, D_MODEL), xs.reshape(nb, 1, D_MODEL),
            ps[0], ss[0], ps[1], ss[1], ps[2], ss[2], ps[3], ss[3], ps[4], ss[4], ps[5], ss[5])
```

```python
import functools

import jax
import jax.numpy as jnp
from jax import lax
from jax.experimental import pallas as pl
from jax.experimental.pallas import tpu as pltpu

F32 = jnp.float32
BF16 = jnp.bfloat16

D_MODEL = 1024
GLA_HEADS, GLA_DK, GLA_DV = 4, 128, 256
GLA_QK, GLA_V, GLA_RANK, GLA_TAU = 512, 1024, 16, 16.0
ML_HEADS, ML_DH, ML_W = 4, 256, 1024
RW_HEAD, RW_W, RW_HEADS = 64, 1024, 16
RW_PAIRS = RW_HEADS // 2
RW_COLS = 3 * RW_W + 64 + 64 + 128
D_FF = 4 * D_MODEL
EPS = 1e-6
CHUNK = 64

OFF_GQ, OFF_GK, OFF_GV, OFF_GG = 0, 512, 1024, 2048
OFF_MQ, OFF_MK, OFF_MV, OFF_MO = 3072, 4096, 5120, 6144
OFF_GATE = 7168
OFF_RW = 10240
OFF_SM = 13568
NP = 13824
SM_I, SM_F = 16, 20

VMEM_LIMIT = 48 * 1024 * 1024


def _dot(a, b):
    return jnp.dot(a.astype(BF16), b.astype(BF16), preferred_element_type=F32)


def _dot_nt(a, b):
    return lax.dot_general(a.astype(BF16), b.astype(BF16), (((1,), (1,)), ((), ())),
                           preferred_element_type=F32)


def _dot_tn(a, b):
    return lax.dot_general(a.astype(BF16), b.astype(BF16), (((0,), (0,)), ((), ())),
                           preferred_element_type=F32)


def _split(x):
    hi = x.astype(BF16)
    lo = (x - hi.astype(F32)).astype(BF16)
    return hi, lo


def _dot2(x, w_exact):
    hi, lo = _split(x)
    return (jnp.dot(hi, w_exact, preferred_element_type=F32)
            + jnp.dot(lo, w_exact, preferred_element_type=F32))


def _dot2_tn(x, w_exact):
    hi, lo = _split(x)
    dn = (((0,), (0,)), ((), ()))
    return (lax.dot_general(hi, w_exact, dn, preferred_element_type=F32)
            + lax.dot_general(lo, w_exact, dn, preferred_element_type=F32))


def _dot2_left(w_exact, x):
    hi, lo = _split(x)
    return (jnp.dot(w_exact, hi, preferred_element_type=F32)
            + jnp.dot(w_exact, lo, preferred_element_type=F32))


_NN = (((1,), (0,)), ((), ()))
_TN = (((0,), (0,)), ((), ()))
_NT = (((1,), (1,)), ((), ()))


def _dot3(a, b, dn=_NN):
    ah, al = _split(a)
    bh, bl = _split(b)
    f = lambda x, y: lax.dot_general(x, y, dn, preferred_element_type=F32)
    return f(ah, bh) + f(al, bh) + f(ah, bl)


def _logsig(x):
    return jnp.minimum(x, 0.0) - jnp.log1p(jnp.exp(-jnp.abs(x)))


def _sigmoid(x):
    return jax.nn.sigmoid(x)


def _rms(x, g):
    return x * lax.rsqrt(jnp.mean(x * x, axis=-1, keepdims=True) + EPS) * g


def _tri(n, m=None):
    m = n if m is None else m
    r = lax.broadcasted_iota(jnp.int32, (n, m), 0)
    c = lax.broadcasted_iota(jnp.int32, (n, m), 1)
    return r, c


def _proj_in_kernel(x_ref, g_ref, w_ref, o_ref, h_ref):
    @pl.when(pl.program_id(1) == 0)
    def _():
        h_ref[...] = _rms(x_ref[...], g_ref[...]).astype(BF16)

    o_ref[...] = jnp.dot(h_ref[...], w_ref[...], preferred_element_type=F32)


def _proj_in(x, g, w, tm, tn=1536):
    m = x.shape[0]
    return pl.pallas_call(
        _proj_in_kernel,
        grid=(m // tm, NP // tn),
        in_specs=[pl.BlockSpec((tm, D_MODEL), lambda i, j: (i, 0)),
                  pl.BlockSpec((1, D_MODEL), lambda i, j: (0, 0)),
                  pl.BlockSpec((D_MODEL, tn), lambda i, j: (0, j))],
        out_specs=pl.BlockSpec((tm, tn), lambda i, j: (i, j)),
        out_shape=jax.ShapeDtypeStruct((m, NP), F32),
        scratch_shapes=[pltpu.VMEM((tm, D_MODEL), BF16)],
        compiler_params=pltpu.CompilerParams(
            dimension_semantics=("parallel", "arbitrary"), vmem_limit_bytes=VMEM_LIMIT),
        name="proj_in",
    )(x, g, w)


def _merge_kernel(og_ref, om_ref, or_ref, g0_ref, g1_ref, g2_ref, x_ref, wb_ref, wo_ref, o_ref):
    acc = jnp.dot(og_ref[...], wb_ref[0], preferred_element_type=F32) * _sigmoid(g0_ref[...])
    acc += jnp.dot(om_ref[...], wb_ref[1], preferred_element_type=F32) * _sigmoid(g1_ref[...])
    acc += jnp.dot(or_ref[...], wb_ref[2], preferred_element_type=F32) * _sigmoid(g2_ref[...])
    o_ref[...] = x_ref[...] + jnp.dot(acc.astype(BF16), wo_ref[...], preferred_element_type=F32)


def _merge(o_gla, o_ml, o_rw, z, x, wb, wo, tm):
    m = x.shape[0]
    row = lambda i: (i, 0)
    gate_blk = OFF_GATE // D_MODEL
    return pl.pallas_call(
        _merge_kernel,
        grid=(m // tm,),
        in_specs=[pl.BlockSpec((tm, D_MODEL), row),
                  pl.BlockSpec((tm, D_MODEL), row),
                  pl.BlockSpec((tm, D_MODEL), row),
                  pl.BlockSpec((tm, D_MODEL), lambda i: (i, gate_blk)),
                  pl.BlockSpec((tm, D_MODEL), lambda i: (i, gate_blk + 1)),
                  pl.BlockSpec((tm, D_MODEL), lambda i: (i, gate_blk + 2)),
                  pl.BlockSpec((tm, D_MODEL), row),
                  pl.BlockSpec((3, D_MODEL, D_MODEL), lambda i: (0, 0, 0)),
                  pl.BlockSpec((D_MODEL, D_MODEL), lambda i: (0, 0))],
        out_specs=pl.BlockSpec((tm, D_MODEL), row),
        out_shape=jax.ShapeDtypeStruct((m, D_MODEL), F32),
        compiler_params=pltpu.CompilerParams(
            dimension_semantics=("parallel",), vmem_limit_bytes=VMEM_LIMIT),
        name="merge",
    )(o_gla, o_ml, o_rw, z, z, z, x, wb, wo)


def _ffn_kernel(x_ref, g_ref, w1_ref, w2_ref, fg_ref, o_ref, h_ref, acc_ref, *, final):
    f = pl.program_id(1)

    @pl.when(f == 0)
    def _():
        h_ref[...] = _rms(x_ref[...], g_ref[...]).astype(BF16)
        acc_ref[...] = jnp.zeros_like(acc_ref)

    a = jnp.dot(h_ref[...], w1_ref[...], preferred_element_type=F32)
    a = jnp.square(jnp.maximum(a, 0.0))
    acc_ref[...] += jnp.dot(a.astype(BF16), w2_ref[...], preferred_element_type=F32)

    @pl.when(f == pl.num_programs(1) - 1)
    def _():
        y = x_ref[...] + acc_ref[...]
        if final:
            y = _rms(y, fg_ref[...])
        o_ref[...] = y


def _ffn(x, g, w1, w2, fg, final, tm, tf=512):
    m = x.shape[0]
    return pl.pallas_call(
        functools.partial(_ffn_kernel, final=final),
        grid=(m // tm, D_FF // tf),
        in_specs=[pl.BlockSpec((tm, D_MODEL), lambda i, f: (i, 0)),
                  pl.BlockSpec((1, D_MODEL), lambda i, f: (0, 0)),
                  pl.BlockSpec((D_MODEL, tf), lambda i, f: (0, f)),
                  pl.BlockSpec((tf, D_MODEL), lambda i, f: (f, 0)),
                  pl.BlockSpec((1, D_MODEL), lambda i, f: (0, 0))],
        out_specs=pl.BlockSpec((tm, D_MODEL), lambda i, f: (i, 0)),
        out_shape=jax.ShapeDtypeStruct((m, D_MODEL), F32),
        scratch_shapes=[pltpu.VMEM((tm, D_MODEL), BF16), pltpu.VMEM((tm, D_MODEL), F32)],
        compiler_params=pltpu.CompilerParams(
            dimension_semantics=("parallel", "arbitrary"), vmem_limit_bytes=VMEM_LIMIT),
        name="ffn",
    )(x, g, w1, w2, fg)


def _gla_prompt_kernel(q_ref, k_ref, v_ref, gg_ref, sm_ref, aup_ref, ab_ref, gn_ref,
                       o_ref, sout_ref, s_ref, *, n_chunks, ns):
    t = pl.program_id(1)

    @pl.when(t == 0)
    def _():
        s_ref[...] = jnp.zeros_like(s_ref)

    c_ = CHUNK
    r, c = _tri(c_)
    incl = r >= c
    tril = jnp.where(incl, 1.0, 0.0).astype(BF16)
    ones_c = jnp.ones((c_, GLA_DK), BF16)

    def chunk(ci, carry):
        rows = pl.ds(pl.multiple_of(ci * c_, c_), c_)
        tokw = []
        for j in range(ns):
            x = _dot(sm_ref[j, rows, :], aup_ref[...]) + ab_ref[...]
            la = _logsig(x) * (1.0 / GLA_TAU)
            b = _dot2_left(tril, la)
            dec = jnp.exp(_dot2_tn(la, ones_c))
            q = q_ref[j, rows, :] * (GLA_DK ** -0.5)
            k = k_ref[j, rows, :]
            tokw.append((q * jnp.exp(b), k * jnp.exp(-b), k * jnp.exp(b[c_ - 1:c_, :] - b), dec))
        its = [(j, h) for j in range(ns) for h in range(GLA_HEADS)]
        hs = range(len(its))
        kls = [slice(h * GLA_DK, (h + 1) * GLA_DK) for _, h in its]
        vls = [slice(h * GLA_DV, (h + 1) * GLA_DV) for _, h in its]
        qe = [tokw[j][0][:, kls[i]] for i, (j, _) in enumerate(its)]
        ke = [tokw[j][1][:, kls[i]] for i, (j, _) in enumerate(its)]
        kd = [tokw[j][2][:, kls[i]] for i, (j, _) in enumerate(its)]
        v = [v_ref[j, rows, vls[i]] for i, (j, _) in enumerate(its)]
        att = [jnp.where(incl, _dot_nt(qe[i], ke[i]), 0.0) for i in hs]
        s = [s_ref[j, h] for j, h in its]
        oi = [_dot(qe[i], s[i]) for i in hs]
        kv = [_dot_tn(kd[i], v[i]) for i in hs]
        o = [oi[i] + _dot(att[i], v[i]) for i in hs]
        for i, (j, h) in enumerate(its):
            dh = tokw[j][3][kls[i], :]
            s_ref[j, h] = s[i] * jnp.concatenate([dh, dh], axis=1) + kv[i]
        msq = [jnp.mean(o[i] * o[i], axis=-1, keepdims=True) for i in hs]
        for i, (j, _) in enumerate(its):
            on = o[i] * lax.rsqrt(msq[i] + EPS) * gn_ref[...]
            gg = gg_ref[j, rows, vls[i]]
            o_ref[j, rows, vls[i]] = (on * (gg * _sigmoid(gg))).astype(BF16)
        return carry

    lax.fori_loop(0, n_chunks, chunk, 0)

    @pl.when(t == pl.num_programs(1) - 1)
    def _():
        sout_ref[...] = s_ref[...]


def _seq_per_step(b_):
    return 2 if b_ % 2 == 0 else 1


def _gla_prompt(z, aup_pad, ab, gn, b_, t_, tt):
    ns = _seq_per_step(b_)
    z3 = z.reshape(b_, t_, NP)
    row = lambda blk: (lambda b, t: (b, t, blk))
    const = lambda b, t: (0, 0)
    o, s_new = pl.pallas_call(
        functools.partial(_gla_prompt_kernel, n_chunks=tt // CHUNK, ns=ns),
        grid=(b_ // ns, t_ // tt),
        in_specs=[pl.BlockSpec((ns, tt, GLA_QK), row(OFF_GQ // GLA_QK)),
                  pl.BlockSpec((ns, tt, GLA_QK), row(OFF_GK // GLA_QK)),
                  pl.BlockSpec((ns, tt, GLA_V), row(OFF_GV // GLA_V)),
                  pl.BlockSpec((ns, tt, GLA_V), row(OFF_GG // GLA_V)),
                  pl.BlockSpec((ns, tt, 128), row(OFF_SM // 128)),
                  pl.BlockSpec((128, GLA_QK), const),
                  pl.BlockSpec((1, GLA_QK), const),
                  pl.BlockSpec((1, GLA_DV), const)],
        out_specs=[pl.BlockSpec((ns, tt, GLA_V), lambda b, t: (b, t, 0)),
                   pl.BlockSpec((ns, GLA_HEADS, GLA_DK, GLA_DV), lambda b, t: (b, 0, 0, 0))],
        out_shape=[jax.ShapeDtypeStruct((b_, t_, GLA_V), BF16),
                   jax.ShapeDtypeStruct((b_, GLA_HEADS, GLA_DK, GLA_DV), F32)],
        scratch_shapes=[pltpu.VMEM((ns, GLA_HEADS, GLA_DK, GLA_DV), F32)],
        compiler_params=pltpu.CompilerParams(
            dimension_semantics=("parallel", "arbitrary"), vmem_limit_bytes=VMEM_LIMIT),
        name="gla_prompt",
    )(z3, z3, z3, z3, z3, aup_pad, ab, gn)
    return o.reshape(b_ * t_, GLA_V), s_new


def _mlstm_prompt_kernel(q_ref, k_ref, v_ref, mo_ref, sm_ref, smt_ref, brow_ref, bcol_ref, ng_ref,
                         o_ref, cout_ref, nout_ref, mout_ref, c_ref, n_ref, m_ref, *, n_chunks, ns):
    t = pl.program_id(1)

    @pl.when(t == 0)
    def _():
        c_ref[...] = jnp.zeros_like(c_ref)
        n_ref[...] = jnp.zeros_like(n_ref)
        m_ref[...] = jnp.zeros_like(m_ref)

    c_ = CHUNK
    r, c = _tri(c_)
    incl = r >= c
    tril = jnp.where(incl, 1.0, 0.0).astype(BF16)
    triu = jnp.where(r <= c, 1.0, 0.0).astype(BF16)
    sub8 = lax.broadcasted_iota(jnp.int32, (8, c_), 0)

    def chunk(ci, carry):
        rows = pl.ds(pl.multiple_of(ci * c_, c_), c_)
        sm, fcol, smt, frow = [], [], [], []
        for j in range(ns):
            sm.append(sm_ref[j, rows, :] + brow_ref[...])
            fcol.append(_dot2_left(tril, _logsig(sm[j])))
            smt.append(smt_ref[j, ci] + bcol_ref[...])
            frow.append(_dot2(jnp.where(sub8 >= 4, _logsig(smt[j]), 0.0), triu))
        its = [(j, h) for j in range(ns) for h in range(ML_HEADS)]
        hs = range(len(its))
        hls = [slice(h * ML_DH, (h + 1) * ML_DH) for _, h in its]
        f_t = [fcol[j][:, SM_F + h:SM_F + h + 1] for j, h in its]
        i_t = [sm[j][:, SM_I + h:SM_I + h + 1] for j, h in its]
        f_s = [frow[j][4 + h:5 + h, :] for j, h in its]
        i_s = [smt[j][h:h + 1, :] for j, h in its]
        m = [m_ref[j, h][0:1, 0:1] for j, h in its]
        q = [q_ref[j, rows, hls[i]] * (ML_DH ** -0.5) for i, (j, _) in enumerate(its)]
        k = [k_ref[j, rows, hls[i]] for i, (j, _) in enumerate(its)]
        v = [v_ref[j, rows, hls[i]] for i, (j, _) in enumerate(its)]
        cm = [c_ref[j, h] for j, h in its]
        n8 = [n_ref[j, h] for j, h in its]
        qk0 = [_dot_nt(q[h], k[h]) for h in hs]
        qc = [_dot(q[h], cm[h]) for h in hs]
        qn = [_dot_nt(q[h], n8[h])[:, 0:1] for h in hs]
        gmax = [jnp.max(jnp.where(incl, i_s[h] - f_s[h], -jnp.inf), axis=1, keepdims=True) for h in hs]
        mt = [jnp.maximum(f_t[h] + m[h], f_t[h] + gmax[h]) for h in hs]
        w_inter = [jnp.exp(f_t[h] + m[h] - mt[h]) for h in hs]
        d = [jnp.exp(jnp.where(incl, (f_t[h] - mt[h]) + (i_s[h] - f_s[h]), -jnp.inf)) for h in hs]
        qk = [qk0[h] * d[h] for h in hs]
        qkv = [_dot(qk[h], v[h]) for h in hs]
        m_new = [mt[h][c_ - 1:c_, :] for h in hs]
        f_last = [f_t[h][c_ - 1:c_, :] for h in hs]
        w_end = [jnp.exp(f_last[h] + m[h] - m_new[h]) for h in hs]
        kws = [k[h] * jnp.exp(f_last[h] - f_t[h] + i_t[h] - m_new[h]) for h in hs]
        kwv = [_dot_tn(kws[h], v[h]) for h in hs]
        for i, (j, h) in enumerate(its):
            c_ref[j, h] = w_end[i] * cm[i] + kwv[i]
            n_ref[j, h] = w_end[i] * n8[i] + jnp.broadcast_to(jnp.sum(kws[i], axis=0, keepdims=True),
                                                                (8, ML_DH))
            m_ref[j, h] = jnp.broadcast_to(m_new[i], (8, 128))
        den = [w_inter[i] * qn[i] + jnp.sum(qk[i], axis=1, keepdims=True) for i in hs]
        hm = [(w_inter[i] * qc[i] + qkv[i]) / jnp.maximum(jnp.abs(den[i]), jnp.exp(-mt[i])) for i in hs]
        mu = [jnp.mean(hm[i], axis=-1, keepdims=True) for i in hs]
        yc = [hm[i] - mu[i] for i in hs]
        var = [jnp.mean(yc[i] * yc[i], axis=-1, keepdims=True) for i in hs]
        for i, (j, _) in enumerate(its):
            y = yc[i] * lax.rsqrt(var[i] + EPS) * ng_ref[...]
            o_ref[j, rows, hls[i]] = (y * _sigmoid(mo_ref[j, rows, hls[i]])).astype(BF16)
        return carry

    lax.fori_loop(0, n_chunks, chunk, 0)

    @pl.when(t == pl.num_programs(1) - 1)
    def _():
        cout_ref[...] = c_ref[...]
        for j in range(ns):
            for h in range(ML_HEADS):
                nout_ref[j, h:h + 1, :] = n_ref[j, h][0:1, :]
                mout_ref[j, h:h + 1, :] = m_ref[j, h][0:1, :]


def _mlstm_prompt(z, smt, brow, bcol, ng, b_, t_, tt):
    ns = _seq_per_step(b_)
    nc = tt // CHUNK
    z3 = z.reshape(b_, t_, NP)
    row = lambda blk: (lambda b, t: (b, t, blk))
    const = lambda b, t: (0, 0)
    o, c_new, n_new, m_new = pl.pallas_call(
        functools.partial(_mlstm_prompt_kernel, n_chunks=nc, ns=ns),
        grid=(b_ // ns, t_ // tt),
        in_specs=[pl.BlockSpec((ns, tt, ML_W), row(OFF_MQ // ML_W)),
                  pl.BlockSpec((ns, tt, ML_W), row(OFF_MK // ML_W)),
                  pl.BlockSpec((ns, tt, ML_W), row(OFF_MV // ML_W)),
                  pl.BlockSpec((ns, tt, ML_W), row(OFF_MO // ML_W)),
                  pl.BlockSpec((ns, tt, 128), row(OFF_SM // 128)),
                  pl.BlockSpec((ns, nc, 8, CHUNK), lambda b, t: (b, t, 0, 0)),
                  pl.BlockSpec((1, 128), const),
                  pl.BlockSpec((8, CHUNK), const),
                  pl.BlockSpec((1, ML_DH), const)],
        out_specs=[pl.BlockSpec((ns, tt, ML_W), lambda b, t: (b, t, 0)),
                   pl.BlockSpec((ns, ML_HEADS, ML_DH, ML_DH), lambda b, t: (b, 0, 0, 0)),
                   pl.BlockSpec((ns, ML_HEADS, ML_DH), lambda b, t: (b, 0, 0)),
                   pl.BlockSpec((ns, ML_HEADS, 128), lambda b, t: (b, 0, 0))],
        out_shape=[jax.ShapeDtypeStruct((b_, t_, ML_W), BF16),
                   jax.ShapeDtypeStruct((b_, ML_HEADS, ML_DH, ML_DH), F32),
                   jax.ShapeDtypeStruct((b_, ML_HEADS, ML_DH), F32),
                   jax.ShapeDtypeStruct((b_, ML_HEADS, 128), F32)],
        scratch_shapes=[pltpu.VMEM((ns, ML_HEADS, ML_DH, ML_DH), F32),
                        pltpu.VMEM((ns, ML_HEADS, 8, ML_DH), F32),
                        pltpu.VMEM((ns, ML_HEADS, 8, 128), F32)],
        compiler_params=pltpu.CompilerParams(
            dimension_semantics=("parallel", "arbitrary"), vmem_limit_bytes=VMEM_LIMIT),
        name="mlstm_prompt",
    )(z3, z3, z3, z3, z3, smt, brow, bcol, ng)
    return o.reshape(b_ * t_, ML_W), c_new, n_new, m_new


def _rw_tokens(zr, zk, zv, zwa, zg, w0, wup, a0, aup, gup, kkw, kaw):
    u = w0 + _dot(jnp.tanh(zwa), wup)
    lw = -jnp.exp(_logsig(u) - 0.5)
    a = _sigmoid(a0 + _dot(zwa, aup))
    g = _dot(_sigmoid(zg), gup)
    kk0 = zk * kkw
    kmod = zk * (1.0 + (a - 1.0) * kaw)
    return lw, a, g, kk0, kmod


def _ones_bd():
    r, c = _tri(128)
    return jnp.where((r // RW_HEAD) == (c // RW_HEAD), 1.0, 0.0).astype(BF16)


def _seg(x, ones_bd):
    return _dot2(x, ones_bd)


def _rw_epilogue(y, r, kmod, v, g, rk, ng, ones_bd):
    mean = _seg(y, ones_bd) * (1.0 / RW_HEAD)
    yc = y - mean
    var = _seg(yc * yc, ones_bd) * (1.0 / RW_HEAD)
    yn = yc * lax.rsqrt(var + EPS) * ng
    bonus = _seg(r * kmod * rk, ones_bd) * v
    return ((yn + bonus) * g).astype(BF16)


def _rw_prompt_kernel(r_ref, k_ref, v_ref, wa_ref, gd_ref,
                      mur_ref, muk_ref, muv_ref, muwa_ref, mug_ref,
                      w0_ref, wup_ref, a0_ref, aup_ref, gup_ref, kkw_ref, kaw_ref, rk_ref, ng_ref,
                      o_ref, sout_ref,
                      s_ref, zr_s, zk_s, zv_s, zwa_s, zg_s, pr_s, pk_s, pv_s, pwa_s, pg_s,
                      *, n_chunks, tt, ns):
    t = pl.program_id(1)

    @pl.when(t == 0)
    def _():
        s_ref[...] = jnp.zeros_like(s_ref)
        for p_ in (pr_s, pk_s, pv_s, pwa_s, pg_s):
            p_[...] = jnp.zeros_like(p_)

    def mix(x_ref, prev_ref, mu_ref, dst_ref):
        for j in range(ns):
            x = x_ref[j]
            rid = lax.broadcasted_iota(jnp.int32, x.shape, 0)
            prev = jnp.where(rid == 0, prev_ref[j], pltpu.roll(x, 1, 0))
            dst_ref[j] = x + (prev - x) * mu_ref[...]
            prev_ref[j] = x[tt - 1:tt, :]

    mix(r_ref, pr_s, mur_ref, zr_s)
    mix(k_ref, pk_s, muk_ref, zk_s)
    mix(v_ref, pv_s, muv_ref, zv_s)
    mix(wa_ref, pwa_s, muwa_ref, zwa_s)
    mix(gd_ref, pg_s, mug_ref, zg_s)

    c_ = CHUNK
    r_, c2 = _tri(c_, 128)
    s_idx = c2 % c_
    strict_pk = s_idx < r_
    incl_pk = s_idx <= r_
    eye_pk = jnp.where(s_idx == r_, 1.0, 0.0)
    rr, cc = _tri(c_)
    tril = jnp.where(rr >= cc, 1.0, 0.0).astype(BF16)
    lane = lax.broadcasted_iota(jnp.int32, (c_, 128), 1)
    lo_half = lane < RW_HEAD
    r128, c128 = _tri(128)
    bdmask = (r128 // RW_HEAD) == (c128 // RW_HEAD)
    ones_bd = jnp.where(bdmask, 1.0, 0.0).astype(BF16)

    def bd(y):
        yb = y.astype(BF16)
        zero = jnp.zeros_like(yb)
        return jnp.concatenate([jnp.where(lo_half, yb, zero), jnp.where(lo_half, zero, yb)], axis=0)

    def pk_mul(x, y):
        return jnp.dot(x.astype(BF16), bd(y), preferred_element_type=F32)

    def seg_all(xs):
        st = _dot2(jnp.concatenate(xs, axis=0), ones_bd)
        return [st[i * c_:(i + 1) * c_] for i in range(len(xs))]

    def chunk(ci, carry):
        rows = pl.ds(pl.multiple_of(ci * c_, c_), c_)
        zr = [zr_s[j, rows, :] for j in range(ns)]
        zv = [zv_s[j, rows, :] for j in range(ns)]
        tok = [_rw_tokens(zr[j], zk_s[j, rows, :], zv[j], zwa_s[j, rows, :], zg_s[j, rows, :],
                          w0_ref[...], wup_ref[...], a0_ref[...], aup_ref[...],
                          gup_ref[...], kkw_ref[...], kaw_ref[...]) for j in range(ns)]
        its = [(j, p) for j in range(ns) for p in range(RW_PAIRS)]
        prs = range(len(its))
        lns = [slice(p * 128, (p + 1) * 128) for _, p in its]
        lw = [tok[j][0][:, lns[i]] for i, (j, _) in enumerate(its)]
        a = [tok[j][1][:, lns[i]] for i, (j, _) in enumerate(its)]
        g = [tok[j][2][:, lns[i]] for i, (j, _) in enumerate(its)]
        kkp = [tok[j][3][:, lns[i]] for i, (j, _) in enumerate(its)]
        kp = [tok[j][4][:, lns[i]] for i, (j, _) in enumerate(its)]
        rp = [zr[j][:, lns[i]] for i, (j, _) in enumerate(its)]
        vp = [zv[j][:, lns[i]] for i, (j, _) in enumerate(its)]
        ssq = seg_all([x * x for x in kkp])
        cum = [_dot2_left(tril, lw[p]) for p in prs]
        kkn = [kkp[p] * lax.rsqrt(ssq[p] + 1e-12) for p in prs]
        bp = [kkn[p] * a[p] for p in prs]
        cl = [cum[p][c_ - 1:c_, :] for p in prs]
        e_neg = [jnp.exp(-cum[p]) for p in prs]
        e_end = [jnp.exp(cl[p] - cum[p]) for p in prs]
        rh = [rp[p] * jnp.exp(cum[p]) for p in prs]
        kh = [kkn[p] * jnp.exp(cum[p] - lw[p]) for p in prs]
        kt = [kp[p] * e_neg[p] for p in prs]
        bt = [bp[p] * e_neg[p] for p in prs]
        kg = [kp[p] * e_end[p] for p in prs]
        bg = [bp[p] * e_end[p] for p in prs]
        nt = (((1,), (1,)), ((), ()))
        kr = [jnp.concatenate([kh[p], rh[p]], axis=0).astype(BF16) for p in prs]
        ktb = [jnp.concatenate([bd(kt[p]), bd(bt[p])], axis=0) for p in prs]
        pkb = [lax.dot_general(kr[p], ktb[p], nt, preferred_element_type=F32) for p in prs]
        ak = [jnp.where(strict_pk, pkb[p][0:c_, 0:128], 0.0) for p in prs]
        gk = [jnp.where(incl_pk, pkb[p][c_:2 * c_, 0:128], 0.0) for p in prs]
        gb = [jnp.where(incl_pk, pkb[p][c_:2 * c_, 128:256], 0.0) for p in prs]
        pw = [jnp.where(strict_pk, -pkb[p][0:c_, 128:256], 0.0) for p in prs]
        tm = [eye_pk + pw[p] for p in prs]
        agv = [pk_mul(jnp.concatenate([ak[p], gk[p]], axis=0), vp[p]) for p in prs]
        pw = [pk_mul(pw[p], pw[p]) for p in prs]
        for _ in range(4):
            x = [pk_mul(jnp.concatenate([pw[p], tm[p]], axis=0), pw[p]) for p in prs]
            tm = [tm[p] + x[p][c_:2 * c_] for p in prs]
            pw = [x[p][0:c_] for p in prs]
        tm = [tm[p] + pk_mul(tm[p], pw[p]) for p in prs]
        ku = [jnp.dot(tm[p].astype(BF16), jnp.concatenate([bd(kh[p]), bd(agv[p][0:c_])], axis=1),
                      preferred_element_type=F32) for p in prs]
        gku = [jnp.dot(gb[p].astype(BF16),
                       jnp.concatenate([bd(ku[p][:, 0:128]), bd(ku[p][:, 128:256])], axis=1),
                       preferred_element_type=F32) for p in prs]
        rhp = [rh[p] - gku[p][:, 0:128] for p in prs]
        y0 = [agv[p][c_:2 * c_] - gku[p][:, 128:256] for p in prs]
        d = [_dot_tn(jnp.concatenate([vp[p], -ku[p][:, 128:256]], axis=0),
                     jnp.concatenate([kg[p], bg[p]], axis=0)) for p in prs]
        s = [s_ref[j, p] for j, p in its]
        uy = [_dot_nt(jnp.concatenate([ku[p][:, 0:128], rhp[p]], axis=0), s[p]) for p in prs]
        ub = [_dot_tn(uy[p][0:c_], bg[p]) for p in prs]
        for i, (j, p) in enumerate(its):
            s_ref[j, p] = s[i] * jnp.exp(cl[i]) + jnp.where(bdmask, d[i] - ub[i], 0.0)
        y = [uy[p][c_:2 * c_] + y0[p] for p in prs]
        mean = seg_all(y)
        yc = [y[p] - mean[p] * (1.0 / RW_HEAD) for p in prs]
        var = seg_all([x * x for x in yc])
        bon = seg_all([rp[p] * kp[p] * rk_ref[:, lns[p]] for p in prs])
        for i, (j, _) in enumerate(its):
            yn = yc[i] * lax.rsqrt(var[i] * (1.0 / RW_HEAD) + EPS) * ng_ref[:, lns[i]]
            o_ref[j, rows, lns[i]] = ((yn + bon[i] * vp[i]) * g[i]).astype(BF16)
        return carry

    lax.fori_loop(0, n_chunks, chunk, 0)

    @pl.when(t == pl.num_programs(1) - 1)
    def _():
        for j in range(ns):
            for p in range(RW_PAIRS):
                s = s_ref[j, p]
                sout_ref[j, 2 * p] = s[0:RW_HEAD, 0:RW_HEAD]
                sout_ref[j, 2 * p + 1] = s[RW_HEAD:, RW_HEAD:]


def _rw_prompt(z, mus, params, b_, t_, tt):
    ns = _seq_per_step(b_)
    nt = t_ // tt
    z3 = z.reshape(b_, t_, NP)
    row = lambda blk: (lambda b, t: (b, t, blk))
    const = lambda b, t: (0, 0)
    full = lambda a: pl.BlockSpec(a.shape, const)
    rw_blk = OFF_RW // RW_W
    o, s_new = pl.pallas_call(
        functools.partial(_rw_prompt_kernel, n_chunks=tt // CHUNK, tt=tt, ns=ns),
        grid=(b_ // ns, nt),
        in_specs=[pl.BlockSpec((ns, tt, RW_W), row(rw_blk)),
                  pl.BlockSpec((ns, tt, RW_W), row(rw_blk + 1)),
                  pl.BlockSpec((ns, tt, RW_W), row(rw_blk + 2)),
                  pl.BlockSpec((ns, tt, 128), row((OFF_RW + 3 * RW_W) // 128)),
                  pl.BlockSpec((ns, tt, 128), row((OFF_RW + 3 * RW_W) // 128 + 1))]
                 + [full(a) for a in mus] + [full(a) for a in params],
        out_specs=[pl.BlockSpec((ns, tt, RW_W), lambda b, t: (b, t, 0)),
                   pl.BlockSpec((ns, RW_HEADS, RW_HEAD, RW_HEAD), lambda b, t: (b, 0, 0, 0))],
        out_shape=[jax.ShapeDtypeStruct((b_, t_, RW_W), BF16),
                   jax.ShapeDtypeStruct((b_, RW_HEADS, RW_HEAD, RW_HEAD), F32)],
        scratch_shapes=[pltpu.VMEM((ns, RW_PAIRS, 128, 128), F32),
                        pltpu.VMEM((ns, tt, RW_W), F32), pltpu.VMEM((ns, tt, RW_W), F32),
                        pltpu.VMEM((ns, tt, RW_W), F32), pltpu.VMEM((ns, tt, 128), F32),
                        pltpu.VMEM((ns, tt, 128), F32),
                        pltpu.VMEM((ns, 1, RW_W), F32), pltpu.VMEM((ns, 1, RW_W), F32),
                        pltpu.VMEM((ns, 1, RW_W), F32), pltpu.VMEM((ns, 1, 128), F32),
                        pltpu.VMEM((ns, 1, 128), F32)],
        compiler_params=pltpu.CompilerParams(
            dimension_semantics=("parallel", "arbitrary"), vmem_limit_bytes=VMEM_LIMIT),
        name="rwkv_prompt",
    )(z3, z3, z3, z3, z3, *mus, *params)
    return o.reshape(b_ * t_, RW_W), s_new


def _sample_prep_kernel(sm_ref, aup_ref, ab_ref, mi_ref, mf_ref, m_ref, ib_ref, fb_ref,
                        r_ref, k_ref, v_ref, wa_ref, gd_ref, sh_r, sh_k, sh_v, sh_wa, sh_g,
                        mur_ref, muk_ref, muv_ref, muwa_ref, mug_ref,
                        w0_ref, wup_ref, a0_ref, aup2_ref, gup_ref, kkw_ref, kaw_ref,
                        gla_la_ref, mnew_ref, wend_ref, ws_ref,
                        rr_ref, rk_ref, rv_ref, rkk_ref, rb_ref, rw_ref, rg_ref):
    gla_la_ref[...] = _logsig(_dot(sm_ref[...], aup_ref[...]) + ab_ref[...]) * (1.0 / GLA_TAU)
    i_pre = mi_ref[...] + ib_ref[...]
    logf = _logsig(mf_ref[...] + fb_ref[...])
    m = m_ref[...]
    m_new = jnp.maximum(logf + m, i_pre)
    mnew_ref[...] = m_new
    wend_ref[...] = jnp.exp(logf + m - m_new)
    ws_ref[...] = jnp.exp(i_pre - m_new)
    mixf = lambda x, s, mu: x[...] + (s[...] - x[...]) * mu[...]
    zr = mixf(r_ref, sh_r, mur_ref)
    zk = mixf(k_ref, sh_k, muk_ref)
    zv = mixf(v_ref, sh_v, muv_ref)
    zwa = mixf(wa_ref, sh_wa, muwa_ref)
    zg = mixf(gd_ref, sh_g, mug_ref)
    lw, a, g, kk0, kmod = _rw_tokens(zr, zk, zv, zwa, zg, w0_ref[...], wup_ref[...], a0_ref[...],
                                     aup2_ref[...], gup_ref[...], kkw_ref[...], kaw_ref[...])
    ones_bd = _ones_bd()
    for p in range(RW_PAIRS):
        ln = slice(p * 128, (p + 1) * 128)
        kkp = kk0[:, ln]
        kkn = kkp * lax.rsqrt(_seg(kkp * kkp, ones_bd) + 1e-12)
        rkk_ref[:, ln] = kkn
        rb_ref[:, ln] = kkn * a[:, ln]
    rr_ref[...] = zr
    rk_ref[...] = kmod
    rv_ref[...] = zv
    rw_ref[...] = jnp.exp(lw)
    rg_ref[...] = g


STEP_BB = 8


def _row_select(rows_list, rid):
    acc = rows_list[0]
    for b in range(1, len(rows_list)):
        acc = jnp.where(rid == b, rows_list[b], acc)
    return acc


def _gla_step_kernel(s_ref, q_ref, k_ref, la_ref, v_ref, *rest):
    sn_ref, o_ref = rest[-2:]
    rid_k = lax.broadcasted_iota(jnp.int32, (STEP_BB, GLA_DK), 0)
    rid_v = lax.broadcasted_iota(jnp.int32, (STEP_BB, GLA_DV), 0)
    bs = range(STEP_BB)
    onehot = [jnp.where(rid_v == b, 1.0, 0.0).astype(BF16) for b in bs]
    for h in range(GLA_HEADS):
        kl = slice(h * GLA_DK, (h + 1) * GLA_DK)
        vl = slice(h * GLA_DV, (h + 1) * GLA_DV)
        qt = q_ref[:, kl] * (GLA_DK ** -0.5)
        kt = k_ref[:, kl]
        lt = la_ref[:, kl]
        vt = v_ref[:, vl]
        dec = [jnp.exp(_dot2_tn(lt, onehot[b])) for b in bs]
        kv = [_dot3(jnp.where(rid_k == b, kt, 0.0), vt, _TN) for b in bs]
        sn = [dec[b] * s_ref[0, b, h] + kv[b] for b in bs]
        for b in bs:
            sn_ref[0, b, h] = sn[b]
        o_ref[:, vl] = _row_select([_dot3(qt, sn[b]) for b in bs], rid_v)


def _mlstm_step_kernel(c_ref, n_ref, q_ref, k_ref, v_ref, wend_ref, ws_ref, mnew_ref, *rest):
    cn_ref, nn_ref, h_ref = rest[-3:]
    rid = lax.broadcasted_iota(jnp.int32, (STEP_BB, ML_DH), 0)
    for h in range(ML_HEADS):
        hl = slice(h * ML_DH, (h + 1) * ML_DH)
        qt = q_ref[:, hl] * (ML_DH ** -0.5)
        kt = k_ref[:, hl]
        vt = v_ref[:, hl]
        wend = wend_ref[:, h:h + 1]
        ws = ws_ref[:, h:h + 1]
        nn = wend * n_ref[0, :, hl] + ws * kt
        nn_ref[0, :, hl] = nn
        den = jnp.sum(qt * nn, axis=1, keepdims=True)
        kws = kt * ws
        nums = []
        for half in range(2):
            bs = range(half * 4, half * 4 + 4)
            kwv = [_dot3(jnp.where(rid == b, kws, 0.0), vt, _TN) for b in bs]
            cn = [wend_ref[b:b + 1, h:h + 1] * c_ref[0, b, h] + kwv[i] for i, b in enumerate(bs)]
            for i, b in enumerate(bs):
                cn_ref[0, b, h] = cn[i]
            nums += [_dot3(qt, cn[i]) for i in range(4)]
        num = _row_select(nums, rid)
        h_ref[:, hl] = num / jnp.maximum(jnp.abs(den), jnp.exp(-mnew_ref[:, h:h + 1]))


def _rw_step_kernel(s_ref, kk_ref, b_ref, k_ref, r_ref, w_ref, v_ref, *rest):
    sn_ref, y_ref = rest[-2:]
    n = RW_HEAD
    rid = lax.broadcasted_iota(jnp.int32, (STEP_BB, n), 0)
    bs = range(STEP_BB)
    onehot = [jnp.where(rid == b, 1.0, 0.0).astype(BF16) for b in bs]
    for h in range(RW_HEADS):
        hl = slice(h * n, (h + 1) * n)
        kkt, bt, kt, rt, wt = kk_ref[:, hl], b_ref[:, hl], k_ref[:, hl], r_ref[:, hl], w_ref[:, hl]
        vhi, vlo = _split(v_ref[:, hl])
        tn = lambda x, y: lax.dot_general(x, y, _TN, preferred_element_type=F32)
        vcol = [tn(vhi, onehot[b]) + tn(vlo, onehot[b]) for b in bs]
        s = [s_ref[0, b, h] for b in bs]
        sa = [_dot3(s[b], jnp.broadcast_to(-kkt[b:b + 1, :], (n, n)), _NT) for b in bs]
        sn = [s[b] * wt[b:b + 1, :] + sa[b] * bt[b:b + 1, :] + vcol[b] * kt[b:b + 1, :] for b in bs]
        for b in bs:
            sn_ref[0, b, h] = sn[b]
        yr = [_dot3(jnp.broadcast_to(rt[b:b + 1, :], (STEP_BB, n)), sn[b], _NT) for b in bs]
        y_ref[:, hl] = _row_select(yr, rid)


def _sample_post_kernel(og_ref, gg_ref, gn_ref, hm_ref, mo_ref, mn_ref,
                        y_ref, r_ref, k_ref, v_ref, g_ref, rk_ref, ng_ref,
                        ogla_ref, oml_ref, orw_ref):
    for h in range(GLA_HEADS):
        vl = slice(h * GLA_DV, (h + 1) * GLA_DV)
        o = og_ref[:, vl]
        on = o * lax.rsqrt(jnp.mean(o * o, axis=-1, keepdims=True) + EPS) * gn_ref[...]
        gg = gg_ref[:, vl]
        ogla_ref[:, vl] = (on * (gg * _sigmoid(gg))).astype(BF16)
    for h in range(ML_HEADS):
        hl = slice(h * ML_DH, (h + 1) * ML_DH)
        hm = hm_ref[:, hl]
        y = hm - jnp.mean(hm, axis=-1, keepdims=True)
        y = y * lax.rsqrt(jnp.mean(y * y, axis=-1, keepdims=True) + EPS) * mn_ref[...]
        oml_ref[:, hl] = (y * _sigmoid(mo_ref[:, hl])).astype(BF16)
    ones_bd = _ones_bd()
    for p in range(RW_PAIRS):
        ln = slice(p * 128, (p + 1) * 128)
        orw_ref[:, ln] = _rw_epilogue(y_ref[:, ln], r_ref[:, ln], k_ref[:, ln], v_ref[:, ln],
                                      g_ref[:, ln], rk_ref[:, ln], ng_ref[:, ln], ones_bd)


def _full_call(kernel_fn, args, out_shapes, name):
    nd = lambda a: (lambda: (0,) * len(a.shape))
    return pl.pallas_call(
        kernel_fn,
        in_specs=[pl.BlockSpec(a.shape, nd(a)) for a in args],
        out_specs=[pl.BlockSpec(s.shape, nd(s)) for s in out_shapes],
        out_shape=out_shapes,
        compiler_params=pltpu.CompilerParams(vmem_limit_bytes=VMEM_LIMIT),
        name=name,
    )(*args)


def _state_call(kernel_fn, l, states, vecs, prev_outs, out_width, name):
    nb = states[0].shape[1]
    st_spec = lambda a: pl.BlockSpec((1, STEP_BB) + tuple(a.shape[2:]),
                                     lambda i, n=a.ndim: (l, i) + (0,) * (n - 2))
    in_specs = [st_spec(a) for a in states]
    in_specs += [pl.BlockSpec((STEP_BB, w), lambda i, c=c: (i, c)) for _, w, c in vecs]
    args = list(states) + [a for a, _, _ in vecs]
    aliases = {}
    for j, po in enumerate(prev_outs or ()):
        aliases[len(args)] = j
        in_specs.append(pl.BlockSpec(memory_space=pl.ANY))
        args.append(po)
    out_shape = [jax.ShapeDtypeStruct(a.shape, F32) for a in states]
    out_shape.append(jax.ShapeDtypeStruct((nb, out_width), F32))
    out_specs = [st_spec(a) for a in states] + [pl.BlockSpec((STEP_BB, out_width), lambda i: (i, 0))]
    return pl.pallas_call(
        kernel_fn,
        grid=(nb // STEP_BB,),
        in_specs=in_specs,
        out_specs=out_specs,
        out_shape=out_shape,
        input_output_aliases=aliases,
        compiler_params=pltpu.CompilerParams(
            dimension_semantics=("parallel",), vmem_limit_bytes=VMEM_LIMIT),
        name=name,
    )(*args)


def _prep_layer_params(l, p):
    w = p["w_in"][l]
    w_in_p = jnp.concatenate(
        [w[:, 0:2048], w[:, 2064:3088], w[:, 3088:6160], w[:, 6168:7192], w[:, 7192:13592],
         w[:, 2048:2064], w[:, 6160:6168], jnp.zeros((D_MODEL, NP - 13592), F32)], axis=1).astype(BF16)
    aup_pad = jnp.zeros((128, GLA_QK), F32).at[0:GLA_RANK].set(p["gla_a_up"][l])
    brow = (jnp.zeros((1, 128), F32).at[0, SM_I:SM_I + 4].set(p["ml_i_bias"][l])
            .at[0, SM_F:SM_F + 4].set(p["ml_f_bias"][l]))
    bcol = jnp.broadcast_to(jnp.concatenate([p["ml_i_bias"][l], p["ml_f_bias"][l]])[:, None], (8, CHUNK))
    mu = p["rw_mu"][l][None, :]
    mus = (mu[:, 0:1024], mu[:, 1024:2048], mu[:, 2048:3072], mu[:, 3072:3200], mu[:, 3200:3328])
    wup_pad = jnp.zeros((128, RW_W), F32).at[0:64].set(p["rw_w_up"][l])
    aup2_pad = jnp.zeros((128, RW_W), F32).at[64:128].set(p["rw_a_up"][l])
    row = lambda a: a.reshape(1, -1)
    rw_params = (row(p["rw_w0"][l]), wup_pad, row(p["rw_a0"][l]), aup2_pad, p["rw_g_up"][l],
                 row(p["rw_k_k"][l]), row(p["rw_k_a"][l]), row(p["rw_r_k"][l]), row(p["rw_norm_g"][l]))
    return dict(
        norm1_g=row(p["norm1_g"][l]), w_in=w_in_p, aup_pad=aup_pad, ab=row(p["gla_a_bias"][l]),
        gla_ng=row(p["gla_norm_g"][l]), brow=brow, bcol=bcol, ib=row(p["ml_i_bias"][l]),
        fb=row(p["ml_f_bias"][l]), ml_ng=row(p["ml_norm_g"][l]), mus=mus, rw=rw_params,
        wb=p["w_branch"][l].astype(BF16), wo=p["w_out"][l].astype(BF16), norm2_g=row(p["norm2_g"][l]),
        w1=p["w_ff1"][l].astype(BF16), w2=p["w_ff2"][l].astype(BF16))


def _prompt_layer(x, lp, fg, final, b_, t_):
    rows = x.shape[0]
    tm = min(512, rows)
    tm_big = min(1024, rows)
    z = _proj_in(x, lp["norm1_g"], lp["w_in"], tm_big)
    tt = min(256, t_)
    o_gla, gla_new = _gla_prompt(z, lp["aup_pad"], lp["ab"], lp["gla_ng"], b_, t_, tt)
    sm = z[:, OFF_SM + SM_I:OFF_SM + SM_I + 8]
    smt = sm.reshape(b_, t_ // CHUNK, CHUNK, 8).transpose(0, 1, 3, 2)
    o_ml, c_new, n_new, m_new = _mlstm_prompt(z, smt, lp["brow"], lp["bcol"], lp["ml_ng"], b_, t_, tt)
    o_rw, rw_new = _rw_prompt(z, lp["mus"], lp["rw"], b_, t_, tt)
    x = _merge(o_gla, o_ml, o_rw, z, x, lp["wb"], lp["wo"], tm)
    x = _ffn(x, lp["norm2_g"], lp["w1"], lp["w2"], fg, final, tm_big)
    shift = z.reshape(b_, t_, NP)[:, t_ - 1, OFF_RW:OFF_RW + RW_COLS]
    return x, (gla_new, c_new, n_new, m_new[:, :, 0], rw_new, shift)


def _sample_layer(l, x, states, prev, lp, fg, final):
    st_gla, st_c, st_n, st_m, st_rw, st_shift = states
    nb = x.shape[0]
    z = _proj_in(x, lp["norm1_g"], lp["w_in"], nb)
    zc = lambda off, w: z[:, off:off + w]
    shift = st_shift[l]
    sh = (shift[:, 0:1024], shift[:, 1024:2048], shift[:, 2048:3072], shift[:, 3072:3200], shift[:, 3200:3328])
    w0, wup, a0, aup2, gup, kkw, kaw, rk, ng = lp["rw"]
    sds = lambda *s: jax.ShapeDtypeStruct(s, F32)
    prep_args = [zc(OFF_SM, 128), lp["aup_pad"], lp["ab"],
                 zc(OFF_SM + SM_I, 4), zc(OFF_SM + SM_F, 4), st_m[l], lp["ib"], lp["fb"],
                 zc(OFF_RW, 1024), zc(OFF_RW + 1024, 1024), zc(OFF_RW + 2048, 1024),
                 zc(OFF_RW + 3072, 128), zc(OFF_RW + 3200, 128), *sh, *lp["mus"],
                 w0, wup, a0, aup2, gup, kkw, kaw]
    (la, m_new, wend, ws, rr, rkm, rv, rkk, rb, rw, rg) = _full_call(
        _sample_prep_kernel, prep_args,
        [sds(nb, GLA_QK), sds(nb, 4), sds(nb, 4), sds(nb, 4)] + [sds(nb, RW_W)] * 7, "sample_prep")

    pv = lambda *idx: None if prev is None else [prev[i] for i in idx]
    gla_all, o_g = _state_call(
        _gla_step_kernel, l, [st_gla],
        [(z, GLA_QK, OFF_GQ // GLA_QK), (z, GLA_QK, OFF_GK // GLA_QK), (la, GLA_QK, 0),
         (z, GLA_V, OFF_GV // GLA_V)], pv(0), GLA_V, "gla_step")
    c_all, n_all, hm = _state_call(
        _mlstm_step_kernel, l, [st_c, st_n],
        [(z, ML_W, OFF_MQ // ML_W), (z, ML_W, OFF_MK // ML_W), (z, ML_W, OFF_MV // ML_W),
         (wend, 4, 0), (ws, 4, 0), (m_new, 4, 0)], pv(1, 2), ML_W, "mlstm_step")
    rw_all, y = _state_call(
        _rw_step_kernel, l, [st_rw],
        [(a, RW_W, 0) for a in (rkk, rb, rkm, rr, rw, rv)], pv(3), RW_W, "rwkv_step")

    bf = lambda: jax.ShapeDtypeStruct((nb, D_MODEL), BF16)
    o_gla, o_ml, o_rw = _full_call(
        _sample_post_kernel,
        [o_g, zc(OFF_GG, GLA_V), lp["gla_ng"], hm, zc(OFF_MO, ML_W), lp["ml_ng"],
         y, rr, rkm, rv, rg, rk, ng],
        [bf(), bf(), bf()], "sample_post")
    x = _merge(o_gla, o_ml, o_rw, z, x, lp["wb"], lp["wo"], nb)
    x = _ffn(x, lp["norm2_g"], lp["w1"], lp["w2"], fg, final, nb)
    return x, (gla_all, c_all, n_all, rw_all), (m_new, zc(OFF_RW, RW_COLS))


def kernel(x_prompt, x_sample, state_gla, state_mlstm_c, state_mlstm_n, state_mlstm_m, state_rwkv, state_shift, norm1_g, w_in, gla_a_up, gla_a_bias, gla_norm_g, ml_i_bias, ml_f_bias, ml_norm_g, rw_mu, rw_w0, rw_w_up, rw_a0, rw_a_up, rw_g_up, rw_k_k, rw_k_a, rw_r_k, rw_norm_g, w_branch, w_out, norm2_g, w_ff1, w_ff2, final_g):
    p = dict(norm1_g=norm1_g, w_in=w_in, gla_a_up=gla_a_up, gla_a_bias=gla_a_bias, gla_norm_g=gla_norm_g,
             ml_i_bias=ml_i_bias, ml_f_bias=ml_f_bias, ml_norm_g=ml_norm_g, rw_mu=rw_mu, rw_w0=rw_w0,
             rw_w_up=rw_w_up, rw_a0=rw_a0, rw_a_up=rw_a_up, rw_g_up=rw_g_up, rw_k_k=rw_k_k, rw_k_a=rw_k_a,
             rw_r_k=rw_r_k, rw_norm_g=rw_norm_g, w_branch=w_branch, w_out=w_out, norm2_g=norm2_g,
             w_ff1=w_ff1, w_ff2=w_ff2)
    depth = w_in.shape[0]
    b_, t_, _ = x_prompt.shape
    nb = x_sample.shape[0]
    fg = final_g.reshape(1, D_MODEL)
    xp = x_prompt.reshape(b_ * t_, D_MODEL)
    xs = x_sample.reshape(nb, D_MODEL)
    s_in = (state_gla, state_mlstm_c, state_mlstm_n.reshape(depth, nb, ML_W), state_mlstm_m, state_rwkv,
            state_shift)
    p_states, s_big, s_small = [], None, []
    for l in range(depth):
        lp = _prep_layer_params(l, p)
        final = l == depth - 1
        xp, stp = _prompt_layer(xp, lp, fg, final, b_, t_)
        xs, s_big, small = _sample_layer(l, xs, s_in, s_big, lp, fg, final)
        p_states.append(stp)
        s_small.append(small)
    ps = [jnp.stack([s[i] for s in p_states]) for i in range(6)]
    gla_s, mlc_s, mln_s, rw_s = s_big
    mln_s = mln_s.reshape(depth, nb, ML_HEADS, ML_DH)
    mlm_s = jnp.stack([s[0] for s in s_small])
    shift_s = jnp.stack([s[1] for s in s_small])
    return (xp.reshape(b_, t_, D_MODEL), xs.reshape(nb, 1, D_MODEL),
            ps[0], gla_s, ps[1], mlc_s, ps[2], mln_s, ps[3], mlm_s, ps[4], rw_s, ps[5], shift_s)
```

```python
import functools

import jax
import jax.numpy as jnp
from jax import lax
from jax.experimental import pallas as pl
from jax.experimental.pallas import tpu as pltpu

F32 = jnp.float32
BF16 = jnp.bfloat16

D_MODEL = 1024
GLA_HEADS, GLA_DK, GLA_DV = 4, 128, 256
GLA_QK, GLA_V, GLA_RANK, GLA_TAU = 512, 1024, 16, 16.0
ML_HEADS, ML_DH, ML_W = 4, 256, 1024
RW_HEAD, RW_W, RW_HEADS = 64, 1024, 16
RW_PAIRS = RW_HEADS // 2
RW_COLS = 3 * RW_W + 64 + 64 + 128
D_FF = 4 * D_MODEL
EPS = 1e-6
CHUNK = 64

OFF_GQ, OFF_GK, OFF_GV, OFF_GG = 0, 512, 1024, 2048
OFF_MQ, OFF_MK, OFF_MV, OFF_MO = 3072, 4096, 5120, 6144
OFF_GATE = 7168
OFF_RW = 10240
OFF_SM = 13568
NP = 13824
SM_I, SM_F = 16, 20

VMEM_LIMIT = 48 * 1024 * 1024
MIXER_VMEM_LIMIT = 56 * 1024 * 1024
MIXER_TT = 128


def _dot(a, b):
    return jnp.dot(a.astype(BF16), b.astype(BF16), preferred_element_type=F32)


def _dot_nt(a, b):
    return lax.dot_general(a.astype(BF16), b.astype(BF16), (((1,), (1,)), ((), ())),
                           preferred_element_type=F32)


def _dot_tn(a, b):
    return lax.dot_general(a.astype(BF16), b.astype(BF16), (((0,), (0,)), ((), ())),
                           preferred_element_type=F32)


def _split(x):
    hi = x.astype(BF16)
    lo = (x - hi.astype(F32)).astype(BF16)
    return hi, lo


def _dot2(x, w_exact):
    hi, lo = _split(x)
    return (jnp.dot(hi, w_exact, preferred_element_type=F32)
            + jnp.dot(lo, w_exact, preferred_element_type=F32))


def _dot2_tn(x, w_exact):
    hi, lo = _split(x)
    dn = (((0,), (0,)), ((), ()))
    return (lax.dot_general(hi, w_exact, dn, preferred_element_type=F32)
            + lax.dot_general(lo, w_exact, dn, preferred_element_type=F32))


def _dot2_left(w_exact, x):
    hi, lo = _split(x)
    return (jnp.dot(w_exact, hi, preferred_element_type=F32)
            + jnp.dot(w_exact, lo, preferred_element_type=F32))


_NN = (((1,), (0,)), ((), ()))
_TN = (((0,), (0,)), ((), ()))
_NT = (((1,), (1,)), ((), ()))


def _dot3(a, b, dn=_NN):
    ah, al = _split(a)
    bh, bl = _split(b)
    f = lambda x, y: lax.dot_general(x, y, dn, preferred_element_type=F32)
    return f(ah, bh) + f(al, bh) + f(ah, bl)


def _logsig(x):
    return jnp.minimum(x, 0.0) - jnp.log1p(jnp.exp(-jnp.abs(x)))


def _sigmoid(x):
    return jax.nn.sigmoid(x)


def _rms(x, g):
    return x * lax.rsqrt(jnp.mean(x * x, axis=-1, keepdims=True) + EPS) * g


def _tri(n, m=None):
    m = n if m is None else m
    r = lax.broadcasted_iota(jnp.int32, (n, m), 0)
    c = lax.broadcasted_iota(jnp.int32, (n, m), 1)
    return r, c


def _proj_in_kernel(x_ref, g_ref, w_ref, o_ref, h_ref):
    @pl.when(pl.program_id(1) == 0)
    def _():
        h_ref[...] = _rms(x_ref[...], g_ref[...]).astype(BF16)

    o_ref[...] = jnp.dot(h_ref[...], w_ref[...], preferred_element_type=F32)


def _proj_in(x, g, w, tm, tn=1536):
    m = x.shape[0]
    return pl.pallas_call(
        _proj_in_kernel,
        grid=(m // tm, NP // tn),
        in_specs=[pl.BlockSpec((tm, D_MODEL), lambda i, j: (i, 0)),
                  pl.BlockSpec((1, D_MODEL), lambda i, j: (0, 0)),
                  pl.BlockSpec((D_MODEL, tn), lambda i, j: (0, j))],
        out_specs=pl.BlockSpec((tm, tn), lambda i, j: (i, j)),
        out_shape=jax.ShapeDtypeStruct((m, NP), F32),
        scratch_shapes=[pltpu.VMEM((tm, D_MODEL), BF16)],
        compiler_params=pltpu.CompilerParams(
            dimension_semantics=("parallel", "arbitrary"), vmem_limit_bytes=VMEM_LIMIT),
        name="proj_in",
    )(x, g, w)


def _merge_kernel(og_ref, om_ref, or_ref, g0_ref, g1_ref, g2_ref, x_ref, wb_ref, wo_ref, o_ref):
    acc = jnp.dot(og_ref[...], wb_ref[0], preferred_element_type=F32) * _sigmoid(g0_ref[...])
    acc += jnp.dot(om_ref[...], wb_ref[1], preferred_element_type=F32) * _sigmoid(g1_ref[...])
    acc += jnp.dot(or_ref[...], wb_ref[2], preferred_element_type=F32) * _sigmoid(g2_ref[...])
    o_ref[...] = x_ref[...] + jnp.dot(acc.astype(BF16), wo_ref[...], preferred_element_type=F32)


def _merge(o_gla, o_ml, o_rw, z, x, wb, wo, tm):
    m = x.shape[0]
    row = lambda i: (i, 0)
    gate_blk = OFF_GATE // D_MODEL
    return pl.pallas_call(
        _merge_kernel,
        grid=(m // tm,),
        in_specs=[pl.BlockSpec((tm, D_MODEL), row),
                  pl.BlockSpec((tm, D_MODEL), row),
                  pl.BlockSpec((tm, D_MODEL), row),
                  pl.BlockSpec((tm, D_MODEL), lambda i: (i, gate_blk)),
                  pl.BlockSpec((tm, D_MODEL), lambda i: (i, gate_blk + 1)),
                  pl.BlockSpec((tm, D_MODEL), lambda i: (i, gate_blk + 2)),
                  pl.BlockSpec((tm, D_MODEL), row),
                  pl.BlockSpec((3, D_MODEL, D_MODEL), lambda i: (0, 0, 0)),
                  pl.BlockSpec((D_MODEL, D_MODEL), lambda i: (0, 0))],
        out_specs=pl.BlockSpec((tm, D_MODEL), row),
        out_shape=jax.ShapeDtypeStruct((m, D_MODEL), F32),
        compiler_params=pltpu.CompilerParams(
            dimension_semantics=("parallel",), vmem_limit_bytes=VMEM_LIMIT),
        name="merge",
    )(o_gla, o_ml, o_rw, z, z, z, x, wb, wo)


def _ffn_kernel(x_ref, g_ref, w1_ref, w2_ref, fg_ref, o_ref, h_ref, acc_ref, *, final):
    f = pl.program_id(1)

    @pl.when(f == 0)
    def _():
        h_ref[...] = _rms(x_ref[...], g_ref[...]).astype(BF16)
        acc_ref[...] = jnp.zeros_like(acc_ref)

    a = jnp.dot(h_ref[...], w1_ref[...], preferred_element_type=F32)
    a = jnp.square(jnp.maximum(a, 0.0))
    acc_ref[...] += jnp.dot(a.astype(BF16), w2_ref[...], preferred_element_type=F32)

    @pl.when(f == pl.num_programs(1) - 1)
    def _():
        y = x_ref[...] + acc_ref[...]
        if final:
            y = _rms(y, fg_ref[...])
        o_ref[...] = y


def _ffn(x, g, w1, w2, fg, final, tm, tf=512):
    m = x.shape[0]
    return pl.pallas_call(
        functools.partial(_ffn_kernel, final=final),
        grid=(m // tm, D_FF // tf),
        in_specs=[pl.BlockSpec((tm, D_MODEL), lambda i, f: (i, 0)),
                  pl.BlockSpec((1, D_MODEL), lambda i, f: (0, 0)),
                  pl.BlockSpec((D_MODEL, tf), lambda i, f: (0, f)),
                  pl.BlockSpec((tf, D_MODEL), lambda i, f: (f, 0)),
                  pl.BlockSpec((1, D_MODEL), lambda i, f: (0, 0))],
        out_specs=pl.BlockSpec((tm, D_MODEL), lambda i, f: (i, 0)),
        out_shape=jax.ShapeDtypeStruct((m, D_MODEL), F32),
        scratch_shapes=[pltpu.VMEM((tm, D_MODEL), BF16), pltpu.VMEM((tm, D_MODEL), F32)],
        compiler_params=pltpu.CompilerParams(
            dimension_semantics=("parallel", "arbitrary"), vmem_limit_bytes=VMEM_LIMIT),
        name="ffn",
    )(x, g, w1, w2, fg)


def _rw_tokens(zr, zk, zv, zwa, zg, w0, wup, a0, aup, gup, kkw, kaw):
    u = w0 + _dot(jnp.tanh(zwa), wup)
    lw = -jnp.exp(_logsig(u) - 0.5)
    a = _sigmoid(a0 + _dot(zwa, aup))
    g = _dot(_sigmoid(zg), gup)
    kk0 = zk * kkw
    kmod = zk * (1.0 + (a - 1.0) * kaw)
    return lw, a, g, kk0, kmod


def _ones_bd():
    r, c = _tri(128)
    return jnp.where((r // RW_HEAD) == (c // RW_HEAD), 1.0, 0.0).astype(BF16)


def _seg(x, ones_bd):
    return _dot2(x, ones_bd)


def _rw_epilogue(y, r, kmod, v, g, rk, ng, ones_bd):
    mean = _seg(y, ones_bd) * (1.0 / RW_HEAD)
    yc = y - mean
    var = _seg(yc * yc, ones_bd) * (1.0 / RW_HEAD)
    yn = yc * lax.rsqrt(var + EPS) * ng
    bonus = _seg(r * kmod * rk, ones_bd) * v
    return ((yn + bonus) * g).astype(BF16)


def _seq_per_step(b_):
    return 2 if b_ % 2 == 0 else 1


def _interleave(gens, pattern):
    live = dict(gens)
    while live:
        for key in pattern:
            g = live.get(key)
            if g is None:
                continue
            try:
                next(g)
            except StopIteration:
                del live[key]


def _gla_chunk(rows, ns, q_ref, k_ref, v_ref, gg_ref, sm_ref, aup_ref, ab_ref, gn_ref, o_ref, s_ref):
    c_ = CHUNK
    r, c = _tri(c_)
    incl = r >= c
    tril = jnp.where(incl, 1.0, 0.0).astype(BF16)
    ones_c = jnp.ones((c_, GLA_DK), BF16)
    tokw = []
    for j in range(ns):
        x = _dot(sm_ref[j, rows, :], aup_ref[...]) + ab_ref[...]
        la = _logsig(x) * (1.0 / GLA_TAU)
        b = _dot2_left(tril, la)
        dec = jnp.exp(_dot2_tn(la, ones_c))
        q = q_ref[j, rows, :] * (GLA_DK ** -0.5)
        k = k_ref[j, rows, :]
        tokw.append((q * jnp.exp(b), k * jnp.exp(-b), k * jnp.exp(b[c_ - 1:c_, :] - b), dec))
    yield
    its = [(j, h) for j in range(ns) for h in range(GLA_HEADS)]
    hs = range(len(its))
    kls = [slice(h * GLA_DK, (h + 1) * GLA_DK) for _, h in its]
    vls = [slice(h * GLA_DV, (h + 1) * GLA_DV) for _, h in its]
    qe = [tokw[j][0][:, kls[i]] for i, (j, _) in enumerate(its)]
    ke = [tokw[j][1][:, kls[i]] for i, (j, _) in enumerate(its)]
    kd = [tokw[j][2][:, kls[i]] for i, (j, _) in enumerate(its)]
    v = [v_ref[j, rows, vls[i]] for i, (j, _) in enumerate(its)]
    att = [jnp.where(incl, _dot_nt(qe[i], ke[i]), 0.0) for i in hs]
    yield
    s = [s_ref[j, h] for j, h in its]
    oi = [_dot(qe[i], s[i]) for i in hs]
    yield
    kv = [_dot_tn(kd[i], v[i]) for i in hs]
    yield
    o = [oi[i] + _dot(att[i], v[i]) for i in hs]
    for i, (j, h) in enumerate(its):
        dh = tokw[j][3][kls[i], :]
        s_ref[j, h] = s[i] * jnp.concatenate([dh, dh], axis=1) + kv[i]
    yield
    msq = [jnp.mean(o[i] * o[i], axis=-1, keepdims=True) for i in hs]
    yield
    for i, (j, _) in enumerate(its):
        on = o[i] * lax.rsqrt(msq[i] + EPS) * gn_ref[...]
        gg = gg_ref[j, rows, vls[i]]
        o_ref[j, rows, vls[i]] = (on * (gg * _sigmoid(gg))).astype(BF16)


def _mlstm_chunk(rows, ci, ns, q_ref, k_ref, v_ref, mo_ref, sm_ref, smt_ref, brow_ref, bcol_ref, ng_ref,
                 o_ref, c_ref, n_ref, m_ref):
    c_ = CHUNK
    r, c = _tri(c_)
    incl = r >= c
    tril = jnp.where(incl, 1.0, 0.0).astype(BF16)
    triu = jnp.where(r <= c, 1.0, 0.0).astype(BF16)
    sub8 = lax.broadcasted_iota(jnp.int32, (8, c_), 0)
    sm, fcol, smt, frow = [], [], [], []
    for j in range(ns):
        sm.append(sm_ref[j, rows, :] + brow_ref[...])
        fcol.append(_dot2_left(tril, _logsig(sm[j])))
        smt.append(smt_ref[j, ci] + bcol_ref[...])
        frow.append(_dot2(jnp.where(sub8 >= 4, _logsig(smt[j]), 0.0), triu))
    yield
    its = [(j, h) for j in range(ns) for h in range(ML_HEADS)]
    hs = range(len(its))
    hls = [slice(h * ML_DH, (h + 1) * ML_DH) for _, h in its]
    f_t = [fcol[j][:, SM_F + h:SM_F + h + 1] for j, h in its]
    i_t = [sm[j][:, SM_I + h:SM_I + h + 1] for j, h in its]
    f_s = [frow[j][4 + h:5 + h, :] for j, h in its]
    i_s = [smt[j][h:h + 1, :] for j, h in its]
    m = [m_ref[j, h][0:1, 0:1] for j, h in its]
    q = [q_ref[j, rows, hls[i]] * (ML_DH ** -0.5) for i, (j, _) in enumerate(its)]
    k = [k_ref[j, rows, hls[i]] for i, (j, _) in enumerate(its)]
    v = [v_ref[j, rows, hls[i]] for i, (j, _) in enumerate(its)]
    qk0 = [_dot_nt(q[i], k[i]) for i in hs]
    gmax = [jnp.max(jnp.where(incl, i_s[i] - f_s[i], -jnp.inf), axis=1, keepdims=True) for i in hs]
    yield
    cm = [c_ref[j, h] for j, h in its]
    n8 = [n_ref[j, h] for j, h in its]
    qc = [_dot(q[i], cm[i]) for i in hs]
    qn = [_dot_nt(q[i], n8[i])[:, 0:1] for i in hs]
    yield
    mt = [jnp.maximum(f_t[i] + m[i], f_t[i] + gmax[i]) for i in hs]
    w_inter = [jnp.exp(f_t[i] + m[i] - mt[i]) for i in hs]
    d = [jnp.exp(jnp.where(incl, (f_t[i] - mt[i]) + (i_s[i] - f_s[i]), -jnp.inf)) for i in hs]
    qk = [qk0[i] * d[i] for i in hs]
    qkv = [_dot(qk[i], v[i]) for i in hs]
    yield
    m_new = [mt[i][c_ - 1:c_, :] for i in hs]
    f_last = [f_t[i][c_ - 1:c_, :] for i in hs]
    w_end = [jnp.exp(f_last[i] + m[i] - m_new[i]) for i in hs]
    kws = [k[i] * jnp.exp(f_last[i] - f_t[i] + i_t[i] - m_new[i]) for i in hs]
    kwv = [_dot_tn(kws[i], v[i]) for i in hs]
    yield
    for i, (j, h) in enumerate(its):
        c_ref[j, h] = w_end[i] * cm[i] + kwv[i]
        n_ref[j, h] = w_end[i] * n8[i] + jnp.broadcast_to(jnp.sum(kws[i], axis=0, keepdims=True),
                                                            (8, ML_DH))
        m_ref[j, h] = jnp.broadcast_to(m_new[i], (8, 128))
    den = [w_inter[i] * qn[i] + jnp.sum(qk[i], axis=1, keepdims=True) for i in hs]
    yield
    hm = [(w_inter[i] * qc[i] + qkv[i]) / jnp.maximum(jnp.abs(den[i]), jnp.exp(-mt[i])) for i in hs]
    mu = [jnp.mean(hm[i], axis=-1, keepdims=True) for i in hs]
    yield
    yc = [hm[i] - mu[i] for i in hs]
    var = [jnp.mean(yc[i] * yc[i], axis=-1, keepdims=True) for i in hs]
    yield
    for i, (j, _) in enumerate(its):
        y = yc[i] * lax.rsqrt(var[i] + EPS) * ng_ref[...]
        o_ref[j, rows, hls[i]] = (y * _sigmoid(mo_ref[j, rows, hls[i]])).astype(BF16)


def _rw_chunk(rows, ns, r_ref, k_ref, v_ref, wa_ref, gd_ref, mu_refs, par_refs, o_ref, s_ref, prev_refs):
    mur_ref, muk_ref, muv_ref, muwa_ref, mug_ref = mu_refs
    w0_ref, wup_ref, a0_ref, aup_ref, gup_ref, kkw_ref, kaw_ref, rk_ref, ng_ref = par_refs
    pr_s, pk_s, pv_s, pwa_s, pg_s = prev_refs
    c_ = CHUNK
    r_, c2 = _tri(c_, 128)
    s_idx = c2 % c_
    strict_pk = s_idx < r_
    incl_pk = s_idx <= r_
    eye_pk = jnp.where(s_idx == r_, 1.0, 0.0)
    rr, cc = _tri(c_)
    tril = jnp.where(rr >= cc, 1.0, 0.0).astype(BF16)
    lo_half = lax.broadcasted_iota(jnp.int32, (c_, 128), 1) < RW_HEAD
    r128, c128 = _tri(128)
    bdmask = (r128 // RW_HEAD) == (c128 // RW_HEAD)
    ones_bd = jnp.where(bdmask, 1.0, 0.0).astype(BF16)
    first_w = lax.broadcasted_iota(jnp.int32, (c_, RW_W), 0) == 0
    first_n = lax.broadcasted_iota(jnp.int32, (c_, 128), 0) == 0

    def mix(x_ref, prev_ref, mu_ref, j):
        x = x_ref[j, rows, :]
        first = first_w if x.shape[1] == RW_W else first_n
        prev = jnp.where(first, prev_ref[j], pltpu.roll(x, 1, 0))
        prev_ref[j] = x[c_ - 1:c_, :]
        return x + (prev - x) * mu_ref[...]

    def bd(y):
        yb = y.astype(BF16)
        zero = jnp.zeros_like(yb)
        return jnp.concatenate([jnp.where(lo_half, yb, zero), jnp.where(lo_half, zero, yb)], axis=0)

    def pk_mul(x, y):
        return jnp.dot(x.astype(BF16), bd(y), preferred_element_type=F32)

    def seg_all(xs):
        st = jnp.dot(jnp.concatenate(xs, axis=0).astype(BF16), ones_bd, preferred_element_type=F32)
        return [st[i * c_:(i + 1) * c_] for i in range(len(xs))]

    zr = [mix(r_ref, pr_s, mur_ref, j) for j in range(ns)]
    zv = [mix(v_ref, pv_s, muv_ref, j) for j in range(ns)]
    tok = [_rw_tokens(zr[j], mix(k_ref, pk_s, muk_ref, j), zv[j],
                      mix(wa_ref, pwa_s, muwa_ref, j), mix(gd_ref, pg_s, mug_ref, j),
                      w0_ref[...], wup_ref[...], a0_ref[...], aup_ref[...],
                      gup_ref[...], kkw_ref[...], kaw_ref[...]) for j in range(ns)]
    yield
    its = [(j, p) for j in range(ns) for p in range(RW_PAIRS)]
    prs = range(len(its))
    lns = [slice(p * 128, (p + 1) * 128) for _, p in its]
    lw = [tok[j][0][:, lns[i]] for i, (j, _) in enumerate(its)]
    a = [tok[j][1][:, lns[i]] for i, (j, _) in enumerate(its)]
    g = [tok[j][2][:, lns[i]] for i, (j, _) in enumerate(its)]
    kkp = [tok[j][3][:, lns[i]] for i, (j, _) in enumerate(its)]
    kp = [tok[j][4][:, lns[i]] for i, (j, _) in enumerate(its)]
    rp = [zr[j][:, lns[i]] for i, (j, _) in enumerate(its)]
    vp = [zv[j][:, lns[i]] for i, (j, _) in enumerate(its)]
    ssq = seg_all([x * x for x in kkp])
    cumf = [_dot2_left(tril, tok[j][0]) for j in range(ns)]
    yield
    cum = [cumf[j][:, lns[i]] for i, (j, _) in enumerate(its)]
    kkn = [kkp[p] * lax.rsqrt(ssq[p] + 1e-12) for p in prs]
    bp = [kkn[p] * a[p] for p in prs]
    cl = [cum[p][c_ - 1:c_, :] for p in prs]
    e_neg = [jnp.exp(-cum[p]) for p in prs]
    e_end = [jnp.exp(cl[p] - cum[p]) for p in prs]
    rh = [rp[p] * jnp.exp(cum[p]) for p in prs]
    kh = [kkn[p] * jnp.exp(cum[p] - lw[p]) for p in prs]
    kt = [kp[p] * e_neg[p] for p in prs]
    bt = [bp[p] * e_neg[p] for p in prs]
    kg = [kp[p] * e_end[p] for p in prs]
    bg = [bp[p] * e_end[p] for p in prs]
    kr = [jnp.concatenate([kh[p], rh[p]], axis=0).astype(BF16) for p in prs]
    ktb = [jnp.concatenate([bd(kt[p]), bd(bt[p])], axis=0) for p in prs]
    pkb = [lax.dot_general(kr[p], ktb[p], _NT, preferred_element_type=F32) for p in prs]
    yield
    ak = [jnp.where(strict_pk, pkb[p][0:c_, 0:128], 0.0) for p in prs]
    gk = [jnp.where(incl_pk, pkb[p][c_:2 * c_, 0:128], 0.0) for p in prs]
    gb = [jnp.where(incl_pk, pkb[p][c_:2 * c_, 128:256], 0.0) for p in prs]
    pw = [jnp.where(strict_pk, -pkb[p][0:c_, 128:256], 0.0) for p in prs]
    tm = [eye_pk + pw[p] for p in prs]
    agv = [pk_mul(jnp.concatenate([ak[p], gk[p]], axis=0), vp[p]) for p in prs]
    pw = [pk_mul(pw[p], pw[p]) for p in prs]
    yield
    for _ in range(4):
        x = [pk_mul(jnp.concatenate([pw[p], tm[p]], axis=0), pw[p]) for p in prs]
        tm = [tm[p] + x[p][c_:2 * c_] for p in prs]
        pw = [x[p][0:c_] for p in prs]
        yield
    tm = [tm[p] + pk_mul(tm[p], pw[p]) for p in prs]
    yield
    ku = [jnp.dot(tm[p].astype(BF16), jnp.concatenate([bd(kh[p]), bd(agv[p][0:c_])], axis=1),
                  preferred_element_type=F32) for p in prs]
    yield
    gku = [jnp.dot(gb[p].astype(BF16),
                   jnp.concatenate([bd(ku[p][:, 0:128]), bd(ku[p][:, 128:256])], axis=1),
                   preferred_element_type=F32) for p in prs]
    d = [_dot_tn(jnp.concatenate([vp[p], -ku[p][:, 128:256]], axis=0),
                 jnp.concatenate([kg[p], bg[p]], axis=0)) for p in prs]
    yield
    rhp = [rh[p] - gku[p][:, 0:128] for p in prs]
    y0 = [agv[p][c_:2 * c_] - gku[p][:, 128:256] for p in prs]
    s = [s_ref[j, p] for j, p in its]
    uy = [_dot_nt(jnp.concatenate([ku[p][:, 0:128], rhp[p]], axis=0), s[p]) for p in prs]
    yield
    ub = [_dot_tn(uy[p][0:c_], bg[p]) for p in prs]
    y = [uy[p][c_:2 * c_] + y0[p] for p in prs]
    mean = seg_all(y)
    yield
    for i, (j, p) in enumerate(its):
        s_ref[j, p] = s[i] * jnp.exp(cl[i]) + jnp.where(bdmask, d[i] - ub[i], 0.0)
    yc = [y[p] - mean[p] * (1.0 / RW_HEAD) for p in prs]
    var = seg_all([x * x for x in yc])
    bon = seg_all([rp[p] * kp[p] * rk_ref[:, lns[p]] for p in prs])
    yield
    for i, (j, _) in enumerate(its):
        yn = yc[i] * lax.rsqrt(var[i] * (1.0 / RW_HEAD) + EPS) * ng_ref[:, lns[i]]
        o_ref[j, rows, lns[i]] = ((yn + bon[i] * vp[i]) * g[i]).astype(BF16)


def _mixers_kernel(*refs, n_chunks, ns):
    (gq, gk, gv, gg, sm, mq, mk, mv, mo, smt, rr, rk, rv, rwa, rgd) = refs[0:15]
    aup, ab, gn, brow, bcol, mng = refs[15:21]
    mu_refs = refs[21:26]
    par_refs = refs[26:35]
    o_gla, o_ml, o_rw, gla_out, c_out, n_out, m_out, rw_out = refs[35:43]
    s_gla, c_s, n_s, m_s, s_rw = refs[43:48]
    prev_refs = refs[48:53]
    t = pl.program_id(1)

    @pl.when(t == 0)
    def _():
        for ref in (s_gla, c_s, n_s, m_s, s_rw) + tuple(prev_refs):
            ref[...] = jnp.zeros_like(ref)

    def chunk(ci, carry):
        rows = pl.ds(pl.multiple_of(ci * CHUNK, CHUNK), CHUNK)
        _interleave(
            {"r": _rw_chunk(rows, ns, rr, rk, rv, rwa, rgd, mu_refs, par_refs, o_rw, s_rw, prev_refs),
             "g": _gla_chunk(rows, ns, gq, gk, gv, gg, sm, aup, ab, gn, o_gla, s_gla),
             "m": _mlstm_chunk(rows, ci, ns, mq, mk, mv, mo, sm, smt, brow, bcol, mng, o_ml, c_s, n_s, m_s)},
            "rmrg")
        return carry

    lax.fori_loop(0, n_chunks, chunk, 0)

    @pl.when(t == pl.num_programs(1) - 1)
    def _():
        gla_out[...] = s_gla[...]
        c_out[...] = c_s[...]
        for j in range(ns):
            for h in range(ML_HEADS):
                n_out[j, h:h + 1, :] = n_s[j, h][0:1, :]
                m_out[j, h:h + 1, :] = m_s[j, h][0:1, :]
            for p in range(RW_PAIRS):
                s = s_rw[j, p]
                rw_out[j, 2 * p] = s[0:RW_HEAD, 0:RW_HEAD]
                rw_out[j, 2 * p + 1] = s[RW_HEAD:, RW_HEAD:]


def _mixers_prompt(z, smt, lp, b_, t_, tt):
    ns = _seq_per_step(b_)
    nc = tt // CHUNK
    z3 = z.reshape(b_, t_, NP)
    zblk = lambda w, off: pl.BlockSpec((ns, tt, w), lambda b, t, c=off // w: (b, t, c))
    const = lambda b, t: (0, 0)
    full = lambda a: pl.BlockSpec(a.shape, const)
    seq = lambda *shape: pl.BlockSpec((ns,) + shape, lambda b, t, n=len(shape): (b,) + (0,) * n)
    params = [lp["aup_pad"], lp["ab"], lp["gla_ng"], lp["brow"], lp["bcol"], lp["ml_ng"], *lp["mus"], *lp["rw"]]
    bf = jax.ShapeDtypeStruct((b_, t_, D_MODEL), BF16)
    outs = pl.pallas_call(
        functools.partial(_mixers_kernel, n_chunks=nc, ns=ns),
        grid=(b_ // ns, t_ // tt),
        in_specs=[zblk(GLA_QK, OFF_GQ), zblk(GLA_QK, OFF_GK), zblk(GLA_V, OFF_GV), zblk(GLA_V, OFF_GG),
                  zblk(128, OFF_SM),
                  zblk(ML_W, OFF_MQ), zblk(ML_W, OFF_MK), zblk(ML_W, OFF_MV), zblk(ML_W, OFF_MO),
                  pl.BlockSpec((ns, nc, 8, CHUNK), lambda b, t: (b, t, 0, 0)),
                  zblk(RW_W, OFF_RW), zblk(RW_W, OFF_RW + RW_W), zblk(RW_W, OFF_RW + 2 * RW_W),
                  zblk(128, OFF_RW + 3 * RW_W), zblk(128, OFF_RW + 3 * RW_W + 128)]
                 + [full(a) for a in params],
        out_specs=[pl.BlockSpec((ns, tt, D_MODEL), lambda b, t: (b, t, 0))] * 3
                  + [seq(GLA_HEADS, GLA_DK, GLA_DV), seq(ML_HEADS, ML_DH, ML_DH), seq(ML_HEADS, ML_DH),
                     seq(ML_HEADS, 128), seq(RW_HEADS, RW_HEAD, RW_HEAD)],
        out_shape=[bf, bf, bf,
                   jax.ShapeDtypeStruct((b_, GLA_HEADS, GLA_DK, GLA_DV), F32),
                   jax.ShapeDtypeStruct((b_, ML_HEADS, ML_DH, ML_DH), F32),
                   jax.ShapeDtypeStruct((b_, ML_HEADS, ML_DH), F32),
                   jax.ShapeDtypeStruct((b_, ML_HEADS, 128), F32),
                   jax.ShapeDtypeStruct((b_, RW_HEADS, RW_HEAD, RW_HEAD), F32)],
        scratch_shapes=[pltpu.VMEM((ns, GLA_HEADS, GLA_DK, GLA_DV), F32),
                        pltpu.VMEM((ns, ML_HEADS, ML_DH, ML_DH), F32),
                        pltpu.VMEM((ns, ML_HEADS, 8, ML_DH), F32),
                        pltpu.VMEM((ns, ML_HEADS, 8, 128), F32),
                        pltpu.VMEM((ns, RW_PAIRS, 128, 128), F32),
                        pltpu.VMEM((ns, 1, RW_W), F32), pltpu.VMEM((ns, 1, RW_W), F32),
                        pltpu.VMEM((ns, 1, RW_W), F32), pltpu.VMEM((ns, 1, 128), F32),
                        pltpu.VMEM((ns, 1, 128), F32)],
        compiler_params=pltpu.CompilerParams(
            dimension_semantics=("parallel", "arbitrary"), vmem_limit_bytes=MIXER_VMEM_LIMIT),
        name="mixers_prompt",
    )(*([z3] * 9), smt, *([z3] * 5), *params)
    o_gla, o_ml, o_rw, gla_new, c_new, n_new, m_new, rw_new = outs
    flat = lambda o: o.reshape(b_ * t_, D_MODEL)
    return flat(o_gla), flat(o_ml), flat(o_rw), (gla_new, c_new, n_new, m_new[:, :, 0], rw_new)


def _sample_prep_kernel(sm_ref, aup_ref, ab_ref, mi_ref, mf_ref, m_ref, ib_ref, fb_ref,
                        r_ref, k_ref, v_ref, wa_ref, gd_ref, sh_r, sh_k, sh_v, sh_wa, sh_g,
                        mur_ref, muk_ref, muv_ref, muwa_ref, mug_ref,
                        w0_ref, wup_ref, a0_ref, aup2_ref, gup_ref, kkw_ref, kaw_ref,
                        gla_la_ref, mnew_ref, wend_ref, ws_ref,
                        rr_ref, rk_ref, rv_ref, rkk_ref, rb_ref, rw_ref, rg_ref):
    gla_la_ref[...] = _logsig(_dot(sm_ref[...], aup_ref[...]) + ab_ref[...]) * (1.0 / GLA_TAU)
    i_pre = mi_ref[...] + ib_ref[...]
    logf = _logsig(mf_ref[...] + fb_ref[...])
    m = m_ref[...]
    m_new = jnp.maximum(logf + m, i_pre)
    mnew_ref[...] = m_new
    wend_ref[...] = jnp.exp(logf + m - m_new)
    ws_ref[...] = jnp.exp(i_pre - m_new)
    mixf = lambda x, s, mu: x[...] + (s[...] - x[...]) * mu[...]
    zr = mixf(r_ref, sh_r, mur_ref)
    zk = mixf(k_ref, sh_k, muk_ref)
    zv = mixf(v_ref, sh_v, muv_ref)
    zwa = mixf(wa_ref, sh_wa, muwa_ref)
    zg = mixf(gd_ref, sh_g, mug_ref)
    lw, a, g, kk0, kmod = _rw_tokens(zr, zk, zv, zwa, zg, w0_ref[...], wup_ref[...], a0_ref[...],
                                     aup2_ref[...], gup_ref[...], kkw_ref[...], kaw_ref[...])
    ones_bd = _ones_bd()
    for p in range(RW_PAIRS):
        ln = slice(p * 128, (p + 1) * 128)
        kkp = kk0[:, ln]
        kkn = kkp * lax.rsqrt(_seg(kkp * kkp, ones_bd) + 1e-12)
        rkk_ref[:, ln] = kkn
        rb_ref[:, ln] = kkn * a[:, ln]
    rr_ref[...] = zr
    rk_ref[...] = kmod
    rv_ref[...] = zv
    rw_ref[...] = jnp.exp(lw)
    rg_ref[...] = g


STEP_BB = 8


def _row_select(rows_list, rid):
    acc = rows_list[0]
    for b in range(1, len(rows_list)):
        acc = jnp.where(rid == b, rows_list[b], acc)
    return acc


def _gla_step_kernel(s_ref, q_ref, k_ref, la_ref, v_ref, *rest):
    sn_ref, o_ref = rest[-2:]
    rid_k = lax.broadcasted_iota(jnp.int32, (STEP_BB, GLA_DK), 0)
    rid_v = lax.broadcasted_iota(jnp.int32, (STEP_BB, GLA_DV), 0)
    bs = range(STEP_BB)
    onehot = [jnp.where(rid_v == b, 1.0, 0.0).astype(BF16) for b in bs]
    for h in range(GLA_HEADS):
        kl = slice(h * GLA_DK, (h + 1) * GLA_DK)
        vl = slice(h * GLA_DV, (h + 1) * GLA_DV)
        qt = q_ref[:, kl] * (GLA_DK ** -0.5)
        kt = k_ref[:, kl]
        lt = la_ref[:, kl]
        vt = v_ref[:, vl]
        dec = [jnp.exp(_dot2_tn(lt, onehot[b])) for b in bs]
        kv = [_dot3(jnp.where(rid_k == b, kt, 0.0), vt, _TN) for b in bs]
        sn = [dec[b] * s_ref[0, b, h] + kv[b] for b in bs]
        for b in bs:
            sn_ref[0, b, h] = sn[b]
        o_ref[:, vl] = _row_select([_dot3(qt, sn[b]) for b in bs], rid_v)


def _mlstm_step_kernel(c_ref, n_ref, q_ref, k_ref, v_ref, wend_ref, ws_ref, mnew_ref, *rest):
    cn_ref, nn_ref, h_ref = rest[-3:]
    rid = lax.broadcasted_iota(jnp.int32, (STEP_BB, ML_DH), 0)
    for h in range(ML_HEADS):
        hl = slice(h * ML_DH, (h + 1) * ML_DH)
        qt = q_ref[:, hl] * (ML_DH ** -0.5)
        kt = k_ref[:, hl]
        vt = v_ref[:, hl]
        wend = wend_ref[:, h:h + 1]
        ws = ws_ref[:, h:h + 1]
        nn = wend * n_ref[0, :, hl] + ws * kt
        nn_ref[0, :, hl] = nn
        den = jnp.sum(qt * nn, axis=1, keepdims=True)
        kws = kt * ws
        nums = []
        for half in range(2):
            bs = range(half * 4, half * 4 + 4)
            kwv = [_dot3(jnp.where(rid == b, kws, 0.0), vt, _TN) for b in bs]
            cn = [wend_ref[b:b + 1, h:h + 1] * c_ref[0, b, h] + kwv[i] for i, b in enumerate(bs)]
            for i, b in enumerate(bs):
                cn_ref[0, b, h] = cn[i]
            nums += [_dot3(qt, cn[i]) for i in range(4)]
        num = _row_select(nums, rid)
        h_ref[:, hl] = num / jnp.maximum(jnp.abs(den), jnp.exp(-mnew_ref[:, h:h + 1]))


def _rw_step_kernel(s_ref, kk_ref, b_ref, k_ref, r_ref, w_ref, v_ref, *rest):
    sn_ref, y_ref = rest[-2:]
    n = RW_HEAD
    rid = lax.broadcasted_iota(jnp.int32, (STEP_BB, n), 0)
    bs = range(STEP_BB)
    onehot = [jnp.where(rid == b, 1.0, 0.0).astype(BF16) for b in bs]
    for h in range(RW_HEADS):
        hl = slice(h * n, (h + 1) * n)
        kkt, bt, kt, rt, wt = kk_ref[:, hl], b_ref[:, hl], k_ref[:, hl], r_ref[:, hl], w_ref[:, hl]
        vhi, vlo = _split(v_ref[:, hl])
        tn = lambda x, y: lax.dot_general(x, y, _TN, preferred_element_type=F32)
        vcol = [tn(vhi, onehot[b]) + tn(vlo, onehot[b]) for b in bs]
        s = [s_ref[0, b, h] for b in bs]
        sa = [_dot3(s[b], jnp.broadcast_to(-kkt[b:b + 1, :], (n, n)), _NT) for b in bs]
        sn = [s[b] * wt[b:b + 1, :] + sa[b] * bt[b:b + 1, :] + vcol[b] * kt[b:b + 1, :] for b in bs]
        for b in bs:
            sn_ref[0, b, h] = sn[b]
        yr = [_dot3(jnp.broadcast_to(rt[b:b + 1, :], (STEP_BB, n)), sn[b], _NT) for b in bs]
        y_ref[:, hl] = _row_select(yr, rid)


def _sample_post_kernel(og_ref, gg_ref, gn_ref, hm_ref, mo_ref, mn_ref,
                        y_ref, r_ref, k_ref, v_ref, g_ref, rk_ref, ng_ref,
                        ogla_ref, oml_ref, orw_ref):
    for h in range(GLA_HEADS):
        vl = slice(h * GLA_DV, (h + 1) * GLA_DV)
        o = og_ref[:, vl]
        on = o * lax.rsqrt(jnp.mean(o * o, axis=-1, keepdims=True) + EPS) * gn_ref[...]
        gg = gg_ref[:, vl]
        ogla_ref[:, vl] = (on * (gg * _sigmoid(gg))).astype(BF16)
    for h in range(ML_HEADS):
        hl = slice(h * ML_DH, (h + 1) * ML_DH)
        hm = hm_ref[:, hl]
        y = hm - jnp.mean(hm, axis=-1, keepdims=True)
        y = y * lax.rsqrt(jnp.mean(y * y, axis=-1, keepdims=True) + EPS) * mn_ref[...]
        oml_ref[:, hl] = (y * _sigmoid(mo_ref[:, hl])).astype(BF16)
    ones_bd = _ones_bd()
    for p in range(RW_PAIRS):
        ln = slice(p * 128, (p + 1) * 128)
        orw_ref[:, ln] = _rw_epilogue(y_ref[:, ln], r_ref[:, ln], k_ref[:, ln], v_ref[:, ln],
                                      g_ref[:, ln], rk_ref[:, ln], ng_ref[:, ln], ones_bd)


def _full_call(kernel_fn, args, out_shapes, name):
    nd = lambda a: (lambda: (0,) * len(a.shape))
    return pl.pallas_call(
        kernel_fn,
        in_specs=[pl.BlockSpec(a.shape, nd(a)) for a in args],
        out_specs=[pl.BlockSpec(s.shape, nd(s)) for s in out_shapes],
        out_shape=out_shapes,
        compiler_params=pltpu.CompilerParams(vmem_limit_bytes=VMEM_LIMIT),
        name=name,
    )(*args)


def _state_call(kernel_fn, l, states, vecs, prev_outs, out_width, name):
    nb = states[0].shape[1]
    st_spec = lambda a: pl.BlockSpec((1, STEP_BB) + tuple(a.shape[2:]),
                                     lambda i, n=a.ndim: (l, i) + (0,) * (n - 2))
    in_specs = [st_spec(a) for a in states]
    in_specs += [pl.BlockSpec((STEP_BB, w), lambda i, c=c: (i, c)) for _, w, c in vecs]
    args = list(states) + [a for a, _, _ in vecs]
    aliases = {}
    for j, po in enumerate(prev_outs or ()):
        aliases[len(args)] = j
        in_specs.append(pl.BlockSpec(memory_space=pl.ANY))
        args.append(po)
    out_shape = [jax.ShapeDtypeStruct(a.shape, F32) for a in states]
    out_shape.append(jax.ShapeDtypeStruct((nb, out_width), F32))
    out_specs = [st_spec(a) for a in states] + [pl.BlockSpec((STEP_BB, out_width), lambda i: (i, 0))]
    return pl.pallas_call(
        kernel_fn,
        grid=(nb // STEP_BB,),
        in_specs=in_specs,
        out_specs=out_specs,
        out_shape=out_shape,
        input_output_aliases=aliases,
        compiler_params=pltpu.CompilerParams(
            dimension_semantics=("parallel",), vmem_limit_bytes=VMEM_LIMIT),
        name=name,
    )(*args)


def _prep_layer_params(l, p):
    w = p["w_in"][l]
    cols = ((0, 2048), (2064, 3088), (3088, 6160), (6168, 7192), (7192, 13592), (2048, 2064), (6160, 6168))
    w_in_p = jnp.concatenate([w[:, a:b].astype(BF16) for a, b in cols]
                             + [jnp.zeros((D_MODEL, NP - 13592), BF16)], axis=1)
    aup_pad = jnp.zeros((128, GLA_QK), F32).at[0:GLA_RANK].set(p["gla_a_up"][l])
    brow = (jnp.zeros((1, 128), F32).at[0, SM_I:SM_I + 4].set(p["ml_i_bias"][l])
            .at[0, SM_F:SM_F + 4].set(p["ml_f_bias"][l]))
    bcol = jnp.broadcast_to(jnp.concatenate([p["ml_i_bias"][l], p["ml_f_bias"][l]])[:, None], (8, CHUNK))
    mu = p["rw_mu"][l][None, :]
    mus = (mu[:, 0:1024], mu[:, 1024:2048], mu[:, 2048:3072], mu[:, 3072:3200], mu[:, 3200:3328])
    wup_pad = jnp.zeros((128, RW_W), F32).at[0:64].set(p["rw_w_up"][l])
    aup2_pad = jnp.zeros((128, RW_W), F32).at[64:128].set(p["rw_a_up"][l])
    row = lambda a: a.reshape(1, -1)
    rw_params = (row(p["rw_w0"][l]), wup_pad, row(p["rw_a0"][l]), aup2_pad, p["rw_g_up"][l],
                 row(p["rw_k_k"][l]), row(p["rw_k_a"][l]), row(p["rw_r_k"][l]), row(p["rw_norm_g"][l]))
    return dict(
        norm1_g=row(p["norm1_g"][l]), w_in=w_in_p, aup_pad=aup_pad, ab=row(p["gla_a_bias"][l]),
        gla_ng=row(p["gla_norm_g"][l]), brow=brow, bcol=bcol, ib=row(p["ml_i_bias"][l]),
        fb=row(p["ml_f_bias"][l]), ml_ng=row(p["ml_norm_g"][l]), mus=mus, rw=rw_params,
        wb=p["w_branch"][l].astype(BF16), wo=p["w_out"][l].astype(BF16), norm2_g=row(p["norm2_g"][l]),
        w1=p["w_ff1"][l].astype(BF16), w2=p["w_ff2"][l].astype(BF16))


def _prompt_layer(x, lp, fg, final, b_, t_):
    rows = x.shape[0]
    tm = min(512, rows)
    tm_big = min(1024, rows)
    z = _proj_in(x, lp["norm1_g"], lp["w_in"], tm_big)
    sm = z[:, OFF_SM + SM_I:OFF_SM + SM_I + 8]
    smt = sm.reshape(b_, t_ // CHUNK, CHUNK, 8).transpose(0, 1, 3, 2)
    o_gla, o_ml, o_rw, new_states = _mixers_prompt(z, smt, lp, b_, t_, min(MIXER_TT, t_))
    x = _merge(o_gla, o_ml, o_rw, z, x, lp["wb"], lp["wo"], tm)
    x = _ffn(x, lp["norm2_g"], lp["w1"], lp["w2"], fg, final, tm_big)
    shift = z.reshape(b_, t_, NP)[:, t_ - 1, OFF_RW:OFF_RW + RW_COLS]
    return x, new_states + (shift,)


def _sample_layer(l, x, states, prev, lp, fg, final):
    st_gla, st_c, st_n, st_m, st_rw, st_shift = states
    nb = x.shape[0]
    z = _proj_in(x, lp["norm1_g"], lp["w_in"], nb)
    zc = lambda off, w: z[:, off:off + w]
    shift = st_shift[l]
    sh = (shift[:, 0:1024], shift[:, 1024:2048], shift[:, 2048:3072], shift[:, 3072:3200], shift[:, 3200:3328])
    w0, wup, a0, aup2, gup, kkw, kaw, rk, ng = lp["rw"]
    sds = lambda *s: jax.ShapeDtypeStruct(s, F32)
    prep_args = [zc(OFF_SM, 128), lp["aup_pad"], lp["ab"],
                 zc(OFF_SM + SM_I, 4), zc(OFF_SM + SM_F, 4), st_m[l], lp["ib"], lp["fb"],
                 zc(OFF_RW, 1024), zc(OFF_RW + 1024, 1024), zc(OFF_RW + 2048, 1024),
                 zc(OFF_RW + 3072, 128), zc(OFF_RW + 3200, 128), *sh, *lp["mus"],
                 w0, wup, a0, aup2, gup, kkw, kaw]
    (la, m_new, wend, ws, rr, rkm, rv, rkk, rb, rw, rg) = _full_call(
        _sample_prep_kernel, prep_args,
        [sds(nb, GLA_QK), sds(nb, 4), sds(nb, 4), sds(nb, 4)] + [sds(nb, RW_W)] * 7, "sample_prep")

    pv = lambda *idx: None if prev is None else [prev[i] for i in idx]
    gla_all, o_g = _state_call(
        _gla_step_kernel, l, [st_gla],
        [(z, GLA_QK, OFF_GQ // GLA_QK), (z, GLA_QK, OFF_GK // GLA_QK), (la, GLA_QK, 0),
         (z, GLA_V, OFF_GV // GLA_V)], pv(0), GLA_V, "gla_step")
    c_all, n_all, hm = _state_call(
        _mlstm_step_kernel, l, [st_c, st_n],
        [(z, ML_W, OFF_MQ // ML_W), (z, ML_W, OFF_MK // ML_W), (z, ML_W, OFF_MV // ML_W),
         (wend, 4, 0), (ws, 4, 0), (m_new, 4, 0)], pv(1, 2), ML_W, "mlstm_step")
    rw_all, y = _state_call(
        _rw_step_kernel, l, [st_rw],
        [(a, RW_W, 0) for a in (rkk, rb, rkm, rr, rw, rv)], pv(3), RW_W, "rwkv_step")

    bf = lambda: jax.ShapeDtypeStruct((nb, D_MODEL), BF16)
    o_gla, o_ml, o_rw = _full_call(
        _sample_post_kernel,
        [o_g, zc(OFF_GG, GLA_V), lp["gla_ng"], hm, zc(OFF_MO, ML_W), lp["ml_ng"],
         y, rr, rkm, rv, rg, rk, ng],
        [bf(), bf(), bf()], "sample_post")
    x = _merge(o_gla, o_ml, o_rw, z, x, lp["wb"], lp["wo"], nb)
    x = _ffn(x, lp["norm2_g"], lp["w1"], lp["w2"], fg, final, nb)
    return x, (gla_all, c_all, n_all, rw_all), (m_new, zc(OFF_RW, RW_COLS))


def kernel(x_prompt, x_sample, state_gla, state_mlstm_c, state_mlstm_n, state_mlstm_m, state_rwkv, state_shift, norm1_g, w_in, gla_a_up, gla_a_bias, gla_norm_g, ml_i_bias, ml_f_bias, ml_norm_g, rw_mu, rw_w0, rw_w_up, rw_a0, rw_a_up, rw_g_up, rw_k_k, rw_k_a, rw_r_k, rw_norm_g, w_branch, w_out, norm2_g, w_ff1, w_ff2, final_g):
    p = dict(norm1_g=norm1_g, w_in=w_in, gla_a_up=gla_a_up, gla_a_bias=gla_a_bias, gla_norm_g=gla_norm_g,
             ml_i_bias=ml_i_bias, ml_f_bias=ml_f_bias, ml_norm_g=ml_norm_g, rw_mu=rw_mu, rw_w0=rw_w0,
             rw_w_up=rw_w_up, rw_a0=rw_a0, rw_a_up=rw_a_up, rw_g_up=rw_g_up, rw_k_k=rw_k_k, rw_k_a=rw_k_a,
             rw_r_k=rw_r_k, rw_norm_g=rw_norm_g, w_branch=w_branch, w_out=w_out, norm2_g=norm2_g,
             w_ff1=w_ff1, w_ff2=w_ff2)
    depth = w_in.shape[0]
    b_, t_, _ = x_prompt.shape
    nb = x_sample.shape[0]
    fg = final_g.reshape(1, D_MODEL)
    xp = x_prompt.reshape(b_ * t_, D_MODEL)
    xs = x_sample.reshape(nb, D_MODEL)
    s_in = (state_gla, state_mlstm_c, state_mlstm_n.reshape(depth, nb, ML_W), state_mlstm_m, state_rwkv,
            state_shift)
    p_states, s_big, s_small = [], None, []
    for l in range(depth):
        lp = _prep_layer_params(l, p)
        final = l == depth - 1
        xp, stp = _prompt_layer(xp, lp, fg, final, b_, t_)
        xs, s_big, small = _sample_layer(l, xs, s_in, s_big, lp, fg, final)
        p_states.append(stp)
        s_small.append(small)
    ps = [jnp.stack([s[i] for s in p_states]) for i in range(6)]
    gla_s, mlc_s, mln_s, rw_s = s_big
    mln_s = mln_s.reshape(depth, nb, ML_HEADS, ML_DH)
    mlm_s = jnp.stack([s[0] for s in s_small])
    shift_s = jnp.stack([s[1] for s in s_small])
    return (xp.reshape(b_, t_, D_MODEL), xs.reshape(nb, 1, D_MODEL),
            ps[0], gla_s, ps[1], mlc_s, ps[2], mln_s, ps[3], mlm_s, ps[4], rw_s, ps[5], shift_s)
```

```python
import functools

import jax
import jax.numpy as jnp
from jax import lax
from jax.experimental import pallas as pl
from jax.experimental.pallas import tpu as pltpu

F32 = jnp.float32
BF16 = jnp.bfloat16

D_MODEL = 1024
GLA_HEADS, GLA_DK, GLA_DV = 4, 128, 256
GLA_QK, GLA_V, GLA_RANK, GLA_TAU = 512, 1024, 16, 16.0
ML_HEADS, ML_DH, ML_W = 4, 256, 1024
RW_HEAD, RW_W, RW_HEADS = 64, 1024, 16
RW_PAIRS = RW_HEADS // 2
RW_COLS = 3 * RW_W + 64 + 64 + 128
D_FF = 4 * D_MODEL
EPS = 1e-6
CHUNK = 64

OFF_GQ, OFF_GK, OFF_GV, OFF_GG = 0, 512, 1024, 2048
OFF_MQ, OFF_MK, OFF_MV, OFF_MO = 3072, 4096, 5120, 6144
OFF_GATE = 7168
OFF_RW = 10240
OFF_SM = 13568
NP = 13824
SM_I, SM_F = 16, 20

VMEM_LIMIT = 48 * 1024 * 1024
MIXER_VMEM_LIMIT = 56 * 1024 * 1024
MIXER_TT = 128


def _dot(a, b):
    return jnp.dot(a.astype(BF16), b.astype(BF16), preferred_element_type=F32)


def _dot_nt(a, b):
    return lax.dot_general(a.astype(BF16), b.astype(BF16), (((1,), (1,)), ((), ())),
                           preferred_element_type=F32)


def _dot_tn(a, b):
    return lax.dot_general(a.astype(BF16), b.astype(BF16), (((0,), (0,)), ((), ())),
                           preferred_element_type=F32)


def _split(x):
    hi = x.astype(BF16)
    lo = (x - hi.astype(F32)).astype(BF16)
    return hi, lo


def _dot2(x, w_exact):
    hi, lo = _split(x)
    return (jnp.dot(hi, w_exact, preferred_element_type=F32)
            + jnp.dot(lo, w_exact, preferred_element_type=F32))


def _dot2_tn(x, w_exact):
    hi, lo = _split(x)
    dn = (((0,), (0,)), ((), ()))
    return (lax.dot_general(hi, w_exact, dn, preferred_element_type=F32)
            + lax.dot_general(lo, w_exact, dn, preferred_element_type=F32))


def _dot2_left(w_exact, x):
    hi, lo = _split(x)
    return (jnp.dot(w_exact, hi, preferred_element_type=F32)
            + jnp.dot(w_exact, lo, preferred_element_type=F32))


_NN = (((1,), (0,)), ((), ()))
_TN = (((0,), (0,)), ((), ()))
_NT = (((1,), (1,)), ((), ()))


def _dot3(a, b, dn=_NN):
    ah, al = _split(a)
    bh, bl = _split(b)
    f = lambda x, y: lax.dot_general(x, y, dn, preferred_element_type=F32)
    return f(ah, bh) + f(al, bh) + f(ah, bl)


def _logsig(x):
    return jnp.minimum(x, 0.0) - jnp.log1p(jnp.exp(-jnp.abs(x)))


def _sigmoid(x):
    return jax.nn.sigmoid(x)


def _rms(x, g):
    return x * lax.rsqrt(jnp.mean(x * x, axis=-1, keepdims=True) + EPS) * g


def _tri(n, m=None):
    m = n if m is None else m
    r = lax.broadcasted_iota(jnp.int32, (n, m), 0)
    c = lax.broadcasted_iota(jnp.int32, (n, m), 1)
    return r, c


def _proj_in_kernel(x_ref, g_ref, w_ref, o_ref, h_ref):
    @pl.when(pl.program_id(1) == 0)
    def _():
        h_ref[...] = _rms(x_ref[...], g_ref[...]).astype(BF16)

    o_ref[...] = jnp.dot(h_ref[...], w_ref[...], preferred_element_type=F32)


def _proj_in(x, g, w, tm, tn=2304):
    m = x.shape[0]
    return pl.pallas_call(
        _proj_in_kernel,
        grid=(m // tm, NP // tn),
        in_specs=[pl.BlockSpec((tm, D_MODEL), lambda i, j: (i, 0)),
                  pl.BlockSpec((1, D_MODEL), lambda i, j: (0, 0)),
                  pl.BlockSpec((D_MODEL, tn), lambda i, j: (0, j))],
        out_specs=pl.BlockSpec((tm, tn), lambda i, j: (i, j)),
        out_shape=jax.ShapeDtypeStruct((m, NP), F32),
        scratch_shapes=[pltpu.VMEM((tm, D_MODEL), BF16)],
        compiler_params=pltpu.CompilerParams(
            dimension_semantics=("parallel", "arbitrary"), vmem_limit_bytes=VMEM_LIMIT),
        name="proj_in",
    )(x, g, w)


def _merge_kernel(og_ref, om_ref, or_ref, g0_ref, g1_ref, g2_ref, x_ref, wb_ref, wo_ref, o_ref):
    acc = jnp.dot(og_ref[...], wb_ref[0], preferred_element_type=F32) * _sigmoid(g0_ref[...])
    acc += jnp.dot(om_ref[...], wb_ref[1], preferred_element_type=F32) * _sigmoid(g1_ref[...])
    acc += jnp.dot(or_ref[...], wb_ref[2], preferred_element_type=F32) * _sigmoid(g2_ref[...])
    o_ref[...] = x_ref[...] + jnp.dot(acc.astype(BF16), wo_ref[...], preferred_element_type=F32)


def _merge(o_gla, o_ml, o_rw, z, x, wb, wo, tm):
    m = x.shape[0]
    row = lambda i: (i, 0)
    gate_blk = OFF_GATE // D_MODEL
    return pl.pallas_call(
        _merge_kernel,
        grid=(m // tm,),
        in_specs=[pl.BlockSpec((tm, D_MODEL), row),
                  pl.BlockSpec((tm, D_MODEL), row),
                  pl.BlockSpec((tm, D_MODEL), row),
                  pl.BlockSpec((tm, D_MODEL), lambda i: (i, gate_blk)),
                  pl.BlockSpec((tm, D_MODEL), lambda i: (i, gate_blk + 1)),
                  pl.BlockSpec((tm, D_MODEL), lambda i: (i, gate_blk + 2)),
                  pl.BlockSpec((tm, D_MODEL), row),
                  pl.BlockSpec((3, D_MODEL, D_MODEL), lambda i: (0, 0, 0)),
                  pl.BlockSpec((D_MODEL, D_MODEL), lambda i: (0, 0))],
        out_specs=pl.BlockSpec((tm, D_MODEL), row),
        out_shape=jax.ShapeDtypeStruct((m, D_MODEL), F32),
        compiler_params=pltpu.CompilerParams(
            dimension_semantics=("parallel",), vmem_limit_bytes=VMEM_LIMIT),
        name="merge",
    )(o_gla, o_ml, o_rw, z, z, z, x, wb, wo)


def _ffn_kernel(x_ref, g_ref, w1_ref, w2_ref, fg_ref, o_ref, h_ref, acc_ref, *, final):
    f = pl.program_id(1)

    @pl.when(f == 0)
    def _():
        h_ref[...] = _rms(x_ref[...], g_ref[...]).astype(BF16)
        acc_ref[...] = jnp.zeros_like(acc_ref)

    a = jnp.dot(h_ref[...], w1_ref[...], preferred_element_type=F32)
    a = jnp.square(jnp.maximum(a, 0.0))
    acc_ref[...] += jnp.dot(a.astype(BF16), w2_ref[...], preferred_element_type=F32)

    @pl.when(f == pl.num_programs(1) - 1)
    def _():
        y = x_ref[...] + acc_ref[...]
        if final:
            y = _rms(y, fg_ref[...])
        o_ref[...] = y


def _ffn(x, g, w1, w2, fg, final, tm, tf=1024):
    m = x.shape[0]
    return pl.pallas_call(
        functools.partial(_ffn_kernel, final=final),
        grid=(m // tm, D_FF // tf),
        in_specs=[pl.BlockSpec((tm, D_MODEL), lambda i, f: (i, 0)),
                  pl.BlockSpec((1, D_MODEL), lambda i, f: (0, 0)),
                  pl.BlockSpec((D_MODEL, tf), lambda i, f: (0, f)),
                  pl.BlockSpec((tf, D_MODEL), lambda i, f: (f, 0)),
                  pl.BlockSpec((1, D_MODEL), lambda i, f: (0, 0))],
        out_specs=pl.BlockSpec((tm, D_MODEL), lambda i, f: (i, 0)),
        out_shape=jax.ShapeDtypeStruct((m, D_MODEL), F32),
        scratch_shapes=[pltpu.VMEM((tm, D_MODEL), BF16), pltpu.VMEM((tm, D_MODEL), F32)],
        compiler_params=pltpu.CompilerParams(
            dimension_semantics=("parallel", "arbitrary"), vmem_limit_bytes=VMEM_LIMIT),
        name="ffn",
    )(x, g, w1, w2, fg)


def _rw_tokens(zr, zk, zv, zwa, zg, w0, wup, a0, aup, gup, kkw, kaw):
    u = w0 + _dot(jnp.tanh(zwa), wup)
    lw = -jnp.exp(_logsig(u) - 0.5)
    a = _sigmoid(a0 + _dot(zwa, aup))
    g = _dot(_sigmoid(zg), gup)
    kk0 = zk * kkw
    kmod = zk * (1.0 + (a - 1.0) * kaw)
    return lw, a, g, kk0, kmod


def _ones_bd():
    r, c = _tri(128)
    return jnp.where((r // RW_HEAD) == (c // RW_HEAD), 1.0, 0.0).astype(BF16)


def _seg(x, ones_bd):
    return _dot2(x, ones_bd)


def _rw_epilogue(y, r, kmod, v, g, rk, ng, ones_bd):
    mean = _seg(y, ones_bd) * (1.0 / RW_HEAD)
    yc = y - mean
    var = _seg(yc * yc, ones_bd) * (1.0 / RW_HEAD)
    yn = yc * lax.rsqrt(var + EPS) * ng
    bonus = _seg(r * kmod * rk, ones_bd) * v
    return ((yn + bonus) * g).astype(BF16)


def _seq_per_step(b_):
    return 2 if b_ % 2 == 0 else 1


def _interleave(gens, pattern):
    live = dict(gens)
    while live:
        for key in pattern:
            g = live.get(key)
            if g is None:
                continue
            try:
                next(g)
            except StopIteration:
                del live[key]


def _gla_chunk(rows, ns, q_ref, k_ref, v_ref, gg_ref, sm_ref, aup_ref, ab_ref, gn_ref, o_ref, s_ref):
    c_ = CHUNK
    r, c = _tri(c_)
    incl = r >= c
    tril = jnp.where(incl, 1.0, 0.0).astype(BF16)
    ones_c = jnp.ones((c_, GLA_DK), BF16)
    tokw = []
    for j in range(ns):
        x = _dot(sm_ref[j, rows, :], aup_ref[...]) + ab_ref[...]
        la = _logsig(x) * (1.0 / GLA_TAU)
        b = _dot2_left(tril, la)
        dec = jnp.exp(_dot2_tn(la, ones_c))
        q = q_ref[j, rows, :] * (GLA_DK ** -0.5)
        k = k_ref[j, rows, :]
        tokw.append((q * jnp.exp(b), k * jnp.exp(-b), k * jnp.exp(b[c_ - 1:c_, :] - b), dec))
    yield
    its = [(j, h) for j in range(ns) for h in range(GLA_HEADS)]
    hs = range(len(its))
    kls = [slice(h * GLA_DK, (h + 1) * GLA_DK) for _, h in its]
    vls = [slice(h * GLA_DV, (h + 1) * GLA_DV) for _, h in its]
    qe = [tokw[j][0][:, kls[i]] for i, (j, _) in enumerate(its)]
    ke = [tokw[j][1][:, kls[i]] for i, (j, _) in enumerate(its)]
    kd = [tokw[j][2][:, kls[i]] for i, (j, _) in enumerate(its)]
    v = [v_ref[j, rows, vls[i]] for i, (j, _) in enumerate(its)]
    att = [jnp.where(incl, _dot_nt(qe[i], ke[i]), 0.0) for i in hs]
    yield
    s = [s_ref[j, h] for j, h in its]
    oi = [_dot(qe[i], s[i]) for i in hs]
    yield
    kv = [_dot_tn(kd[i], v[i]) for i in hs]
    yield
    o = [oi[i] + _dot(att[i], v[i]) for i in hs]
    for i, (j, h) in enumerate(its):
        dh = tokw[j][3][kls[i], :]
        s_ref[j, h] = s[i] * jnp.concatenate([dh, dh], axis=1) + kv[i]
    yield
    msq = [jnp.mean(o[i] * o[i], axis=-1, keepdims=True) for i in hs]
    yield
    for i, (j, _) in enumerate(its):
        on = o[i] * lax.rsqrt(msq[i] + EPS) * gn_ref[...]
        gg = gg_ref[j, rows, vls[i]]
        o_ref[j, rows, vls[i]] = (on * (gg * _sigmoid(gg))).astype(BF16)


def _mlstm_chunk(rows, ci, ns, q_ref, k_ref, v_ref, mo_ref, sm_ref, smt_ref, brow_ref, bcol_ref, ng_ref,
                 o_ref, c_ref, n_ref, m_ref):
    c_ = CHUNK
    r, c = _tri(c_)
    incl = r >= c
    tril = jnp.where(incl, 1.0, 0.0).astype(BF16)
    triu = jnp.where(r <= c, 1.0, 0.0).astype(BF16)
    sub8 = lax.broadcasted_iota(jnp.int32, (8, c_), 0)
    sm, fcol, smt, frow = [], [], [], []
    for j in range(ns):
        sm.append(sm_ref[j, rows, :] + brow_ref[...])
        fcol.append(_dot2_left(tril, _logsig(sm[j])))
        smt.append(smt_ref[j, ci] + bcol_ref[...])
        frow.append(_dot2(jnp.where(sub8 >= 4, _logsig(smt[j]), 0.0), triu))
    yield
    its = [(j, h) for j in range(ns) for h in range(ML_HEADS)]
    hs = range(len(its))
    hls = [slice(h * ML_DH, (h + 1) * ML_DH) for _, h in its]
    f_t = [fcol[j][:, SM_F + h:SM_F + h + 1] for j, h in its]
    i_t = [sm[j][:, SM_I + h:SM_I + h + 1] for j, h in its]
    f_s = [frow[j][4 + h:5 + h, :] for j, h in its]
    i_s = [smt[j][h:h + 1, :] for j, h in its]
    m = [m_ref[j, h][0:1, 0:1] for j, h in its]
    q = [q_ref[j, rows, hls[i]] * (ML_DH ** -0.5) for i, (j, _) in enumerate(its)]
    k = [k_ref[j, rows, hls[i]] for i, (j, _) in enumerate(its)]
    v = [v_ref[j, rows, hls[i]] for i, (j, _) in enumerate(its)]
    qk0 = [_dot_nt(q[i], k[i]) for i in hs]
    gmax = [jnp.max(jnp.where(incl, i_s[i] - f_s[i], -jnp.inf), axis=1, keepdims=True) for i in hs]
    yield
    cm = [c_ref[j, h] for j, h in its]
    n8 = [n_ref[j, h] for j, h in its]
    qc = [_dot(q[i], cm[i]) for i in hs]
    qn = [_dot_nt(q[i], n8[i])[:, 0:1] for i in hs]
    yield
    mt = [jnp.maximum(f_t[i] + m[i], f_t[i] + gmax[i]) for i in hs]
    w_inter = [jnp.exp(f_t[i] + m[i] - mt[i]) for i in hs]
    d = [jnp.exp(jnp.where(incl, (f_t[i] - mt[i]) + (i_s[i] - f_s[i]), -jnp.inf)) for i in hs]
    qk = [qk0[i] * d[i] for i in hs]
    qkv = [_dot(qk[i], v[i]) for i in hs]
    yield
    m_new = [mt[i][c_ - 1:c_, :] for i in hs]
    f_last = [f_t[i][c_ - 1:c_, :] for i in hs]
    w_end = [jnp.exp(f_last[i] + m[i] - m_new[i]) for i in hs]
    kws = [k[i] * jnp.exp(f_last[i] - f_t[i] + i_t[i] - m_new[i]) for i in hs]
    kwv = [_dot_tn(kws[i], v[i]) for i in hs]
    yield
    for i, (j, h) in enumerate(its):
        c_ref[j, h] = w_end[i] * cm[i] + kwv[i]
        n_ref[j, h] = w_end[i] * n8[i] + jnp.broadcast_to(jnp.sum(kws[i], axis=0, keepdims=True),
                                                            (8, ML_DH))
        m_ref[j, h] = jnp.broadcast_to(m_new[i], (8, 128))
    den = [w_inter[i] * qn[i] + jnp.sum(qk[i], axis=1, keepdims=True) for i in hs]
    yield
    hm = [(w_inter[i] * qc[i] + qkv[i]) / jnp.maximum(jnp.abs(den[i]), jnp.exp(-mt[i])) for i in hs]
    mu = [jnp.mean(hm[i], axis=-1, keepdims=True) for i in hs]
    yield
    yc = [hm[i] - mu[i] for i in hs]
    var = [jnp.mean(yc[i] * yc[i], axis=-1, keepdims=True) for i in hs]
    yield
    for i, (j, _) in enumerate(its):
        y = yc[i] * lax.rsqrt(var[i] + EPS) * ng_ref[...]
        o_ref[j, rows, hls[i]] = (y * _sigmoid(mo_ref[j, rows, hls[i]])).astype(BF16)


def _rw_chunk(rows, ns, r_ref, k_ref, v_ref, wa_ref, gd_ref, mu_refs, par_refs, o_ref, s_ref, prev_refs):
    mur_ref, muk_ref, muv_ref, muwa_ref, mug_ref = mu_refs
    w0_ref, wup_ref, a0_ref, aup_ref, gup_ref, kkw_ref, kaw_ref, rk_ref, ng_ref = par_refs
    pr_s, pk_s, pv_s, pwa_s, pg_s = prev_refs
    c_ = CHUNK
    r_, c2 = _tri(c_, 128)
    s_idx = c2 % c_
    strict_pk = s_idx < r_
    incl_pk = s_idx <= r_
    eye_pk = jnp.where(s_idx == r_, 1.0, 0.0)
    rr, cc = _tri(c_)
    tril = jnp.where(rr >= cc, 1.0, 0.0).astype(BF16)
    lo_half = lax.broadcasted_iota(jnp.int32, (c_, 128), 1) < RW_HEAD
    r128, c128 = _tri(128)
    bdmask = (r128 // RW_HEAD) == (c128 // RW_HEAD)
    ones_bd = jnp.where(bdmask, 1.0, 0.0).astype(BF16)
    first_w = lax.broadcasted_iota(jnp.int32, (c_, RW_W), 0) == 0
    first_n = lax.broadcasted_iota(jnp.int32, (c_, 128), 0) == 0

    def mix(x_ref, prev_ref, mu_ref, j):
        x = x_ref[j, rows, :]
        first = first_w if x.shape[1] == RW_W else first_n
        prev = jnp.where(first, prev_ref[j], pltpu.roll(x, 1, 0))
        prev_ref[j] = x[c_ - 1:c_, :]
        return x + (prev - x) * mu_ref[...]

    def bd(y):
        yb = y.astype(BF16)
        zero = jnp.zeros_like(yb)
        return jnp.concatenate([jnp.where(lo_half, yb, zero), jnp.where(lo_half, zero, yb)], axis=0)

    def pk_mul(x, y):
        return jnp.dot(x.astype(BF16), bd(y), preferred_element_type=F32)

    def seg_all(xs):
        st = jnp.dot(jnp.concatenate(xs, axis=0).astype(BF16), ones_bd, preferred_element_type=F32)
        return [st[i * c_:(i + 1) * c_] for i in range(len(xs))]

    zr = [mix(r_ref, pr_s, mur_ref, j) for j in range(ns)]
    zv = [mix(v_ref, pv_s, muv_ref, j) for j in range(ns)]
    tok = [_rw_tokens(zr[j], mix(k_ref, pk_s, muk_ref, j), zv[j],
                      mix(wa_ref, pwa_s, muwa_ref, j), mix(gd_ref, pg_s, mug_ref, j),
                      w0_ref[...], wup_ref[...], a0_ref[...], aup_ref[...],
                      gup_ref[...], kkw_ref[...], kaw_ref[...]) for j in range(ns)]
    yield
    its = [(j, p) for j in range(ns) for p in range(RW_PAIRS)]
    prs = range(len(its))
    lns = [slice(p * 128, (p + 1) * 128) for _, p in its]
    lw = [tok[j][0][:, lns[i]] for i, (j, _) in enumerate(its)]
    a = [tok[j][1][:, lns[i]] for i, (j, _) in enumerate(its)]
    g = [tok[j][2][:, lns[i]] for i, (j, _) in enumerate(its)]
    kkp = [tok[j][3][:, lns[i]] for i, (j, _) in enumerate(its)]
    kp = [tok[j][4][:, lns[i]] for i, (j, _) in enumerate(its)]
    rp = [zr[j][:, lns[i]] for i, (j, _) in enumerate(its)]
    vp = [zv[j][:, lns[i]] for i, (j, _) in enumerate(its)]
    ssq = seg_all([x * x for x in kkp])
    cumf = [_dot2_left(tril, tok[j][0]) for j in range(ns)]
    yield
    cum = [cumf[j][:, lns[i]] for i, (j, _) in enumerate(its)]
    kkn = [kkp[p] * lax.rsqrt(ssq[p] + 1e-12) for p in prs]
    bp = [kkn[p] * a[p] for p in prs]
    cl = [cum[p][c_ - 1:c_, :] for p in prs]
    e_neg = [jnp.exp(-cum[p]) for p in prs]
    e_end = [jnp.exp(cl[p] - cum[p]) for p in prs]
    rh = [rp[p] * jnp.exp(cum[p]) for p in prs]
    kh = [kkn[p] * jnp.exp(cum[p] - lw[p]) for p in prs]
    kt = [kp[p] * e_neg[p] for p in prs]
    bt = [bp[p] * e_neg[p] for p in prs]
    kg = [kp[p] * e_end[p] for p in prs]
    bg = [bp[p] * e_end[p] for p in prs]
    kr = [jnp.concatenate([kh[p], rh[p]], axis=0).astype(BF16) for p in prs]
    ktb = [jnp.concatenate([bd(kt[p]), bd(bt[p])], axis=0) for p in prs]
    pkb = [lax.dot_general(kr[p], ktb[p], _NT, preferred_element_type=F32) for p in prs]
    yield
    ak = [jnp.where(strict_pk, pkb[p][0:c_, 0:128], 0.0) for p in prs]
    gk = [jnp.where(incl_pk, pkb[p][c_:2 * c_, 0:128], 0.0) for p in prs]
    gb = [jnp.where(incl_pk, pkb[p][c_:2 * c_, 128:256], 0.0) for p in prs]
    pw = [jnp.where(strict_pk, -pkb[p][0:c_, 128:256], 0.0) for p in prs]
    tm = [eye_pk + pw[p] for p in prs]
    agv = [pk_mul(jnp.concatenate([ak[p], gk[p]], axis=0), vp[p]) for p in prs]
    pw = [pk_mul(pw[p], pw[p]) for p in prs]
    yield
    for _ in range(4):
        x = [pk_mul(jnp.concatenate([pw[p], tm[p]], axis=0), pw[p]) for p in prs]
        tm = [tm[p] + x[p][c_:2 * c_] for p in prs]
        pw = [x[p][0:c_] for p in prs]
        yield
    tm = [tm[p] + pk_mul(tm[p], pw[p]) for p in prs]
    yield
    ku = [jnp.dot(tm[p].astype(BF16), jnp.concatenate([bd(kh[p]), bd(agv[p][0:c_])], axis=1),
                  preferred_element_type=F32) for p in prs]
    yield
    gku = [jnp.dot(gb[p].astype(BF16),
                   jnp.concatenate([bd(ku[p][:, 0:128]), bd(ku[p][:, 128:256])], axis=1),
                   preferred_element_type=F32) for p in prs]
    d = [_dot_tn(jnp.concatenate([vp[p], -ku[p][:, 128:256]], axis=0),
                 jnp.concatenate([kg[p], bg[p]], axis=0)) for p in prs]
    yield
    rhp = [rh[p] - gku[p][:, 0:128] for p in prs]
    y0 = [agv[p][c_:2 * c_] - gku[p][:, 128:256] for p in prs]
    s = [s_ref[j, p] for j, p in its]
    uy = [_dot_nt(jnp.concatenate([ku[p][:, 0:128], rhp[p]], axis=0), s[p]) for p in prs]
    yield
    ub = [_dot_tn(uy[p][0:c_], bg[p]) for p in prs]
    y = [uy[p][c_:2 * c_] + y0[p] for p in prs]
    mean = seg_all(y)
    yield
    for i, (j, p) in enumerate(its):
        s_ref[j, p] = s[i] * jnp.exp(cl[i]) + jnp.where(bdmask, d[i] - ub[i], 0.0)
    yc = [y[p] - mean[p] * (1.0 / RW_HEAD) for p in prs]
    var = seg_all([x * x for x in yc])
    bon = seg_all([rp[p] * kp[p] * rk_ref[:, lns[p]] for p in prs])
    yield
    for i, (j, _) in enumerate(its):
        yn = yc[i] * lax.rsqrt(var[i] * (1.0 / RW_HEAD) + EPS) * ng_ref[:, lns[i]]
        o_ref[j, rows, lns[i]] = ((yn + bon[i] * vp[i]) * g[i]).astype(BF16)


def _mixers_kernel(*refs, n_chunks, ns):
    (gq, gk, gv, gg, sm, mq, mk, mv, mo, smt, rr, rk, rv, rwa, rgd) = refs[0:15]
    aup, ab, gn, brow, bcol, mng = refs[15:21]
    mu_refs = refs[21:26]
    par_refs = refs[26:35]
    o_gla, o_ml, o_rw, gla_out, c_out, n_out, m_out, rw_out = refs[35:43]
    s_gla, c_s, n_s, m_s, s_rw = refs[43:48]
    prev_refs = refs[48:53]
    t = pl.program_id(1)

    @pl.when(t == 0)
    def _():
        for ref in (s_gla, c_s, n_s, m_s, s_rw) + tuple(prev_refs):
            ref[...] = jnp.zeros_like(ref)

    def chunk(ci, carry):
        rows = pl.ds(pl.multiple_of(ci * CHUNK, CHUNK), CHUNK)
        _interleave(
            {"r": _rw_chunk(rows, ns, rr, rk, rv, rwa, rgd, mu_refs, par_refs, o_rw, s_rw, prev_refs),
             "g": _gla_chunk(rows, ns, gq, gk, gv, gg, sm, aup, ab, gn, o_gla, s_gla),
             "m": _mlstm_chunk(rows, ci, ns, mq, mk, mv, mo, sm, smt, brow, bcol, mng, o_ml, c_s, n_s, m_s)},
            "rmrg")
        return carry

    lax.fori_loop(0, n_chunks, chunk, 0)

    @pl.when(t == pl.num_programs(1) - 1)
    def _():
        gla_out[...] = s_gla[...]
        c_out[...] = c_s[...]
        for j in range(ns):
            for h in range(ML_HEADS):
                n_out[j, h:h + 1, :] = n_s[j, h][0:1, :]
                m_out[j, h:h + 1, :] = m_s[j, h][0:1, :]
            for p in range(RW_PAIRS):
                s = s_rw[j, p]
                rw_out[j, 2 * p] = s[0:RW_HEAD, 0:RW_HEAD]
                rw_out[j, 2 * p + 1] = s[RW_HEAD:, RW_HEAD:]


def _mixers_prompt(z, smt, lp, b_, t_, tt):
    ns = _seq_per_step(b_)
    nc = tt // CHUNK
    z3 = z.reshape(b_, t_, NP)
    zblk = lambda w, off: pl.BlockSpec((ns, tt, w), lambda b, t, c=off // w: (b, t, c))
    const = lambda b, t: (0, 0)
    full = lambda a: pl.BlockSpec(a.shape, const)
    seq = lambda *shape: pl.BlockSpec((ns,) + shape, lambda b, t, n=len(shape): (b,) + (0,) * n)
    params = [lp["aup_pad"], lp["ab"], lp["gla_ng"], lp["brow"], lp["bcol"], lp["ml_ng"], *lp["mus"], *lp["rw"]]
    bf = jax.ShapeDtypeStruct((b_, t_, D_MODEL), BF16)
    outs = pl.pallas_call(
        functools.partial(_mixers_kernel, n_chunks=nc, ns=ns),
        grid=(b_ // ns, t_ // tt),
        in_specs=[zblk(GLA_QK, OFF_GQ), zblk(GLA_QK, OFF_GK), zblk(GLA_V, OFF_GV), zblk(GLA_V, OFF_GG),
                  zblk(128, OFF_SM),
                  zblk(ML_W, OFF_MQ), zblk(ML_W, OFF_MK), zblk(ML_W, OFF_MV), zblk(ML_W, OFF_MO),
                  pl.BlockSpec((ns, nc, 8, CHUNK), lambda b, t: (b, t, 0, 0)),
                  zblk(RW_W, OFF_RW), zblk(RW_W, OFF_RW + RW_W), zblk(RW_W, OFF_RW + 2 * RW_W),
                  zblk(128, OFF_RW + 3 * RW_W), zblk(128, OFF_RW + 3 * RW_W + 128)]
                 + [full(a) for a in params],
        out_specs=[pl.BlockSpec((ns, tt, D_MODEL), lambda b, t: (b, t, 0))] * 3
                  + [seq(GLA_HEADS, GLA_DK, GLA_DV), seq(ML_HEADS, ML_DH, ML_DH), seq(ML_HEADS, ML_DH),
                     seq(ML_HEADS, 128), seq(RW_HEADS, RW_HEAD, RW_HEAD)],
        out_shape=[bf, bf, bf,
                   jax.ShapeDtypeStruct((b_, GLA_HEADS, GLA_DK, GLA_DV), F32),
                   jax.ShapeDtypeStruct((b_, ML_HEADS, ML_DH, ML_DH), F32),
                   jax.ShapeDtypeStruct((b_, ML_HEADS, ML_DH), F32),
                   jax.ShapeDtypeStruct((b_, ML_HEADS, 128), F32),
                   jax.ShapeDtypeStruct((b_, RW_HEADS, RW_HEAD, RW_HEAD), F32)],
        scratch_shapes=[pltpu.VMEM((ns, GLA_HEADS, GLA_DK, GLA_DV), F32),
                        pltpu.VMEM((ns, ML_HEADS, ML_DH, ML_DH), F32),
                        pltpu.VMEM((ns, ML_HEADS, 8, ML_DH), F32),
                        pltpu.VMEM((ns, ML_HEADS, 8, 128), F32),
                        pltpu.VMEM((ns, RW_PAIRS, 128, 128), F32),
                        pltpu.VMEM((ns, 1, RW_W), F32), pltpu.VMEM((ns, 1, RW_W), F32),
                        pltpu.VMEM((ns, 1, RW_W), F32), pltpu.VMEM((ns, 1, 128), F32),
                        pltpu.VMEM((ns, 1, 128), F32)],
        compiler_params=pltpu.CompilerParams(
            dimension_semantics=("parallel", "arbitrary"), vmem_limit_bytes=MIXER_VMEM_LIMIT),
        name="mixers_prompt",
    )(*([z3] * 9), smt, *([z3] * 5), *params)
    o_gla, o_ml, o_rw, gla_new, c_new, n_new, m_new, rw_new = outs
    flat = lambda o: o.reshape(b_ * t_, D_MODEL)
    return flat(o_gla), flat(o_ml), flat(o_rw), (gla_new, c_new, n_new, m_new[:, :, 0], rw_new)


def _sample_prep_kernel(sm_ref, aup_ref, ab_ref, mi_ref, mf_ref, m_ref, ib_ref, fb_ref,
                        r_ref, k_ref, v_ref, wa_ref, gd_ref, sh_r, sh_k, sh_v, sh_wa, sh_g,
                        mur_ref, muk_ref, muv_ref, muwa_ref, mug_ref,
                        w0_ref, wup_ref, a0_ref, aup2_ref, gup_ref, kkw_ref, kaw_ref,
                        gla_la_ref, mnew_ref, wend_ref, ws_ref,
                        rr_ref, rk_ref, rv_ref, rg_ref, kkt_ref, bt_ref, kt_ref, rt_ref, wt_ref, vt_ref):
    gla_la_ref[...] = _logsig(_dot(sm_ref[...], aup_ref[...]) + ab_ref[...]) * (1.0 / GLA_TAU)
    i_pre = mi_ref[...] + ib_ref[...]
    logf = _logsig(mf_ref[...] + fb_ref[...])
    m = m_ref[...]
    m_new = jnp.maximum(logf + m, i_pre)
    mnew_ref[...] = m_new
    wend_ref[...] = jnp.exp(logf + m - m_new)
    ws_ref[...] = jnp.exp(i_pre - m_new)
    mixf = lambda x, s, mu: x[...] + (s[...] - x[...]) * mu[...]
    zr = mixf(r_ref, sh_r, mur_ref)
    zk = mixf(k_ref, sh_k, muk_ref)
    zv = mixf(v_ref, sh_v, muv_ref)
    zwa = mixf(wa_ref, sh_wa, muwa_ref)
    zg = mixf(gd_ref, sh_g, mug_ref)
    lw, a, g, kk0, kmod = _rw_tokens(zr, zk, zv, zwa, zg, w0_ref[...], wup_ref[...], a0_ref[...],
                                     aup2_ref[...], gup_ref[...], kkw_ref[...], kaw_ref[...])
    ones_bd = _ones_bd()
    for p in range(RW_PAIRS):
        ln = slice(p * 128, (p + 1) * 128)
        kkp = kk0[:, ln]
        kkn = kkp * lax.rsqrt(_seg(kkp * kkp, ones_bd) + 1e-12)
        kkt_ref[ln, :] = kkn.T
        bt_ref[ln, :] = (kkn * a[:, ln]).T
    rr_ref[...] = zr
    rk_ref[...] = kmod
    rv_ref[...] = zv
    rg_ref[...] = g
    kt_ref[...] = kmod.T
    rt_ref[...] = zr.T
    wt_ref[...] = jnp.exp(lw).T
    vt_ref[...] = zv.T


STEP_BB = 8


def _row_select(rows_list, rid):
    acc = rows_list[0]
    for b in range(1, len(rows_list)):
        acc = jnp.where(rid == b, rows_list[b], acc)
    return acc


def _gla_step_kernel(s_ref, q_ref, k_ref, la_ref, v_ref, *rest):
    sn_ref, o_ref = rest[-2:]
    rid_k = lax.broadcasted_iota(jnp.int32, (STEP_BB, GLA_DK), 0)
    rid_v = lax.broadcasted_iota(jnp.int32, (STEP_BB, GLA_DV), 0)
    bs = range(STEP_BB)
    onehot = [jnp.where(rid_v == b, 1.0, 0.0).astype(BF16) for b in bs]
    for h in range(GLA_HEADS):
        kl = slice(h * GLA_DK, (h + 1) * GLA_DK)
        vl = slice(h * GLA_DV, (h + 1) * GLA_DV)
        qt = q_ref[:, kl] * (GLA_DK ** -0.5)
        kt = k_ref[:, kl]
        lt = la_ref[:, kl]
        vt = v_ref[:, vl]
        dec = [jnp.exp(_dot2_tn(lt, onehot[b])) for b in bs]
        kv = [_dot3(jnp.where(rid_k == b, kt, 0.0), vt, _TN) for b in bs]
        sn = [dec[b] * s_ref[0, b, h] + kv[b] for b in bs]
        for b in bs:
            sn_ref[0, b, h] = sn[b]
        o_ref[:, vl] = _row_select([_dot3(qt, sn[b]) for b in bs], rid_v)


def _mlstm_step_kernel(c_ref, n_ref, q_ref, k_ref, v_ref, wend_ref, ws_ref, mnew_ref, *rest):
    cn_ref, nn_ref, h_ref = rest[-3:]
    rid = lax.broadcasted_iota(jnp.int32, (STEP_BB, ML_DH), 0)
    for h in range(ML_HEADS):
        hl = slice(h * ML_DH, (h + 1) * ML_DH)
        qt = q_ref[:, hl] * (ML_DH ** -0.5)
        kt = k_ref[:, hl]
        vt = v_ref[:, hl]
        wend = wend_ref[:, h:h + 1]
        ws = ws_ref[:, h:h + 1]
        nn = wend * n_ref[0, :, hl] + ws * kt
        nn_ref[0, :, hl] = nn
        den = jnp.sum(qt * nn, axis=1, keepdims=True)
        kws = kt * ws
        nums = []
        for half in range(2):
            bs = range(half * 4, half * 4 + 4)
            kwv = [_dot3(jnp.where(rid == b, kws, 0.0), vt, _TN) for b in bs]
            cn = [wend_ref[b:b + 1, h:h + 1] * c_ref[0, b, h] + kwv[i] for i, b in enumerate(bs)]
            for i, b in enumerate(bs):
                cn_ref[0, b, h] = cn[i]
            nums += [_dot3(qt, cn[i]) for i in range(4)]
        num = _row_select(nums, rid)
        h_ref[:, hl] = num / jnp.maximum(jnp.abs(den), jnp.exp(-mnew_ref[:, h:h + 1]))


RW_STEP_HB = 2


def _rw_step_kernel(s_ref, kk_ref, b_ref, k_ref, r_ref, w_ref, v_ref, *rest):
    sn_ref, y_ref = rest[-2:]
    n = RW_HEAD
    for hh in range(RW_STEP_HB):
        fl = slice(hh * n, (hh + 1) * n)
        nkk, bb, kt, rt, wt = -kk_ref[fl, :], b_ref[fl, :], k_ref[fl, :], r_ref[fl, :], w_ref[fl, :]

        def row(i, carry):
            s = s_ref[0, hh, i]
            sa = jnp.sum(s * nkk, axis=0, keepdims=True)
            sn = s * wt + sa * bb + v_ref[pl.ds(hh * n + i, 1), :] * kt
            sn_ref[0, hh, i] = sn
            y_ref[pl.ds(hh * n + i, 1), :] = jnp.sum(sn * rt, axis=0, keepdims=True)
            return carry

        lax.fori_loop(0, n, row, 0, unroll=4)


def _rw_state_call(l, state_t, vecs_t, prev_out):
    nb = state_t.shape[-1]
    st_spec = pl.BlockSpec((1, RW_STEP_HB, RW_HEAD, RW_HEAD, nb), lambda h: (l, h, 0, 0, 0))
    vec_spec = pl.BlockSpec((RW_STEP_HB * RW_HEAD, nb), lambda h: (h, 0))
    args = [state_t] + list(vecs_t)
    in_specs = [st_spec] + [vec_spec] * len(vecs_t)
    aliases = {}
    if prev_out is not None:
        aliases[len(args)] = 0
        in_specs.append(pl.BlockSpec(memory_space=pl.ANY))
        args.append(prev_out)
    return pl.pallas_call(
        _rw_step_kernel,
        grid=(RW_HEADS // RW_STEP_HB,),
        in_specs=in_specs,
        out_specs=[st_spec, vec_spec],
        out_shape=[jax.ShapeDtypeStruct(state_t.shape, F32), jax.ShapeDtypeStruct((RW_W, nb), F32)],
        input_output_aliases=aliases,
        compiler_params=pltpu.CompilerParams(
            dimension_semantics=("parallel",), vmem_limit_bytes=VMEM_LIMIT),
        name="rwkv_step",
    )(*args)


def _sample_post_kernel(og_ref, gg_ref, gn_ref, hm_ref, mo_ref, mn_ref,
                        yt_ref, r_ref, k_ref, v_ref, g_ref, rk_ref, ng_ref,
                        ogla_ref, oml_ref, orw_ref):
    for h in range(GLA_HEADS):
        vl = slice(h * GLA_DV, (h + 1) * GLA_DV)
        o = og_ref[:, vl]
        on = o * lax.rsqrt(jnp.mean(o * o, axis=-1, keepdims=True) + EPS) * gn_ref[...]
        gg = gg_ref[:, vl]
        ogla_ref[:, vl] = (on * (gg * _sigmoid(gg))).astype(BF16)
    for h in range(ML_HEADS):
        hl = slice(h * ML_DH, (h + 1) * ML_DH)
        hm = hm_ref[:, hl]
        y = hm - jnp.mean(hm, axis=-1, keepdims=True)
        y = y * lax.rsqrt(jnp.mean(y * y, axis=-1, keepdims=True) + EPS) * mn_ref[...]
        oml_ref[:, hl] = (y * _sigmoid(mo_ref[:, hl])).astype(BF16)
    ones_bd = _ones_bd()
    for p in range(RW_PAIRS):
        ln = slice(p * 128, (p + 1) * 128)
        y = yt_ref[ln, :].T
        orw_ref[:, ln] = _rw_epilogue(y, r_ref[:, ln], k_ref[:, ln], v_ref[:, ln],
                                      g_ref[:, ln], rk_ref[:, ln], ng_ref[:, ln], ones_bd)


def _full_call(kernel_fn, args, out_shapes, name):
    nd = lambda a: (lambda: (0,) * len(a.shape))
    return pl.pallas_call(
        kernel_fn,
        in_specs=[pl.BlockSpec(a.shape, nd(a)) for a in args],
        out_specs=[pl.BlockSpec(s.shape, nd(s)) for s in out_shapes],
        out_shape=out_shapes,
        compiler_params=pltpu.CompilerParams(vmem_limit_bytes=VMEM_LIMIT),
        name=name,
    )(*args)


def _state_call(kernel_fn, l, states, vecs, prev_outs, out_width, name):
    nb = states[0].shape[1]
    st_spec = lambda a: pl.BlockSpec((1, STEP_BB) + tuple(a.shape[2:]),
                                     lambda i, n=a.ndim: (l, i) + (0,) * (n - 2))
    in_specs = [st_spec(a) for a in states]
    in_specs += [pl.BlockSpec((STEP_BB, w), lambda i, c=c: (i, c)) for _, w, c in vecs]
    args = list(states) + [a for a, _, _ in vecs]
    aliases = {}
    for j, po in enumerate(prev_outs or ()):
        aliases[len(args)] = j
        in_specs.append(pl.BlockSpec(memory_space=pl.ANY))
        args.append(po)
    out_shape = [jax.ShapeDtypeStruct(a.shape, F32) for a in states]
    out_shape.append(jax.ShapeDtypeStruct((nb, out_width), F32))
    out_specs = [st_spec(a) for a in states] + [pl.BlockSpec((STEP_BB, out_width), lambda i: (i, 0))]
    return pl.pallas_call(
        kernel_fn,
        grid=(nb // STEP_BB,),
        in_specs=in_specs,
        out_specs=out_specs,
        out_shape=out_shape,
        input_output_aliases=aliases,
        compiler_params=pltpu.CompilerParams(
            dimension_semantics=("parallel",), vmem_limit_bytes=VMEM_LIMIT),
        name=name,
    )(*args)


def _prep_layer_params(l, p):
    w = p["w_in"][l]
    cols = ((0, 2048), (2064, 3088), (3088, 6160), (6168, 7192), (7192, 13592), (2048, 2064), (6160, 6168))
    w_in_p = jnp.concatenate([w[:, a:b].astype(BF16) for a, b in cols]
                             + [jnp.zeros((D_MODEL, NP - 13592), BF16)], axis=1)
    aup_pad = jnp.zeros((128, GLA_QK), F32).at[0:GLA_RANK].set(p["gla_a_up"][l])
    brow = (jnp.zeros((1, 128), F32).at[0, SM_I:SM_I + 4].set(p["ml_i_bias"][l])
            .at[0, SM_F:SM_F + 4].set(p["ml_f_bias"][l]))
    bcol = jnp.broadcast_to(jnp.concatenate([p["ml_i_bias"][l], p["ml_f_bias"][l]])[:, None], (8, CHUNK))
    mu = p["rw_mu"][l][None, :]
    mus = (mu[:, 0:1024], mu[:, 1024:2048], mu[:, 2048:3072], mu[:, 3072:3200], mu[:, 3200:3328])
    wup_pad = jnp.zeros((128, RW_W), F32).at[0:64].set(p["rw_w_up"][l])
    aup2_pad = jnp.zeros((128, RW_W), F32).at[64:128].set(p["rw_a_up"][l])
    row = lambda a: a.reshape(1, -1)
    rw_params = (row(p["rw_w0"][l]), wup_pad, row(p["rw_a0"][l]), aup2_pad, p["rw_g_up"][l],
                 row(p["rw_k_k"][l]), row(p["rw_k_a"][l]), row(p["rw_r_k"][l]), row(p["rw_norm_g"][l]))
    return dict(
        norm1_g=row(p["norm1_g"][l]), w_in=w_in_p, aup_pad=aup_pad, ab=row(p["gla_a_bias"][l]),
        gla_ng=row(p["gla_norm_g"][l]), brow=brow, bcol=bcol, ib=row(p["ml_i_bias"][l]),
        fb=row(p["ml_f_bias"][l]), ml_ng=row(p["ml_norm_g"][l]), mus=mus, rw=rw_params,
        wb=p["w_branch"][l].astype(BF16), wo=p["w_out"][l].astype(BF16), norm2_g=row(p["norm2_g"][l]),
        w1=p["w_ff1"][l].astype(BF16), w2=p["w_ff2"][l].astype(BF16))


def _prompt_layer(x, lp, fg, final, b_, t_):
    rows = x.shape[0]
    tm = min(512, rows)
    tm_big = min(1024, rows)
    z = _proj_in(x, lp["norm1_g"], lp["w_in"], tm_big)
    sm = z[:, OFF_SM + SM_I:OFF_SM + SM_I + 8]
    smt = sm.reshape(b_, t_ // CHUNK, CHUNK, 8).transpose(0, 1, 3, 2)
    o_gla, o_ml, o_rw, new_states = _mixers_prompt(z, smt, lp, b_, t_, min(MIXER_TT, t_))
    x = _merge(o_gla, o_ml, o_rw, z, x, lp["wb"], lp["wo"], tm)
    x = _ffn(x, lp["norm2_g"], lp["w1"], lp["w2"], fg, final, tm_big)
    shift = z.reshape(b_, t_, NP)[:, t_ - 1, OFF_RW:OFF_RW + RW_COLS]
    return x, new_states + (shift,)


def _sample_layer(l, x, states, prev, lp, fg, final):
    st_gla, st_c, st_n, st_m, st_rw, st_shift = states
    nb = x.shape[0]
    z = _proj_in(x, lp["norm1_g"], lp["w_in"], nb)
    zc = lambda off, w: z[:, off:off + w]
    shift = st_shift[l]
    sh = (shift[:, 0:1024], shift[:, 1024:2048], shift[:, 2048:3072], shift[:, 3072:3200], shift[:, 3200:3328])
    w0, wup, a0, aup2, gup, kkw, kaw, rk, ng = lp["rw"]
    sds = lambda *s: jax.ShapeDtypeStruct(s, F32)
    prep_args = [zc(OFF_SM, 128), lp["aup_pad"], lp["ab"],
                 zc(OFF_SM + SM_I, 4), zc(OFF_SM + SM_F, 4), st_m[l], lp["ib"], lp["fb"],
                 zc(OFF_RW, 1024), zc(OFF_RW + 1024, 1024), zc(OFF_RW + 2048, 1024),
                 zc(OFF_RW + 3072, 128), zc(OFF_RW + 3200, 128), *sh, *lp["mus"],
                 w0, wup, a0, aup2, gup, kkw, kaw]
    (la, m_new, wend, ws, rr, rkm, rv, rg, kkt, bt, kt, rt, wt, vt) = _full_call(
        _sample_prep_kernel, prep_args,
        [sds(nb, GLA_QK), sds(nb, 4), sds(nb, 4), sds(nb, 4)] + [sds(nb, RW_W)] * 4 + [sds(RW_W, nb)] * 6,
        "sample_prep")

    pv = lambda *idx: None if prev is None else [prev[i] for i in idx]
    gla_all, o_g = _state_call(
        _gla_step_kernel, l, [st_gla],
        [(z, GLA_QK, OFF_GQ // GLA_QK), (z, GLA_QK, OFF_GK // GLA_QK), (la, GLA_QK, 0),
         (z, GLA_V, OFF_GV // GLA_V)], pv(0), GLA_V, "gla_step")
    c_all, n_all, hm = _state_call(
        _mlstm_step_kernel, l, [st_c, st_n],
        [(z, ML_W, OFF_MQ // ML_W), (z, ML_W, OFF_MK // ML_W), (z, ML_W, OFF_MV // ML_W),
         (wend, 4, 0), (ws, 4, 0), (m_new, 4, 0)], pv(1, 2), ML_W, "mlstm_step")
    rw_all, y = _rw_state_call(l, st_rw, (kkt, bt, kt, rt, wt, vt), None if prev is None else prev[3])

    bf = lambda: jax.ShapeDtypeStruct((nb, D_MODEL), BF16)
    o_gla, o_ml, o_rw = _full_call(
        _sample_post_kernel,
        [o_g, zc(OFF_GG, GLA_V), lp["gla_ng"], hm, zc(OFF_MO, ML_W), lp["ml_ng"],
         y, rr, rkm, rv, rg, rk, ng],
        [bf(), bf(), bf()], "sample_post")
    x = _merge(o_gla, o_ml, o_rw, z, x, lp["wb"], lp["wo"], nb)
    x = _ffn(x, lp["norm2_g"], lp["w1"], lp["w2"], fg, final, nb)
    return x, (gla_all, c_all, n_all, rw_all), (m_new, zc(OFF_RW, RW_COLS))


def kernel(x_prompt, x_sample, state_gla, state_mlstm_c, state_mlstm_n, state_mlstm_m, state_rwkv, state_shift, norm1_g, w_in, gla_a_up, gla_a_bias, gla_norm_g, ml_i_bias, ml_f_bias, ml_norm_g, rw_mu, rw_w0, rw_w_up, rw_a0, rw_a_up, rw_g_up, rw_k_k, rw_k_a, rw_r_k, rw_norm_g, w_branch, w_out, norm2_g, w_ff1, w_ff2, final_g):
    p = dict(norm1_g=norm1_g, w_in=w_in, gla_a_up=gla_a_up, gla_a_bias=gla_a_bias, gla_norm_g=gla_norm_g,
             ml_i_bias=ml_i_bias, ml_f_bias=ml_f_bias, ml_norm_g=ml_norm_g, rw_mu=rw_mu, rw_w0=rw_w0,
             rw_w_up=rw_w_up, rw_a0=rw_a0, rw_a_up=rw_a_up, rw_g_up=rw_g_up, rw_k_k=rw_k_k, rw_k_a=rw_k_a,
             rw_r_k=rw_r_k, rw_norm_g=rw_norm_g, w_branch=w_branch, w_out=w_out, norm2_g=norm2_g,
             w_ff1=w_ff1, w_ff2=w_ff2)
    depth = w_in.shape[0]
    b_, t_, _ = x_prompt.shape
    nb = x_sample.shape[0]
    fg = final_g.reshape(1, D_MODEL)
    xp = x_prompt.reshape(b_ * t_, D_MODEL)
    xs = x_sample.reshape(nb, D_MODEL)
    s_in = (state_gla, state_mlstm_c, state_mlstm_n.reshape(depth, nb, ML_W), state_mlstm_m,
            jnp.transpose(state_rwkv, (0, 2, 3, 4, 1)), state_shift)
    p_states, s_big, s_small = [], None, []
    for l in range(depth):
        lp = _prep_layer_params(l, p)
        final = l == depth - 1
        xp, stp = _prompt_layer(xp, lp, fg, final, b_, t_)
        xs, s_big, small = _sample_layer(l, xs, s_in, s_big, lp, fg, final)
        p_states.append(stp)
        s_small.append(small)
    ps = [jnp.stack([s[i] for s in p_states]) for i in range(6)]
    gla_s, mlc_s, mln_s, rw_s = s_big
    rw_s = jnp.transpose(rw_s, (0, 4, 1, 2, 3))
    mln_s = mln_s.reshape(depth, nb, ML_HEADS, ML_DH)
    mlm_s = jnp.stack([s[0] for s in s_small])
    shift_s = jnp.stack([s[1] for s in s_small])
    return (xp.reshape(b_, t_, D_MODEL), xs.reshape(nb, 1, D_MODEL),
            ps[0], gla_s, ps[1], mlc_s, ps[2], mln_s, ps[3], mlm_s, ps[4], rw_s, ps[5], shift_s)
```

```python
import functools
import math

import jax
import jax.numpy as jnp
from jax import lax
from jax.experimental import pallas as pl
from jax.experimental.pallas import tpu as pltpu

F32 = jnp.float32
BF16 = jnp.bfloat16

D_MODEL = 1024
GLA_HEADS, GLA_DK, GLA_DV = 4, 128, 256
GLA_QK, GLA_V, GLA_RANK, GLA_TAU = 512, 1024, 16, 16.0
ML_HEADS, ML_DH, ML_W = 4, 256, 1024
RW_HEAD, RW_W, RW_HEADS = 64, 1024, 16
RW_PAIRS = RW_HEADS // 2
RW_COLS = 3 * RW_W + 64 + 64 + 128
D_FF = 4 * D_MODEL
EPS = 1e-6
CHUNK = 64

OFF_GQ, OFF_GK, OFF_GV, OFF_GG = 0, 512, 1024, 2048
OFF_MQ, OFF_MK, OFF_MV, OFF_MO = 3072, 4096, 5120, 6144
OFF_GATE = 7168
OFF_RW = 10240
OFF_SM = 13568
NP = 13824
SM_I, SM_F = 16, 20

VMEM_LIMIT = 48 * 1024 * 1024
MIXER_VMEM_LIMIT = 56 * 1024 * 1024
MIXER_TT = 128


def _dot(a, b):
    return jnp.dot(a.astype(BF16), b.astype(BF16), preferred_element_type=F32)


def _dot_nt(a, b):
    return lax.dot_general(a.astype(BF16), b.astype(BF16), (((1,), (1,)), ((), ())),
                           preferred_element_type=F32)


def _dot_tn(a, b):
    return lax.dot_general(a.astype(BF16), b.astype(BF16), (((0,), (0,)), ((), ())),
                           preferred_element_type=F32)


def _split(x):
    hi = x.astype(BF16)
    lo = (x - hi.astype(F32)).astype(BF16)
    return hi, lo


def _dot2(x, w_exact):
    hi, lo = _split(x)
    return (jnp.dot(hi, w_exact, preferred_element_type=F32)
            + jnp.dot(lo, w_exact, preferred_element_type=F32))


def _dot2_tn(x, w_exact):
    hi, lo = _split(x)
    dn = (((0,), (0,)), ((), ()))
    return (lax.dot_general(hi, w_exact, dn, preferred_element_type=F32)
            + lax.dot_general(lo, w_exact, dn, preferred_element_type=F32))


def _dot2_left(w_exact, x):
    hi, lo = _split(x)
    return (jnp.dot(w_exact, hi, preferred_element_type=F32)
            + jnp.dot(w_exact, lo, preferred_element_type=F32))


_NN = (((1,), (0,)), ((), ()))
_TN = (((0,), (0,)), ((), ()))
_NT = (((1,), (1,)), ((), ()))


def _dot3(a, b, dn=_NN):
    ah, al = _split(a)
    bh, bl = _split(b)
    f = lambda x, y: lax.dot_general(x, y, dn, preferred_element_type=F32)
    return f(ah, bh) + f(al, bh) + f(ah, bl)


def _logsig(x):
    return jnp.minimum(x, 0.0) - jnp.log(1.0 + jnp.exp(-jnp.abs(x)))


def _sigmoid(x):
    return jax.nn.sigmoid(x)


def _rms(x, g):
    return x * lax.rsqrt(jnp.mean(x * x, axis=-1, keepdims=True) + EPS) * g


def _tri(n, m=None):
    m = n if m is None else m
    r = lax.broadcasted_iota(jnp.int32, (n, m), 0)
    c = lax.broadcasted_iota(jnp.int32, (n, m), 1)
    return r, c


def _proj_in_kernel(x_ref, g_ref, w_ref, o_ref, h_ref):
    @pl.when(pl.program_id(1) == 0)
    def _():
        h_ref[...] = _rms(x_ref[...], g_ref[...]).astype(BF16)

    o_ref[...] = jnp.dot(h_ref[...], w_ref[...], preferred_element_type=F32)


def _proj_in(x, g, w, tm, tn=2304):
    m = x.shape[0]
    return pl.pallas_call(
        _proj_in_kernel,
        grid=(m // tm, NP // tn),
        in_specs=[pl.BlockSpec((tm, D_MODEL), lambda i, j: (i, 0)),
                  pl.BlockSpec((1, D_MODEL), lambda i, j: (0, 0)),
                  pl.BlockSpec((D_MODEL, tn), lambda i, j: (0, j))],
        out_specs=pl.BlockSpec((tm, tn), lambda i, j: (i, j)),
        out_shape=jax.ShapeDtypeStruct((m, NP), F32),
        scratch_shapes=[pltpu.VMEM((tm, D_MODEL), BF16)],
        compiler_params=pltpu.CompilerParams(
            dimension_semantics=("parallel", "arbitrary"), vmem_limit_bytes=VMEM_LIMIT),
        name="proj_in",
    )(x, g, w)


def _merge_kernel(og_ref, om_ref, or_ref, g0_ref, g1_ref, g2_ref, x_ref, wb_ref, wo_ref, o_ref):
    acc = jnp.dot(og_ref[...], wb_ref[0], preferred_element_type=F32) * _sigmoid(g0_ref[...])
    acc += jnp.dot(om_ref[...], wb_ref[1], preferred_element_type=F32) * _sigmoid(g1_ref[...])
    acc += jnp.dot(or_ref[...], wb_ref[2], preferred_element_type=F32) * _sigmoid(g2_ref[...])
    o_ref[...] = x_ref[...] + jnp.dot(acc.astype(BF16), wo_ref[...], preferred_element_type=F32)


def _merge(o_gla, o_ml, o_rw, z, x, wb, wo, tm):
    m = x.shape[0]
    row = lambda i: (i, 0)
    gate_blk = OFF_GATE // D_MODEL
    return pl.pallas_call(
        _merge_kernel,
        grid=(m // tm,),
        in_specs=[pl.BlockSpec((tm, D_MODEL), row),
                  pl.BlockSpec((tm, D_MODEL), row),
                  pl.BlockSpec((tm, D_MODEL), row),
                  pl.BlockSpec((tm, D_MODEL), lambda i: (i, gate_blk)),
                  pl.BlockSpec((tm, D_MODEL), lambda i: (i, gate_blk + 1)),
                  pl.BlockSpec((tm, D_MODEL), lambda i: (i, gate_blk + 2)),
                  pl.BlockSpec((tm, D_MODEL), row),
                  pl.BlockSpec((3, D_MODEL, D_MODEL), lambda i: (0, 0, 0)),
                  pl.BlockSpec((D_MODEL, D_MODEL), lambda i: (0, 0))],
        out_specs=pl.BlockSpec((tm, D_MODEL), row),
        out_shape=jax.ShapeDtypeStruct((m, D_MODEL), F32),
        compiler_params=pltpu.CompilerParams(
            dimension_semantics=("parallel",), vmem_limit_bytes=VMEM_LIMIT),
        name="merge",
    )(o_gla, o_ml, o_rw, z, z, z, x, wb, wo)


def _ffn_kernel(x_ref, g_ref, w1_ref, w2_ref, fg_ref, o_ref, h_ref, acc_ref, *, final):
    f = pl.program_id(1)

    @pl.when(f == 0)
    def _():
        h_ref[...] = _rms(x_ref[...], g_ref[...]).astype(BF16)
        acc_ref[...] = jnp.zeros_like(acc_ref)

    a = jnp.dot(h_ref[...], w1_ref[...], preferred_element_type=F32)
    a = jnp.square(jnp.maximum(a, 0.0))
    acc_ref[...] += jnp.dot(a.astype(BF16), w2_ref[...], preferred_element_type=F32)

    @pl.when(f == pl.num_programs(1) - 1)
    def _():
        y = x_ref[...] + acc_ref[...]
        if final:
            y = _rms(y, fg_ref[...])
        o_ref[...] = y


def _ffn(x, g, w1, w2, fg, final, tm, tf=1024):
    m = x.shape[0]
    return pl.pallas_call(
        functools.partial(_ffn_kernel, final=final),
        grid=(m // tm, D_FF // tf),
        in_specs=[pl.BlockSpec((tm, D_MODEL), lambda i, f: (i, 0)),
                  pl.BlockSpec((1, D_MODEL), lambda i, f: (0, 0)),
                  pl.BlockSpec((D_MODEL, tf), lambda i, f: (0, f)),
                  pl.BlockSpec((tf, D_MODEL), lambda i, f: (f, 0)),
                  pl.BlockSpec((1, D_MODEL), lambda i, f: (0, 0))],
        out_specs=pl.BlockSpec((tm, D_MODEL), lambda i, f: (i, 0)),
        out_shape=jax.ShapeDtypeStruct((m, D_MODEL), F32),
        scratch_shapes=[pltpu.VMEM((tm, D_MODEL), BF16), pltpu.VMEM((tm, D_MODEL), F32)],
        compiler_params=pltpu.CompilerParams(
            dimension_semantics=("parallel", "arbitrary"), vmem_limit_bytes=VMEM_LIMIT),
        name="ffn",
    )(x, g, w1, w2, fg)


def _rw_tokens(zr, zk, zv, zwa, zg, w0, wup, a0, aup, gup, kkw, kaw):
    u = w0 + _dot(jnp.tanh(zwa), wup)
    lw = _sigmoid(u) * (-math.exp(-0.5))
    a = _sigmoid(a0 + _dot(zwa, aup))
    g = _dot(_sigmoid(zg), gup)
    kk0 = zk * kkw
    kmod = zk * (1.0 + (a - 1.0) * kaw)
    return lw, a, g, kk0, kmod


def _ones_bd():
    r, c = _tri(128)
    return jnp.where((r // RW_HEAD) == (c // RW_HEAD), 1.0, 0.0).astype(BF16)


def _seg(x, ones_bd):
    return _dot2(x, ones_bd)


def _rw_epilogue(y, r, kmod, v, g, rk, ng, ones_bd):
    mean = _seg(y, ones_bd) * (1.0 / RW_HEAD)
    yc = y - mean
    var = _seg(yc * yc, ones_bd) * (1.0 / RW_HEAD)
    yn = yc * lax.rsqrt(var + EPS) * ng
    bonus = _seg(r * kmod * rk, ones_bd) * v
    return ((yn + bonus) * g).astype(BF16)


def _seq_per_step(b_):
    return 2 if b_ % 2 == 0 else 1


def _interleave(gens, pattern):
    live = dict(gens)
    while live:
        for key in pattern:
            g = live.get(key)
            if g is None:
                continue
            try:
                next(g)
            except StopIteration:
                del live[key]


def _gla_chunk(rows, ns, q_ref, k_ref, v_ref, gg_ref, sm_ref, aup_ref, ab_ref, gn_ref, o_ref, s_ref):
    c_ = CHUNK
    r, c = _tri(c_)
    incl = r >= c
    tril = jnp.where(incl, 1.0, 0.0).astype(BF16)
    ones_c = jnp.ones((c_, GLA_DK), BF16)
    tokw = []
    for j in range(ns):
        x = _dot(sm_ref[j, rows, :], aup_ref[...]) + ab_ref[...]
        la = _logsig(x) * (1.0 / GLA_TAU)
        b = _dot2_left(tril, la)
        dec = jnp.exp(_dot2_tn(la, ones_c))
        q = q_ref[j, rows, :] * (GLA_DK ** -0.5)
        k = k_ref[j, rows, :]
        tokw.append((q * jnp.exp(b), k * jnp.exp(-b), k * jnp.exp(b[c_ - 1:c_, :] - b), dec))
    yield
    its = [(j, h) for j in range(ns) for h in range(GLA_HEADS)]
    hs = range(len(its))
    kls = [slice(h * GLA_DK, (h + 1) * GLA_DK) for _, h in its]
    vls = [slice(h * GLA_DV, (h + 1) * GLA_DV) for _, h in its]
    qe = [tokw[j][0][:, kls[i]] for i, (j, _) in enumerate(its)]
    ke = [tokw[j][1][:, kls[i]] for i, (j, _) in enumerate(its)]
    kd = [tokw[j][2][:, kls[i]] for i, (j, _) in enumerate(its)]
    v = [v_ref[j, rows, vls[i]] for i, (j, _) in enumerate(its)]
    att = [jnp.where(incl, _dot_nt(qe[i], ke[i]), 0.0) for i in hs]
    yield
    s = [s_ref[j, h] for j, h in its]
    oi = [_dot(qe[i], s[i]) for i in hs]
    yield
    kv = [_dot_tn(kd[i], v[i]) for i in hs]
    yield
    o = [oi[i] + _dot(att[i], v[i]) for i in hs]
    for i, (j, h) in enumerate(its):
        dh = tokw[j][3][kls[i], :]
        s_ref[j, h] = s[i] * jnp.concatenate([dh, dh], axis=1) + kv[i]
    yield
    msq = [jnp.mean(o[i] * o[i], axis=-1, keepdims=True) for i in hs]
    yield
    for i, (j, _) in enumerate(its):
        on = o[i] * lax.rsqrt(msq[i] + EPS) * gn_ref[...]
        gg = gg_ref[j, rows, vls[i]]
        o_ref[j, rows, vls[i]] = (on * (gg * _sigmoid(gg))).astype(BF16)


def _mlstm_chunk(rows, ns, q_ref, k_ref, v_ref, mo_ref, sm_ref, brow_ref, bcol_ref, ng_ref,
                 o_ref, c_ref, n_ref, m_ref):
    c_ = CHUNK
    r, c = _tri(c_)
    incl = r >= c
    tril = jnp.where(incl, 1.0, 0.0).astype(BF16)
    triu = jnp.where(r <= c, 1.0, 0.0).astype(BF16)
    sub8 = lax.broadcasted_iota(jnp.int32, (8, c_), 0)
    eye = jnp.where(r == c, 1.0, 0.0).astype(BF16)
    sm, fcol, smt, frow = [], [], [], []
    for j in range(ns):
        raw = sm_ref[j, rows, :]
        sm.append(raw + brow_ref[...])
        fcol.append(_dot2_left(tril, _logsig(sm[j])))
        hi, lo = _split(raw)
        lo2 = (raw - hi.astype(F32) - lo.astype(F32)).astype(BF16)
        rawt = sum(lax.dot_general(piece, eye, _TN, preferred_element_type=F32) for piece in (hi, lo, lo2))
        smt.append(rawt[SM_I:SM_I + 8, :] + bcol_ref[...])
        frow.append(_dot2(jnp.where(sub8 >= 4, _logsig(smt[j]), 0.0), triu))
    yield
    its = [(j, h) for j in range(ns) for h in range(ML_HEADS)]
    hs = range(len(its))
    hls = [slice(h * ML_DH, (h + 1) * ML_DH) for _, h in its]
    f_t = [fcol[j][:, SM_F + h:SM_F + h + 1] for j, h in its]
    i_t = [sm[j][:, SM_I + h:SM_I + h + 1] for j, h in its]
    f_s = [frow[j][4 + h:5 + h, :] for j, h in its]
    i_s = [smt[j][h:h + 1, :] for j, h in its]
    m = [m_ref[j, h][0:1, 0:1] for j, h in its]
    q = [q_ref[j, rows, hls[i]] * (ML_DH ** -0.5) for i, (j, _) in enumerate(its)]
    k = [k_ref[j, rows, hls[i]] for i, (j, _) in enumerate(its)]
    v = [v_ref[j, rows, hls[i]] for i, (j, _) in enumerate(its)]
    qk0 = [_dot_nt(q[i], k[i]) for i in hs]
    gmax = [jnp.max(jnp.where(incl, i_s[i] - f_s[i], -jnp.inf), axis=1, keepdims=True) for i in hs]
    yield
    cm = [c_ref[j, h] for j, h in its]
    n8 = [n_ref[j, h] for j, h in its]
    qc = [_dot(q[i], cm[i]) for i in hs]
    qn = [_dot_nt(q[i], n8[i])[:, 0:1] for i in hs]
    yield
    mt = [jnp.maximum(f_t[i] + m[i], f_t[i] + gmax[i]) for i in hs]
    w_inter = [jnp.exp(f_t[i] + m[i] - mt[i]) for i in hs]
    d = [jnp.exp(jnp.where(incl, (f_t[i] - mt[i]) + (i_s[i] - f_s[i]), -jnp.inf)) for i in hs]
    qk = [qk0[i] * d[i] for i in hs]
    qkv = [_dot(qk[i], v[i]) for i in hs]
    yield
    m_new = [mt[i][c_ - 1:c_, :] for i in hs]
    f_last = [f_t[i][c_ - 1:c_, :] for i in hs]
    w_end = [jnp.exp(f_last[i] + m[i] - m_new[i]) for i in hs]
    kws = [k[i] * jnp.exp(f_last[i] - f_t[i] + i_t[i] - m_new[i]) for i in hs]
    kwv = [_dot_tn(kws[i], v[i]) for i in hs]
    yield
    for i, (j, h) in enumerate(its):
        c_ref[j, h] = w_end[i] * cm[i] + kwv[i]
        n_ref[j, h] = w_end[i] * n8[i] + jnp.broadcast_to(jnp.sum(kws[i], axis=0, keepdims=True),
                                                            (8, ML_DH))
        m_ref[j, h] = jnp.broadcast_to(m_new[i], (8, 128))
    den = [w_inter[i] * qn[i] + jnp.sum(qk[i], axis=1, keepdims=True) for i in hs]
    yield
    hm = [(w_inter[i] * qc[i] + qkv[i]) / jnp.maximum(jnp.abs(den[i]), jnp.exp(-mt[i])) for i in hs]
    mu = [jnp.mean(hm[i], axis=-1, keepdims=True) for i in hs]
    yield
    yc = [hm[i] - mu[i] for i in hs]
    var = [jnp.mean(yc[i] * yc[i], axis=-1, keepdims=True) for i in hs]
    yield
    for i, (j, _) in enumerate(its):
        y = yc[i] * lax.rsqrt(var[i] + EPS) * ng_ref[...]
        o_ref[j, rows, hls[i]] = (y * _sigmoid(mo_ref[j, rows, hls[i]])).astype(BF16)


def _rw_chunk(rows, ns, r_ref, k_ref, v_ref, wa_ref, gd_ref, mu_refs, par_refs, o_ref, s_ref, prev_refs):
    mur_ref, muk_ref, muv_ref, muwa_ref, mug_ref = mu_refs
    w0_ref, wup_ref, a0_ref, aup_ref, gup_ref, kkw_ref, kaw_ref, rk_ref, ng_ref = par_refs
    pr_s, pk_s, pv_s, pwa_s, pg_s = prev_refs
    c_ = CHUNK
    r_, c2 = _tri(c_, 128)
    s_idx = c2 % c_
    strict_pk = s_idx < r_
    incl_pk = s_idx <= r_
    eye_pk = jnp.where(s_idx == r_, 1.0, 0.0)
    rr, cc = _tri(c_)
    tril = jnp.where(rr >= cc, 1.0, 0.0).astype(BF16)
    lo_half = lax.broadcasted_iota(jnp.int32, (c_, 128), 1) < RW_HEAD
    r128, c128 = _tri(128)
    bdmask = (r128 // RW_HEAD) == (c128 // RW_HEAD)
    ones_bd = jnp.where(bdmask, 1.0, 0.0).astype(BF16)
    first_w = lax.broadcasted_iota(jnp.int32, (c_, RW_W), 0) == 0
    first_n = lax.broadcasted_iota(jnp.int32, (c_, 128), 0) == 0

    def mix(x_ref, prev_ref, mu_ref, j):
        x = x_ref[j, rows, :]
        first = first_w if x.shape[1] == RW_W else first_n
        prev = jnp.where(first, prev_ref[j], pltpu.roll(x, 1, 0))
        prev_ref[j] = x[c_ - 1:c_, :]
        return x + (prev - x) * mu_ref[...]

    def bd(y):
        yb = y.astype(BF16)
        zero = jnp.zeros_like(yb)
        return jnp.concatenate([jnp.where(lo_half, yb, zero), jnp.where(lo_half, zero, yb)], axis=0)

    def pk_mul(x, y):
        return jnp.dot(x.astype(BF16), bd(y), preferred_element_type=F32)

    def seg_all(xs):
        st = jnp.dot(jnp.concatenate(xs, axis=0).astype(BF16), ones_bd, preferred_element_type=F32)
        return [st[i * c_:(i + 1) * c_] for i in range(len(xs))]

    zr = [mix(r_ref, pr_s, mur_ref, j) for j in range(ns)]
    zv = [mix(v_ref, pv_s, muv_ref, j) for j in range(ns)]
    tok = [_rw_tokens(zr[j], mix(k_ref, pk_s, muk_ref, j), zv[j],
                      mix(wa_ref, pwa_s, muwa_ref, j), mix(gd_ref, pg_s, mug_ref, j),
                      w0_ref[...], wup_ref[...], a0_ref[...], aup_ref[...],
                      gup_ref[...], kkw_ref[...], kaw_ref[...]) for j in range(ns)]
    yield
    its = [(j, p) for j in range(ns) for p in range(RW_PAIRS)]
    prs = range(len(its))
    lns = [slice(p * 128, (p + 1) * 128) for _, p in its]
    lw = [tok[j][0][:, lns[i]] for i, (j, _) in enumerate(its)]
    a = [tok[j][1][:, lns[i]] for i, (j, _) in enumerate(its)]
    g = [tok[j][2][:, lns[i]] for i, (j, _) in enumerate(its)]
    kkp = [tok[j][3][:, lns[i]] for i, (j, _) in enumerate(its)]
    kp = [tok[j][4][:, lns[i]] for i, (j, _) in enumerate(its)]
    rp = [zr[j][:, lns[i]] for i, (j, _) in enumerate(its)]
    vp = [zv[j][:, lns[i]] for i, (j, _) in enumerate(its)]
    ssq = seg_all([x * x for x in kkp])
    cumf = [_dot2_left(tril, tok[j][0]) for j in range(ns)]
    yield
    cum = [cumf[j][:, lns[i]] for i, (j, _) in enumerate(its)]
    kkn = [kkp[p] * lax.rsqrt(ssq[p] + 1e-12) for p in prs]
    bp = [kkn[p] * a[p] for p in prs]
    cl = [cum[p][c_ - 1:c_, :] for p in prs]
    e_neg = [jnp.exp(-cum[p]) for p in prs]
    e_end = [jnp.exp(cl[p] - cum[p]) for p in prs]
    rh = [rp[p] * jnp.exp(cum[p]) for p in prs]
    kh = [kkn[p] * jnp.exp(cum[p] - lw[p]) for p in prs]
    kt = [kp[p] * e_neg[p] for p in prs]
    bt = [bp[p] * e_neg[p] for p in prs]
    kg = [kp[p] * e_end[p] for p in prs]
    bg = [bp[p] * e_end[p] for p in prs]
    kr = [jnp.concatenate([kh[p], rh[p]], axis=0).astype(BF16) for p in prs]
    ktb = [jnp.concatenate([bd(kt[p]), bd(bt[p])], axis=0) for p in prs]
    pkb = [lax.dot_general(kr[p], ktb[p], _NT, preferred_element_type=F32) for p in prs]
    yield
    ak = [jnp.where(strict_pk, pkb[p][0:c_, 0:128], 0.0) for p in prs]
    gk = [jnp.where(incl_pk, pkb[p][c_:2 * c_, 0:128], 0.0) for p in prs]
    gb = [jnp.where(incl_pk, pkb[p][c_:2 * c_, 128:256], 0.0) for p in prs]
    pw = [jnp.where(strict_pk, -pkb[p][0:c_, 128:256], 0.0) for p in prs]
    tm = [eye_pk + pw[p] for p in prs]
    agv = [pk_mul(jnp.concatenate([ak[p], gk[p]], axis=0), vp[p]) for p in prs]
    pw = [pk_mul(pw[p], pw[p]) for p in prs]
    yield
    for _ in range(4):
        x = [pk_mul(jnp.concatenate([pw[p], tm[p]], axis=0), pw[p]) for p in prs]
        tm = [tm[p] + x[p][c_:2 * c_] for p in prs]
        pw = [x[p][0:c_] for p in prs]
        yield
    tm = [tm[p] + pk_mul(tm[p], pw[p]) for p in prs]
    yield
    ku = [jnp.dot(tm[p].astype(BF16), jnp.concatenate([bd(kh[p]), bd(agv[p][0:c_])], axis=1),
                  preferred_element_type=F32) for p in prs]
    yield
    gku = [jnp.dot(gb[p].astype(BF16),
                   jnp.concatenate([bd(ku[p][:, 0:128]), bd(ku[p][:, 128:256])], axis=1),
                   preferred_element_type=F32) for p in prs]
    d = [_dot_tn(jnp.concatenate([vp[p], -ku[p][:, 128:256]], axis=0),
                 jnp.concatenate([kg[p], bg[p]], axis=0)) for p in prs]
    yield
    rhp = [rh[p] - gku[p][:, 0:128] for p in prs]
    y0 = [agv[p][c_:2 * c_] - gku[p][:, 128:256] for p in prs]
    s = [s_ref[j, p] for j, p in its]
    uy = [_dot_nt(jnp.concatenate([ku[p][:, 0:128], rhp[p]], axis=0), s[p]) for p in prs]
    yield
    ub = [_dot_tn(uy[p][0:c_], bg[p]) for p in prs]
    y = [uy[p][c_:2 * c_] + y0[p] for p in prs]
    mean = seg_all(y)
    yield
    for i, (j, p) in enumerate(its):
        s_ref[j, p] = s[i] * jnp.exp(cl[i]) + jnp.where(bdmask, d[i] - ub[i], 0.0)
    yc = [y[p] - mean[p] * (1.0 / RW_HEAD) for p in prs]
    var = seg_all([x * x for x in yc])
    bon = seg_all([rp[p] * kp[p] * rk_ref[:, lns[p]] for p in prs])
    yield
    for i, (j, _) in enumerate(its):
        yn = yc[i] * lax.rsqrt(var[i] * (1.0 / RW_HEAD) + EPS) * ng_ref[:, lns[i]]
        o_ref[j, rows, lns[i]] = ((yn + bon[i] * vp[i]) * g[i]).astype(BF16)


def _mixers_kernel(*refs, n_chunks, ns):
    (gq, gk, gv, gg, sm, mq, mk, mv, mo, rr, rk, rv, rwa, rgd) = refs[0:14]
    aup, ab, gn, brow, bcol, mng = refs[14:20]
    mu_refs = refs[20:25]
    par_refs = refs[25:34]
    o_gla, o_ml, o_rw, gla_out, c_out, n_out, m_out, rw_out = refs[34:42]
    s_gla, c_s, n_s, m_s, s_rw = refs[42:47]
    prev_refs = refs[47:52]
    t = pl.program_id(1)

    @pl.when(t == 0)
    def _():
        for ref in (s_gla, c_s, n_s, m_s, s_rw) + tuple(prev_refs):
            ref[...] = jnp.zeros_like(ref)

    def chunk(ci, carry):
        rows = pl.ds(pl.multiple_of(ci * CHUNK, CHUNK), CHUNK)
        _interleave(
            {"r": _rw_chunk(rows, ns, rr, rk, rv, rwa, rgd, mu_refs, par_refs, o_rw, s_rw, prev_refs),
             "g": _gla_chunk(rows, ns, gq, gk, gv, gg, sm, aup, ab, gn, o_gla, s_gla),
             "m": _mlstm_chunk(rows, ns, mq, mk, mv, mo, sm, brow, bcol, mng, o_ml, c_s, n_s, m_s)},
            "rmrg")
        return carry

    lax.fori_loop(0, n_chunks, chunk, 0)

    @pl.when(t == pl.num_programs(1) - 1)
    def _():
        gla_out[...] = s_gla[...]
        c_out[...] = c_s[...]
        for j in range(ns):
            for h in range(ML_HEADS):
                n_out[j, h:h + 1, :] = n_s[j, h][0:1, :]
                m_out[j, h:h + 1, :] = m_s[j, h][0:1, :]
            for p in range(RW_PAIRS):
                s = s_rw[j, p]
                rw_out[j, 2 * p] = s[0:RW_HEAD, 0:RW_HEAD]
                rw_out[j, 2 * p + 1] = s[RW_HEAD:, RW_HEAD:]


def _mixers_prompt(z, lp, b_, t_, tt):
    ns = _seq_per_step(b_)
    nc = tt // CHUNK
    z3 = z.reshape(b_, t_, NP)
    zblk = lambda w, off: pl.BlockSpec((ns, tt, w), lambda b, t, c=off // w: (b, t, c))
    const = lambda b, t: (0, 0)
    full = lambda a: pl.BlockSpec(a.shape, const)
    seq = lambda *shape: pl.BlockSpec((ns,) + shape, lambda b, t, n=len(shape): (b,) + (0,) * n)
    params = [lp["aup_pad"], lp["ab"], lp["gla_ng"], lp["brow"], lp["bcol"], lp["ml_ng"], *lp["mus"], *lp["rw"]]
    bf = jax.ShapeDtypeStruct((b_, t_, D_MODEL), BF16)
    outs = pl.pallas_call(
        functools.partial(_mixers_kernel, n_chunks=nc, ns=ns),
        grid=(b_ // ns, t_ // tt),
        in_specs=[zblk(GLA_QK, OFF_GQ), zblk(GLA_QK, OFF_GK), zblk(GLA_V, OFF_GV), zblk(GLA_V, OFF_GG),
                  zblk(128, OFF_SM),
                  zblk(ML_W, OFF_MQ), zblk(ML_W, OFF_MK), zblk(ML_W, OFF_MV), zblk(ML_W, OFF_MO),
                  zblk(RW_W, OFF_RW), zblk(RW_W, OFF_RW + RW_W), zblk(RW_W, OFF_RW + 2 * RW_W),
                  zblk(128, OFF_RW + 3 * RW_W), zblk(128, OFF_RW + 3 * RW_W + 128)]
                 + [full(a) for a in params],
        out_specs=[pl.BlockSpec((ns, tt, D_MODEL), lambda b, t: (b, t, 0))] * 3
                  + [seq(GLA_HEADS, GLA_DK, GLA_DV), seq(ML_HEADS, ML_DH, ML_DH), seq(ML_HEADS, ML_DH),
                     seq(ML_HEADS, 128), seq(RW_HEADS, RW_HEAD, RW_HEAD)],
        out_shape=[bf, bf, bf,
                   jax.ShapeDtypeStruct((b_, GLA_HEADS, GLA_DK, GLA_DV), F32),
                   jax.ShapeDtypeStruct((b_, ML_HEADS, ML_DH, ML_DH), F32),
                   jax.ShapeDtypeStruct((b_, ML_HEADS, ML_DH), F32),
                   jax.ShapeDtypeStruct((b_, ML_HEADS, 128), F32),
                   jax.ShapeDtypeStruct((b_, RW_HEADS, RW_HEAD, RW_HEAD), F32)],
        scratch_shapes=[pltpu.VMEM((ns, GLA_HEADS, GLA_DK, GLA_DV), F32),
                        pltpu.VMEM((ns, ML_HEADS, ML_DH, ML_DH), F32),
                        pltpu.VMEM((ns, ML_HEADS, 8, ML_DH), F32),
                        pltpu.VMEM((ns, ML_HEADS, 8, 128), F32),
                        pltpu.VMEM((ns, RW_PAIRS, 128, 128), F32),
                        pltpu.VMEM((ns, 1, RW_W), F32), pltpu.VMEM((ns, 1, RW_W), F32),
                        pltpu.VMEM((ns, 1, RW_W), F32), pltpu.VMEM((ns, 1, 128), F32),
                        pltpu.VMEM((ns, 1, 128), F32)],
        compiler_params=pltpu.CompilerParams(
            dimension_semantics=("parallel", "arbitrary"), vmem_limit_bytes=MIXER_VMEM_LIMIT),
        name="mixers_prompt",
    )(*([z3] * 14), *params)
    o_gla, o_ml, o_rw, gla_new, c_new, n_new, m_new, rw_new = outs
    flat = lambda o: o.reshape(b_ * t_, D_MODEL)
    return flat(o_gla), flat(o_ml), flat(o_rw), (gla_new, c_new, n_new, m_new[:, :, 0], rw_new)


def _sample_prep_kernel(sm_ref, r_ref, k_ref, v_ref, wa_ref, gd_ref, m_ref, shift_ref,
                        aup_ref, ab_ref, ib_ref, fb_ref,
                        mur_ref, muk_ref, muv_ref, muwa_ref, mug_ref,
                        w0_ref, wup_ref, a0_ref, aup2_ref, gup_ref, kkw_ref, kaw_ref,
                        gla_la_ref, mnew_ref, wend_ref, ws_ref,
                        rr_ref, rk_ref, rv_ref, rg_ref, kkt_ref, bt_ref, kt_ref, rt_ref, wt_ref, vt_ref):
    sm = sm_ref[...]
    gla_la_ref[...] = _logsig(_dot(sm, aup_ref[...]) + ab_ref[...]) * (1.0 / GLA_TAU)
    i_pre = sm[:, SM_I:SM_I + ML_HEADS] + ib_ref[...]
    logf = _logsig(sm[:, SM_F:SM_F + ML_HEADS] + fb_ref[...])
    m = m_ref[0]
    m_new = jnp.maximum(logf + m, i_pre)
    mnew_ref[...] = m_new
    wend_ref[...] = jnp.exp(logf + m - m_new)
    ws_ref[...] = jnp.exp(i_pre - m_new)
    mixf = lambda x, off, mu: x[...] + (shift_ref[0, :, off:off + x.shape[1]] - x[...]) * mu[...]
    zr = mixf(r_ref, 0, mur_ref)
    zk = mixf(k_ref, RW_W, muk_ref)
    zv = mixf(v_ref, 2 * RW_W, muv_ref)
    zwa = mixf(wa_ref, 3 * RW_W, muwa_ref)
    zg = mixf(gd_ref, 3 * RW_W + 128, mug_ref)
    lw, a, g, kk0, kmod = _rw_tokens(zr, zk, zv, zwa, zg, w0_ref[...], wup_ref[...], a0_ref[...],
                                     aup2_ref[...], gup_ref[...], kkw_ref[...], kaw_ref[...])
    ones_bd = _ones_bd()
    for p in range(RW_PAIRS):
        ln = slice(p * 128, (p + 1) * 128)
        kkp = kk0[:, ln]
        kkn = kkp * lax.rsqrt(_seg(kkp * kkp, ones_bd) + 1e-12)
        kkt_ref[ln, :] = kkn.T
        bt_ref[ln, :] = (kkn * a[:, ln]).T
    rr_ref[...] = zr
    rk_ref[...] = kmod
    rv_ref[...] = zv
    rg_ref[...] = g
    kt_ref[...] = kmod.T
    rt_ref[...] = zr.T
    wt_ref[...] = jnp.exp(lw).T
    vt_ref[...] = zv.T


STEP_BB = 8


def _row_select(rows_list, rid):
    acc = rows_list[0]
    for b in range(1, len(rows_list)):
        acc = jnp.where(rid == b, rows_list[b], acc)
    return acc


def _gla_step_kernel(s_ref, q_ref, k_ref, la_ref, v_ref, *rest):
    sn_ref, o_ref = rest[-2:]
    rid_k = lax.broadcasted_iota(jnp.int32, (STEP_BB, GLA_DK), 0)
    rid_v = lax.broadcasted_iota(jnp.int32, (STEP_BB, GLA_DV), 0)
    bs = range(STEP_BB)
    onehot = [jnp.where(rid_v == b, 1.0, 0.0).astype(BF16) for b in bs]
    for h in range(GLA_HEADS):
        kl = slice(h * GLA_DK, (h + 1) * GLA_DK)
        vl = slice(h * GLA_DV, (h + 1) * GLA_DV)
        qt = q_ref[:, kl] * (GLA_DK ** -0.5)
        kt = k_ref[:, kl]
        lt = la_ref[:, kl]
        vt = v_ref[:, vl]
        dec = [jnp.exp(_dot2_tn(lt, onehot[b])) for b in bs]
        kv = [_dot3(jnp.where(rid_k == b, kt, 0.0), vt, _TN) for b in bs]
        sn = [dec[b] * s_ref[0, b, h] + kv[b] for b in bs]
        for b in bs:
            sn_ref[0, b, h] = sn[b]
        o_ref[:, vl] = _row_select([_dot3(qt, sn[b]) for b in bs], rid_v)


def _mlstm_step_kernel(c_ref, n_ref, q_ref, k_ref, v_ref, wend_ref, ws_ref, mnew_ref, *rest):
    cn_ref, nn_ref, h_ref = rest[-3:]
    rid = lax.broadcasted_iota(jnp.int32, (STEP_BB, ML_DH), 0)
    for h in range(ML_HEADS):
        hl = slice(h * ML_DH, (h + 1) * ML_DH)
        qt = q_ref[:, hl] * (ML_DH ** -0.5)
        kt = k_ref[:, hl]
        vt = v_ref[:, hl]
        wend = wend_ref[:, h:h + 1]
        ws = ws_ref[:, h:h + 1]
        nn = wend * n_ref[0, :, hl] + ws * kt
        nn_ref[0, :, hl] = nn
        den = jnp.sum(qt * nn, axis=1, keepdims=True)
        kws = kt * ws
        nums = []
        for half in range(2):
            bs = range(half * 4, half * 4 + 4)
            kwv = [_dot3(jnp.where(rid == b, kws, 0.0), vt, _TN) for b in bs]
            cn = [wend_ref[b:b + 1, h:h + 1] * c_ref[0, b, h] + kwv[i] for i, b in enumerate(bs)]
            for i, b in enumerate(bs):
                cn_ref[0, b, h] = cn[i]
            nums += [_dot3(qt, cn[i]) for i in range(4)]
        num = _row_select(nums, rid)
        h_ref[:, hl] = num / jnp.maximum(jnp.abs(den), jnp.exp(-mnew_ref[:, h:h + 1]))


RW_STEP_HB = 2


def _rw_step_kernel(s_ref, kk_ref, b_ref, k_ref, r_ref, w_ref, v_ref, *rest):
    sn_ref, y_ref = rest[-2:]
    n = RW_HEAD
    for hh in range(RW_STEP_HB):
        fl = slice(hh * n, (hh + 1) * n)
        nkk, bb, kt, rt, wt = -kk_ref[fl, :], b_ref[fl, :], k_ref[fl, :], r_ref[fl, :], w_ref[fl, :]

        def row(i, carry):
            s = s_ref[0, hh, i]
            sa = jnp.sum(s * nkk, axis=0, keepdims=True)
            sn = s * wt + sa * bb + v_ref[pl.ds(hh * n + i, 1), :] * kt
            sn_ref[0, hh, i] = sn
            y_ref[pl.ds(hh * n + i, 1), :] = jnp.sum(sn * rt, axis=0, keepdims=True)
            return carry

        lax.fori_loop(0, n, row, 0, unroll=4)


def _rw_state_call(l, state_t, vecs_t, prev_out):
    nb = state_t.shape[-1]
    st_spec = pl.BlockSpec((1, RW_STEP_HB, RW_HEAD, RW_HEAD, nb), lambda h: (l, h, 0, 0, 0))
    vec_spec = pl.BlockSpec((RW_STEP_HB * RW_HEAD, nb), lambda h: (h, 0))
    args = [state_t] + list(vecs_t)
    in_specs = [st_spec] + [vec_spec] * len(vecs_t)
    aliases = {}
    if prev_out is not None:
        aliases[len(args)] = 0
        in_specs.append(pl.BlockSpec(memory_space=pl.ANY))
        args.append(prev_out)
    return pl.pallas_call(
        _rw_step_kernel,
        grid=(RW_HEADS // RW_STEP_HB,),
        in_specs=in_specs,
        out_specs=[st_spec, vec_spec],
        out_shape=[jax.ShapeDtypeStruct(state_t.shape, F32), jax.ShapeDtypeStruct((RW_W, nb), F32)],
        input_output_aliases=aliases,
        compiler_params=pltpu.CompilerParams(
            dimension_semantics=("parallel",), vmem_limit_bytes=VMEM_LIMIT),
        name="rwkv_step",
    )(*args)


def _sample_post_kernel(og_ref, gg_ref, gn_ref, hm_ref, mo_ref, mn_ref,
                        yt_ref, r_ref, k_ref, v_ref, g_ref, rk_ref, ng_ref,
                        ogla_ref, oml_ref, orw_ref):
    for h in range(GLA_HEADS):
        vl = slice(h * GLA_DV, (h + 1) * GLA_DV)
        o = og_ref[:, vl]
        on = o * lax.rsqrt(jnp.mean(o * o, axis=-1, keepdims=True) + EPS) * gn_ref[...]
        gg = gg_ref[:, vl]
        ogla_ref[:, vl] = (on * (gg * _sigmoid(gg))).astype(BF16)
    for h in range(ML_HEADS):
        hl = slice(h * ML_DH, (h + 1) * ML_DH)
        hm = hm_ref[:, hl]
        y = hm - jnp.mean(hm, axis=-1, keepdims=True)
        y = y * lax.rsqrt(jnp.mean(y * y, axis=-1, keepdims=True) + EPS) * mn_ref[...]
        oml_ref[:, hl] = (y * _sigmoid(mo_ref[:, hl])).astype(BF16)
    ones_bd = _ones_bd()
    for p in range(RW_PAIRS):
        ln = slice(p * 128, (p + 1) * 128)
        y = yt_ref[ln, :].T
        orw_ref[:, ln] = _rw_epilogue(y, r_ref[:, ln], k_ref[:, ln], v_ref[:, ln],
                                      g_ref[:, ln], rk_ref[:, ln], ng_ref[:, ln], ones_bd)


def _full_call(kernel_fn, args, out_shapes, name):
    nd = lambda a: (lambda i: (0,) * len(a.shape))
    in_specs, arrays = [], []
    for a in args:
        if isinstance(a, tuple):
            arr, blk, idx = a
            in_specs.append(pl.BlockSpec(blk, lambda i, idx=idx: idx))
            arrays.append(arr)
        else:
            in_specs.append(pl.BlockSpec(a.shape, nd(a)))
            arrays.append(a)
    return pl.pallas_call(
        kernel_fn,
        grid=(1,),
        in_specs=in_specs,
        out_specs=[pl.BlockSpec(s.shape, nd(s)) for s in out_shapes],
        out_shape=out_shapes,
        compiler_params=pltpu.CompilerParams(vmem_limit_bytes=VMEM_LIMIT),
        name=name,
    )(*arrays)


def _state_call(kernel_fn, l, states, vecs, prev_outs, out_width, name):
    nb = states[0].shape[1]
    st_spec = lambda a: pl.BlockSpec((1, STEP_BB) + tuple(a.shape[2:]),
                                     lambda i, n=a.ndim: (l, i) + (0,) * (n - 2))
    in_specs = [st_spec(a) for a in states]
    in_specs += [pl.BlockSpec((STEP_BB, w), lambda i, c=c: (i, c)) for _, w, c in vecs]
    args = list(states) + [a for a, _, _ in vecs]
    aliases = {}
    for j, po in enumerate(prev_outs or ()):
        aliases[len(args)] = j
        in_specs.append(pl.BlockSpec(memory_space=pl.ANY))
        args.append(po)
    out_shape = [jax.ShapeDtypeStruct(a.shape, F32) for a in states]
    out_shape.append(jax.ShapeDtypeStruct((nb, out_width), F32))
    out_specs = [st_spec(a) for a in states] + [pl.BlockSpec((STEP_BB, out_width), lambda i: (i, 0))]
    return pl.pallas_call(
        kernel_fn,
        grid=(nb // STEP_BB,),
        in_specs=in_specs,
        out_specs=out_specs,
        out_shape=out_shape,
        input_output_aliases=aliases,
        compiler_params=pltpu.CompilerParams(
            dimension_semantics=("parallel",), vmem_limit_bytes=VMEM_LIMIT),
        name=name,
    )(*args)


def _prep_layer_params(l, p):
    w = p["w_in"][l]
    cols = ((0, 2048), (2064, 3088), (3088, 6160), (6168, 7192), (7192, 13592), (2048, 2064), (6160, 6168))
    w_in_p = jnp.concatenate([w[:, a:b].astype(BF16) for a, b in cols]
                             + [jnp.zeros((D_MODEL, NP - 13592), BF16)], axis=1)
    aup_pad = jnp.zeros((128, GLA_QK), F32).at[0:GLA_RANK].set(p["gla_a_up"][l])
    brow = (jnp.zeros((1, 128), F32).at[0, SM_I:SM_I + 4].set(p["ml_i_bias"][l])
            .at[0, SM_F:SM_F + 4].set(p["ml_f_bias"][l]))
    bcol = jnp.broadcast_to(jnp.concatenate([p["ml_i_bias"][l], p["ml_f_bias"][l]])[:, None], (8, CHUNK))
    mu = p["rw_mu"][l][None, :]
    mus = (mu[:, 0:1024], mu[:, 1024:2048], mu[:, 2048:3072], mu[:, 3072:3200], mu[:, 3200:3328])
    wup_pad = jnp.zeros((128, RW_W), F32).at[0:64].set(p["rw_w_up"][l])
    aup2_pad = jnp.zeros((128, RW_W), F32).at[64:128].set(p["rw_a_up"][l])
    row = lambda a: a.reshape(1, -1)
    rw_params = (row(p["rw_w0"][l]), wup_pad, row(p["rw_a0"][l]), aup2_pad, p["rw_g_up"][l],
                 row(p["rw_k_k"][l]), row(p["rw_k_a"][l]), row(p["rw_r_k"][l]), row(p["rw_norm_g"][l]))
    return dict(
        norm1_g=row(p["norm1_g"][l]), w_in=w_in_p, aup_pad=aup_pad, ab=row(p["gla_a_bias"][l]),
        gla_ng=row(p["gla_norm_g"][l]), brow=brow, bcol=bcol, ib=row(p["ml_i_bias"][l]),
        fb=row(p["ml_f_bias"][l]), ml_ng=row(p["ml_norm_g"][l]), mus=mus, rw=rw_params,
        wb=p["w_branch"][l].astype(BF16), wo=p["w_out"][l].astype(BF16), norm2_g=row(p["norm2_g"][l]),
        w1=p["w_ff1"][l].astype(BF16), w2=p["w_ff2"][l].astype(BF16))


def _prompt_layer(x, lp, fg, final, b_, t_):
    rows = x.shape[0]
    tm = min(512, rows)
    tm_big = min(1024, rows)
    z = _proj_in(x, lp["norm1_g"], lp["w_in"], tm_big)
    o_gla, o_ml, o_rw, new_states = _mixers_prompt(z, lp, b_, t_, min(MIXER_TT, t_))
    x = _merge(o_gla, o_ml, o_rw, z, x, lp["wb"], lp["wo"], tm)
    x = _ffn(x, lp["norm2_g"], lp["w1"], lp["w2"], fg, final, tm_big)
    shift = z.reshape(b_, t_, NP)[:, t_ - 1, OFF_RW:OFF_RW + RW_COLS]
    return x, new_states + (shift,)


def _sample_layer(l, x, states, prev, lp, fg, final):
    st_gla, st_c, st_n, st_m, st_rw, st_shift = states
    nb = x.shape[0]
    z = _proj_in(x, lp["norm1_g"], lp["w_in"], nb)
    zw = lambda off, w: (z, (nb, w), (0, off // w))
    w0, wup, a0, aup2, gup, kkw, kaw, rk, ng = lp["rw"]
    sds = lambda *s: jax.ShapeDtypeStruct(s, F32)
    prep_args = [zw(OFF_SM, 128), zw(OFF_RW, RW_W), zw(OFF_RW + RW_W, RW_W), zw(OFF_RW + 2 * RW_W, RW_W),
                 zw(OFF_RW + 3 * RW_W, 128), zw(OFF_RW + 3 * RW_W + 128, 128),
                 (st_m, (1, nb, ML_HEADS), (l, 0, 0)), (st_shift, (1, nb, RW_COLS), (l, 0, 0)),
                 lp["aup_pad"], lp["ab"], lp["ib"], lp["fb"], *lp["mus"],
                 w0, wup, a0, aup2, gup, kkw, kaw]
    (la, m_new, wend, ws, rr, rkm, rv, rg, kkt, bt, kt, rt, wt, vt) = _full_call(
        _sample_prep_kernel, prep_args,
        [sds(nb, GLA_QK), sds(nb, 4), sds(nb, 4), sds(nb, 4)] + [sds(nb, RW_W)] * 4 + [sds(RW_W, nb)] * 6,
        "sample_prep")

    pv = lambda *idx: None if prev is None else [prev[i] for i in idx]
    gla_all, o_g = _state_call(
        _gla_step_kernel, l, [st_gla],
        [(z, GLA_QK, OFF_GQ // GLA_QK), (z, GLA_QK, OFF_GK // GLA_QK), (la, GLA_QK, 0),
         (z, GLA_V, OFF_GV // GLA_V)], pv(0), GLA_V, "gla_step")
    c_all, n_all, hm = _state_call(
        _mlstm_step_kernel, l, [st_c, st_n],
        [(z, ML_W, OFF_MQ // ML_W), (z, ML_W, OFF_MK // ML_W), (z, ML_W, OFF_MV // ML_W),
         (wend, 4, 0), (ws, 4, 0), (m_new, 4, 0)], pv(1, 2), ML_W, "mlstm_step")
    rw_all, y = _rw_state_call(l, st_rw, (kkt, bt, kt, rt, wt, vt), None if prev is None else prev[3])

    bf = lambda: jax.ShapeDtypeStruct((nb, D_MODEL), BF16)
    o_gla, o_ml, o_rw = _full_call(
        _sample_post_kernel,
        [o_g, zw(OFF_GG, GLA_V), lp["gla_ng"], hm, zw(OFF_MO, ML_W), lp["ml_ng"],
         y, rr, rkm, rv, rg, rk, ng],
        [bf(), bf(), bf()], "sample_post")
    x = _merge(o_gla, o_ml, o_rw, z, x, lp["wb"], lp["wo"], nb)
    x = _ffn(x, lp["norm2_g"], lp["w1"], lp["w2"], fg, final, nb)
    return x, (gla_all, c_all, n_all, rw_all), (m_new, z[:, OFF_RW:OFF_RW + RW_COLS])


def kernel(x_prompt, x_sample, state_gla, state_mlstm_c, state_mlstm_n, state_mlstm_m, state_rwkv, state_shift, norm1_g, w_in, gla_a_up, gla_a_bias, gla_norm_g, ml_i_bias, ml_f_bias, ml_norm_g, rw_mu, rw_w0, rw_w_up, rw_a0, rw_a_up, rw_g_up, rw_k_k, rw_k_a, rw_r_k, rw_norm_g, w_branch, w_out, norm2_g, w_ff1, w_ff2, final_g):
    p = dict(norm1_g=norm1_g, w_in=w_in, gla_a_up=gla_a_up, gla_a_bias=gla_a_bias, gla_norm_g=gla_norm_g,
             ml_i_bias=ml_i_bias, ml_f_bias=ml_f_bias, ml_norm_g=ml_norm_g, rw_mu=rw_mu, rw_w0=rw_w0,
             rw_w_up=rw_w_up, rw_a0=rw_a0, rw_a_up=rw_a_up, rw_g_up=rw_g_up, rw_k_k=rw_k_k, rw_k_a=rw_k_a,
             rw_r_k=rw_r_k, rw_norm_g=rw_norm_g, w_branch=w_branch, w_out=w_out, norm2_g=norm2_g,
             w_ff1=w_ff1, w_ff2=w_ff2)
    depth = w_in.shape[0]
    b_, t_, _ = x_prompt.shape
    nb = x_sample.shape[0]
    fg = final_g.reshape(1, D_MODEL)
    xp = x_prompt.reshape(b_ * t_, D_MODEL)
    xs = x_sample.reshape(nb, D_MODEL)
    s_in = (state_gla, state_mlstm_c, state_mlstm_n.reshape(depth, nb, ML_W), state_mlstm_m,
            jnp.transpose(state_rwkv, (0, 2, 3, 4, 1)), state_shift)
    p_states, s_big, s_small = [], None, []
    for l in range(depth):
        lp = _prep_layer_params(l, p)
        final = l == depth - 1
        xp, stp = _prompt_layer(xp, lp, fg, final, b_, t_)
        xs, s_big, small = _sample_layer(l, xs, s_in, s_big, lp, fg, final)
        p_states.append(stp)
        s_small.append(small)
    ps = [jnp.stack([s[i] for s in p_states]) for i in range(6)]
    gla_s, mlc_s, mln_s, rw_s = s_big
    rw_s = jnp.transpose(rw_s, (0, 4, 1, 2, 3))
    mln_s = mln_s.reshape(depth, nb, ML_HEADS, ML_DH)
    mlm_s = jnp.stack([s[0] for s in s_small])
    shift_s = jnp.stack([s[1] for s in s_small])
    return (xp.reshape(b_, t_, D_MODEL), xs.reshape(nb, 1, D_MODEL),
            ps[0], gla_s, ps[1], mlc_s, ps[2], mln_s, ps[3], mlm_s, ps[4], rw_s, ps[5], shift_s)
```

```python
import functools
import math

import jax
import jax.numpy as jnp
from jax import lax
from jax.experimental import pallas as pl
from jax.experimental.pallas import tpu as pltpu

F32 = jnp.float32
BF16 = jnp.bfloat16

D_MODEL = 1024
GLA_HEADS, GLA_DK, GLA_DV = 4, 128, 256
GLA_QK, GLA_V, GLA_RANK, GLA_TAU = 512, 1024, 16, 16.0
ML_HEADS, ML_DH, ML_W = 4, 256, 1024
RW_HEAD, RW_W, RW_HEADS = 64, 1024, 16
RW_PAIRS = RW_HEADS // 2
RW_COLS = 3 * RW_W + 64 + 64 + 128
D_FF = 4 * D_MODEL
EPS = 1e-6
CHUNK = 64

OFF_GQ, OFF_GK, OFF_GV, OFF_GG = 0, 512, 1024, 2048
OFF_MQ, OFF_MK, OFF_MV, OFF_MO = 3072, 4096, 5120, 6144
OFF_GATE = 7168
OFF_RW = 10240
OFF_SM = 13568
NP = 13824
SM_I, SM_F = 16, 20

VMEM_LIMIT = 48 * 1024 * 1024
MIXER_VMEM_LIMIT = 56 * 1024 * 1024
MIXER_TT = 128


def _dot(a, b):
    return jnp.dot(a.astype(BF16), b.astype(BF16), preferred_element_type=F32)


def _dot_nt(a, b):
    return lax.dot_general(a.astype(BF16), b.astype(BF16), (((1,), (1,)), ((), ())),
                           preferred_element_type=F32)


def _dot_tn(a, b):
    return lax.dot_general(a.astype(BF16), b.astype(BF16), (((0,), (0,)), ((), ())),
                           preferred_element_type=F32)


def _split(x):
    hi = x.astype(BF16)
    lo = (x - hi.astype(F32)).astype(BF16)
    return hi, lo


def _dot2(x, w_exact):
    hi, lo = _split(x)
    return (jnp.dot(hi, w_exact, preferred_element_type=F32)
            + jnp.dot(lo, w_exact, preferred_element_type=F32))


def _dot2_tn(x, w_exact):
    hi, lo = _split(x)
    dn = (((0,), (0,)), ((), ()))
    return (lax.dot_general(hi, w_exact, dn, preferred_element_type=F32)
            + lax.dot_general(lo, w_exact, dn, preferred_element_type=F32))


def _dot2_left(w_exact, x):
    hi, lo = _split(x)
    return (jnp.dot(w_exact, hi, preferred_element_type=F32)
            + jnp.dot(w_exact, lo, preferred_element_type=F32))


_NN = (((1,), (0,)), ((), ()))
_TN = (((0,), (0,)), ((), ()))
_NT = (((1,), (1,)), ((), ()))


def _dot3(a, b, dn=_NN):
    ah, al = _split(a)
    bh, bl = _split(b)
    f = lambda x, y: lax.dot_general(x, y, dn, preferred_element_type=F32)
    return f(ah, bh) + f(al, bh) + f(ah, bl)


def _logsig(x):
    return jnp.minimum(x, 0.0) - jnp.log(1.0 + jnp.exp(-jnp.abs(x)))


def _sigmoid(x):
    return jax.nn.sigmoid(x)


def _rms(x, g):
    return x * lax.rsqrt(jnp.mean(x * x, axis=-1, keepdims=True) + EPS) * g


def _tri(n, m=None):
    m = n if m is None else m
    r = lax.broadcasted_iota(jnp.int32, (n, m), 0)
    c = lax.broadcasted_iota(jnp.int32, (n, m), 1)
    return r, c


def _proj_in_kernel(x_ref, g_ref, w_ref, o_ref, h_ref):
    @pl.when(pl.program_id(1) == 0)
    def _():
        h_ref[...] = _rms(x_ref[...], g_ref[...]).astype(BF16)

    o_ref[...] = jnp.dot(h_ref[...], w_ref[...], preferred_element_type=F32)


def _proj_in(x, g, w, tm, tn=2304):
    m = x.shape[0]
    return pl.pallas_call(
        _proj_in_kernel,
        grid=(m // tm, NP // tn),
        in_specs=[pl.BlockSpec((tm, D_MODEL), lambda i, j: (i, 0)),
                  pl.BlockSpec((1, D_MODEL), lambda i, j: (0, 0)),
                  pl.BlockSpec((D_MODEL, tn), lambda i, j: (0, j))],
        out_specs=pl.BlockSpec((tm, tn), lambda i, j: (i, j)),
        out_shape=jax.ShapeDtypeStruct((m, NP), F32),
        scratch_shapes=[pltpu.VMEM((tm, D_MODEL), BF16)],
        compiler_params=pltpu.CompilerParams(
            dimension_semantics=("parallel", "arbitrary"), vmem_limit_bytes=VMEM_LIMIT),
        name="proj_in",
    )(x, g, w)


def _merge_kernel(og_ref, om_ref, or_ref, g0_ref, g1_ref, g2_ref, x_ref, wb_ref, wo_ref, o_ref):
    acc = jnp.dot(og_ref[...], wb_ref[0], preferred_element_type=F32) * _sigmoid(g0_ref[...])
    acc += jnp.dot(om_ref[...], wb_ref[1], preferred_element_type=F32) * _sigmoid(g1_ref[...])
    acc += jnp.dot(or_ref[...], wb_ref[2], preferred_element_type=F32) * _sigmoid(g2_ref[...])
    o_ref[...] = x_ref[...] + jnp.dot(acc.astype(BF16), wo_ref[...], preferred_element_type=F32)


def _merge(o_gla, o_ml, o_rw, z, x, wb, wo, tm):
    m = x.shape[0]
    row = lambda i: (i, 0)
    gate_blk = OFF_GATE // D_MODEL
    return pl.pallas_call(
        _merge_kernel,
        grid=(m // tm,),
        in_specs=[pl.BlockSpec((tm, D_MODEL), row),
                  pl.BlockSpec((tm, D_MODEL), row),
                  pl.BlockSpec((tm, D_MODEL), row),
                  pl.BlockSpec((tm, D_MODEL), lambda i: (i, gate_blk)),
                  pl.BlockSpec((tm, D_MODEL), lambda i: (i, gate_blk + 1)),
                  pl.BlockSpec((tm, D_MODEL), lambda i: (i, gate_blk + 2)),
                  pl.BlockSpec((tm, D_MODEL), row),
                  pl.BlockSpec((3, D_MODEL, D_MODEL), lambda i: (0, 0, 0)),
                  pl.BlockSpec((D_MODEL, D_MODEL), lambda i: (0, 0))],
        out_specs=pl.BlockSpec((tm, D_MODEL), row),
        out_shape=jax.ShapeDtypeStruct((m, D_MODEL), F32),
        compiler_params=pltpu.CompilerParams(
            dimension_semantics=("parallel",), vmem_limit_bytes=VMEM_LIMIT),
        name="merge",
    )(o_gla, o_ml, o_rw, z, z, z, x, wb, wo)


def _ffn_kernel(x_ref, g_ref, w1_ref, w2_ref, fg_ref, o_ref, h_ref, acc_ref, *, final):
    f = pl.program_id(1)

    @pl.when(f == 0)
    def _():
        h_ref[...] = _rms(x_ref[...], g_ref[...]).astype(BF16)
        acc_ref[...] = jnp.zeros_like(acc_ref)

    a = jnp.dot(h_ref[...], w1_ref[...], preferred_element_type=F32)
    a = jnp.square(jnp.maximum(a, 0.0))
    acc_ref[...] += jnp.dot(a.astype(BF16), w2_ref[...], preferred_element_type=F32)

    @pl.when(f == pl.num_programs(1) - 1)
    def _():
        y = x_ref[...] + acc_ref[...]
        if final:
            y = _rms(y, fg_ref[...])
        o_ref[...] = y


def _ffn(x, g, w1, w2, fg, final, tm, tf=1024):
    m = x.shape[0]
    return pl.pallas_call(
        functools.partial(_ffn_kernel, final=final),
        grid=(m // tm, D_FF // tf),
        in_specs=[pl.BlockSpec((tm, D_MODEL), lambda i, f: (i, 0)),
                  pl.BlockSpec((1, D_MODEL), lambda i, f: (0, 0)),
                  pl.BlockSpec((D_MODEL, tf), lambda i, f: (0, f)),
                  pl.BlockSpec((tf, D_MODEL), lambda i, f: (f, 0)),
                  pl.BlockSpec((1, D_MODEL), lambda i, f: (0, 0))],
        out_specs=pl.BlockSpec((tm, D_MODEL), lambda i, f: (i, 0)),
        out_shape=jax.ShapeDtypeStruct((m, D_MODEL), F32),
        scratch_shapes=[pltpu.VMEM((tm, D_MODEL), BF16), pltpu.VMEM((tm, D_MODEL), F32)],
        compiler_params=pltpu.CompilerParams(
            dimension_semantics=("parallel", "arbitrary"), vmem_limit_bytes=VMEM_LIMIT),
        name="ffn",
    )(x, g, w1, w2, fg)


def _rw_tokens(zr, zk, zv, zwa, zg, w0, wup, a0, aup, gup, kkw, kaw):
    u = w0 + _dot(jnp.tanh(zwa), wup)
    lw = _sigmoid(u) * (-math.exp(-0.5))
    a = _sigmoid(a0 + _dot(zwa, aup))
    g = _dot(_sigmoid(zg), gup)
    kk0 = zk * kkw
    kmod = zk * (1.0 + (a - 1.0) * kaw)
    return lw, a, g, kk0, kmod


def _ones_bd():
    r, c = _tri(128)
    return jnp.where((r // RW_HEAD) == (c // RW_HEAD), 1.0, 0.0).astype(BF16)


def _seg(x, ones_bd):
    return _dot2(x, ones_bd)


def _rw_epilogue(y, r, kmod, v, g, rk, ng, ones_bd):
    mean = _seg(y, ones_bd) * (1.0 / RW_HEAD)
    yc = y - mean
    var = _seg(yc * yc, ones_bd) * (1.0 / RW_HEAD)
    yn = yc * lax.rsqrt(var + EPS) * ng
    bonus = _seg(r * kmod * rk, ones_bd) * v
    return ((yn + bonus) * g).astype(BF16)


def _seq_per_step(b_):
    return 2 if b_ % 2 == 0 else 1


def _interleave(gens, pattern):
    live = dict(gens)
    while live:
        for key in pattern:
            g = live.get(key)
            if g is None:
                continue
            try:
                next(g)
            except StopIteration:
                del live[key]


def _gla_chunk(rows, ns, q_ref, k_ref, v_ref, gg_ref, sm_ref, aup_ref, ab_ref, gn_ref, o_ref, s_ref):
    c_ = CHUNK
    r, c = _tri(c_)
    incl = r >= c
    tril = jnp.where(incl, 1.0, 0.0).astype(BF16)
    ones_c = jnp.ones((c_, GLA_DK), BF16)
    tokw = []
    for j in range(ns):
        x = _dot(sm_ref[j, rows, :], aup_ref[...]) + ab_ref[...]
        la = _logsig(x) * (1.0 / GLA_TAU)
        b = _dot2_left(tril, la)
        dec = jnp.exp(_dot2_tn(la, ones_c))
        q = q_ref[j, rows, :] * (GLA_DK ** -0.5)
        k = k_ref[j, rows, :]
        tokw.append((q * jnp.exp(b), k * jnp.exp(-b), k * jnp.exp(b[c_ - 1:c_, :] - b), dec))
    yield
    its = [(j, h) for j in range(ns) for h in range(GLA_HEADS)]
    hs = range(len(its))
    kls = [slice(h * GLA_DK, (h + 1) * GLA_DK) for _, h in its]
    vls = [slice(h * GLA_DV, (h + 1) * GLA_DV) for _, h in its]
    qe = [tokw[j][0][:, kls[i]] for i, (j, _) in enumerate(its)]
    ke = [tokw[j][1][:, kls[i]] for i, (j, _) in enumerate(its)]
    kd = [tokw[j][2][:, kls[i]] for i, (j, _) in enumerate(its)]
    v = [v_ref[j, rows, vls[i]] for i, (j, _) in enumerate(its)]
    att = [jnp.where(incl, _dot_nt(qe[i], ke[i]), 0.0) for i in hs]
    yield
    s = [s_ref[j, h] for j, h in its]
    oi = [_dot(qe[i], s[i]) for i in hs]
    yield
    kv = [_dot_tn(kd[i], v[i]) for i in hs]
    yield
    o = [oi[i] + _dot(att[i], v[i]) for i in hs]
    for i, (j, h) in enumerate(its):
        dh = tokw[j][3][kls[i], :]
        s_ref[j, h] = s[i] * jnp.concatenate([dh, dh], axis=1) + kv[i]
    yield
    msq = [jnp.mean(o[i] * o[i], axis=-1, keepdims=True) for i in hs]
    yield
    for i, (j, _) in enumerate(its):
        on = o[i] * lax.rsqrt(msq[i] + EPS) * gn_ref[...]
        gg = gg_ref[j, rows, vls[i]]
        o_ref[j, rows, vls[i]] = (on * (gg * _sigmoid(gg))).astype(BF16)


def _mlstm_chunk(rows, ns, q_ref, k_ref, v_ref, mo_ref, sm_ref, brow_ref, bcol_ref, ng_ref,
                 o_ref, c_ref, n_ref, m_ref):
    c_ = CHUNK
    r, c = _tri(c_)
    incl = r >= c
    tril = jnp.where(incl, 1.0, 0.0).astype(BF16)
    triu = jnp.where(r <= c, 1.0, 0.0).astype(BF16)
    sub8 = lax.broadcasted_iota(jnp.int32, (8, c_), 0)
    eye = jnp.where(r == c, 1.0, 0.0).astype(BF16)
    sm, fcol, smt, frow = [], [], [], []
    for j in range(ns):
        raw = sm_ref[j, rows, :]
        sm.append(raw + brow_ref[...])
        fcol.append(_dot2_left(tril, _logsig(sm[j])))
        hi, lo = _split(raw)
        lo2 = (raw - hi.astype(F32) - lo.astype(F32)).astype(BF16)
        rawt = sum(lax.dot_general(piece, eye, _TN, preferred_element_type=F32) for piece in (hi, lo, lo2))
        smt.append(rawt[SM_I:SM_I + 8, :] + bcol_ref[...])
        frow.append(_dot2(jnp.where(sub8 >= 4, _logsig(smt[j]), 0.0), triu))
    yield
    its = [(j, h) for j in range(ns) for h in range(ML_HEADS)]
    hs = range(len(its))
    hls = [slice(h * ML_DH, (h + 1) * ML_DH) for _, h in its]
    f_t = [fcol[j][:, SM_F + h:SM_F + h + 1] for j, h in its]
    i_t = [sm[j][:, SM_I + h:SM_I + h + 1] for j, h in its]
    f_s = [frow[j][4 + h:5 + h, :] for j, h in its]
    i_s = [smt[j][h:h + 1, :] for j, h in its]
    m = [m_ref[j, h][0:1, 0:1] for j, h in its]
    q = [q_ref[j, rows, hls[i]] * (ML_DH ** -0.5) for i, (j, _) in enumerate(its)]
    k = [k_ref[j, rows, hls[i]] for i, (j, _) in enumerate(its)]
    v = [v_ref[j, rows, hls[i]] for i, (j, _) in enumerate(its)]
    qk0 = [_dot_nt(q[i], k[i]) for i in hs]
    gmax = [jnp.max(jnp.where(incl, i_s[i] - f_s[i], -jnp.inf), axis=1, keepdims=True) for i in hs]
    yield
    cm = [c_ref[j, h] for j, h in its]
    n8 = [n_ref[j, h] for j, h in its]
    qc = [_dot(q[i], cm[i]) for i in hs]
    qn = [_dot_nt(q[i], n8[i])[:, 0:1] for i in hs]
    yield
    mt = [jnp.maximum(f_t[i] + m[i], f_t[i] + gmax[i]) for i in hs]
    w_inter = [jnp.exp(f_t[i] + m[i] - mt[i]) for i in hs]
    d = [jnp.exp(jnp.where(incl, (f_t[i] - mt[i]) + (i_s[i] - f_s[i]), -jnp.inf)) for i in hs]
    qk = [qk0[i] * d[i] for i in hs]
    qkv = [_dot(qk[i], v[i]) for i in hs]
    yield
    m_new = [mt[i][c_ - 1:c_, :] for i in hs]
    f_last = [f_t[i][c_ - 1:c_, :] for i in hs]
    w_end = [jnp.exp(f_last[i] + m[i] - m_new[i]) for i in hs]
    kws = [k[i] * jnp.exp(f_last[i] - f_t[i] + i_t[i] - m_new[i]) for i in hs]
    kwv = [_dot_tn(kws[i], v[i]) for i in hs]
    yield
    for i, (j, h) in enumerate(its):
        c_ref[j, h] = w_end[i] * cm[i] + kwv[i]
        n_ref[j, h] = w_end[i] * n8[i] + jnp.broadcast_to(jnp.sum(kws[i], axis=0, keepdims=True),
                                                            (8, ML_DH))
        m_ref[j, h] = jnp.broadcast_to(m_new[i], (8, 128))
    den = [w_inter[i] * qn[i] + jnp.sum(qk[i], axis=1, keepdims=True) for i in hs]
    yield
    hm = [(w_inter[i] * qc[i] + qkv[i]) / jnp.maximum(jnp.abs(den[i]), jnp.exp(-mt[i])) for i in hs]
    mu = [jnp.mean(hm[i], axis=-1, keepdims=True) for i in hs]
    yield
    yc = [hm[i] - mu[i] for i in hs]
    var = [jnp.mean(yc[i] * yc[i], axis=-1, keepdims=True) for i in hs]
    yield
    for i, (j, _) in enumerate(its):
        y = yc[i] * lax.rsqrt(var[i] + EPS) * ng_ref[...]
        o_ref[j, rows, hls[i]] = (y * _sigmoid(mo_ref[j, rows, hls[i]])).astype(BF16)


def _rw_chunk(rows, ns, r_ref, k_ref, v_ref, wa_ref, gd_ref, mu_refs, par_refs, o_ref, s_ref, prev_refs):
    mur_ref, muk_ref, muv_ref, muwa_ref, mug_ref = mu_refs
    w0_ref, wup_ref, a0_ref, aup_ref, gup_ref, kkw_ref, kaw_ref, rk_ref, ng_ref = par_refs
    pr_s, pk_s, pv_s, pwa_s, pg_s = prev_refs
    c_ = CHUNK
    r_, c2 = _tri(c_, 128)
    s_idx = c2 % c_
    strict_pk = s_idx < r_
    incl_pk = s_idx <= r_
    eye_pk = jnp.where(s_idx == r_, 1.0, 0.0)
    rr, cc = _tri(c_)
    tril = jnp.where(rr >= cc, 1.0, 0.0).astype(BF16)
    lo_half = lax.broadcasted_iota(jnp.int32, (c_, 128), 1) < RW_HEAD
    r128, c128 = _tri(128)
    bdmask = (r128 // RW_HEAD) == (c128 // RW_HEAD)
    ones_bd = jnp.where(bdmask, 1.0, 0.0).astype(BF16)
    first_w = lax.broadcasted_iota(jnp.int32, (c_, RW_W), 0) == 0
    first_n = lax.broadcasted_iota(jnp.int32, (c_, 128), 0) == 0

    def mix(x_ref, prev_ref, mu_ref, j):
        x = x_ref[j, rows, :]
        first = first_w if x.shape[1] == RW_W else first_n
        prev = jnp.where(first, prev_ref[j], pltpu.roll(x, 1, 0))
        prev_ref[j] = x[c_ - 1:c_, :]
        return x + (prev - x) * mu_ref[...]

    def bd(y):
        yb = y.astype(BF16)
        zero = jnp.zeros_like(yb)
        return jnp.concatenate([jnp.where(lo_half, yb, zero), jnp.where(lo_half, zero, yb)], axis=0)

    def pk_mul(x, y):
        return jnp.dot(x.astype(BF16), bd(y), preferred_element_type=F32)

    def seg_all(xs):
        st = jnp.dot(jnp.concatenate(xs, axis=0).astype(BF16), ones_bd, preferred_element_type=F32)
        return [st[i * c_:(i + 1) * c_] for i in range(len(xs))]

    zr = [mix(r_ref, pr_s, mur_ref, j) for j in range(ns)]
    zv = [mix(v_ref, pv_s, muv_ref, j) for j in range(ns)]
    tok = [_rw_tokens(zr[j], mix(k_ref, pk_s, muk_ref, j), zv[j],
                      mix(wa_ref, pwa_s, muwa_ref, j), mix(gd_ref, pg_s, mug_ref, j),
                      w0_ref[...], wup_ref[...], a0_ref[...], aup_ref[...],
                      gup_ref[...], kkw_ref[...], kaw_ref[...]) for j in range(ns)]
    yield
    its = [(j, p) for j in range(ns) for p in range(RW_PAIRS)]
    prs = range(len(its))
    lns = [slice(p * 128, (p + 1) * 128) for _, p in its]
    lw = [tok[j][0][:, lns[i]] for i, (j, _) in enumerate(its)]
    a = [tok[j][1][:, lns[i]] for i, (j, _) in enumerate(its)]
    g = [tok[j][2][:, lns[i]] for i, (j, _) in enumerate(its)]
    kkp = [tok[j][3][:, lns[i]] for i, (j, _) in enumerate(its)]
    kp = [tok[j][4][:, lns[i]] for i, (j, _) in enumerate(its)]
    rp = [zr[j][:, lns[i]] for i, (j, _) in enumerate(its)]
    vp = [zv[j][:, lns[i]] for i, (j, _) in enumerate(its)]
    ssq = seg_all([x * x for x in kkp])
    cumf = [_dot2_left(tril, tok[j][0]) for j in range(ns)]
    yield
    cum = [cumf[j][:, lns[i]] for i, (j, _) in enumerate(its)]
    kkn = [kkp[p] * lax.rsqrt(ssq[p] + 1e-12) for p in prs]
    bp = [kkn[p] * a[p] for p in prs]
    cl = [cum[p][c_ - 1:c_, :] for p in prs]
    e_neg = [jnp.exp(-cum[p]) for p in prs]
    e_end = [jnp.exp(cl[p] - cum[p]) for p in prs]
    rh = [rp[p] * jnp.exp(cum[p]) for p in prs]
    kh = [kkn[p] * jnp.exp(cum[p] - lw[p]) for p in prs]
    kt = [kp[p] * e_neg[p] for p in prs]
    bt = [bp[p] * e_neg[p] for p in prs]
    kg = [kp[p] * e_end[p] for p in prs]
    bg = [bp[p] * e_end[p] for p in prs]
    kr = [jnp.concatenate([kh[p], rh[p]], axis=0).astype(BF16) for p in prs]
    ktb = [jnp.concatenate([bd(kt[p]), bd(bt[p])], axis=0) for p in prs]
    pkb = [lax.dot_general(kr[p], ktb[p], _NT, preferred_element_type=F32) for p in prs]
    yield
    ak = [jnp.where(strict_pk, pkb[p][0:c_, 0:128], 0.0) for p in prs]
    gk = [jnp.where(incl_pk, pkb[p][c_:2 * c_, 0:128], 0.0) for p in prs]
    gb = [jnp.where(incl_pk, pkb[p][c_:2 * c_, 128:256], 0.0) for p in prs]
    pw = [jnp.where(strict_pk, -pkb[p][0:c_, 128:256], 0.0) for p in prs]
    tm = [eye_pk + pw[p] for p in prs]
    agv = [pk_mul(jnp.concatenate([ak[p], gk[p]], axis=0), vp[p]) for p in prs]
    pw = [pk_mul(pw[p], pw[p]) for p in prs]
    yield
    for _ in range(4):
        x = [pk_mul(jnp.concatenate([pw[p], tm[p]], axis=0), pw[p]) for p in prs]
        tm = [tm[p] + x[p][c_:2 * c_] for p in prs]
        pw = [x[p][0:c_] for p in prs]
        yield
    tm = [tm[p] + pk_mul(tm[p], pw[p]) for p in prs]
    yield
    ku = [jnp.dot(tm[p].astype(BF16), jnp.concatenate([bd(kh[p]), bd(agv[p][0:c_])], axis=1),
                  preferred_element_type=F32) for p in prs]
    yield
    gku = [jnp.dot(gb[p].astype(BF16),
                   jnp.concatenate([bd(ku[p][:, 0:128]), bd(ku[p][:, 128:256])], axis=1),
                   preferred_element_type=F32) for p in prs]
    d = [_dot_tn(jnp.concatenate([vp[p], -ku[p][:, 128:256]], axis=0),
                 jnp.concatenate([kg[p], bg[p]], axis=0)) for p in prs]
    yield
    rhp = [rh[p] - gku[p][:, 0:128] for p in prs]
    y0 = [agv[p][c_:2 * c_] - gku[p][:, 128:256] for p in prs]
    s = [s_ref[j, p] for j, p in its]
    uy = [_dot_nt(jnp.concatenate([ku[p][:, 0:128], rhp[p]], axis=0), s[p]) for p in prs]
    yield
    ub = [_dot_tn(uy[p][0:c_], bg[p]) for p in prs]
    y = [uy[p][c_:2 * c_] + y0[p] for p in prs]
    mean = seg_all(y)
    yield
    for i, (j, p) in enumerate(its):
        s_ref[j, p] = s[i] * jnp.exp(cl[i]) + jnp.where(bdmask, d[i] - ub[i], 0.0)
    yc = [y[p] - mean[p] * (1.0 / RW_HEAD) for p in prs]
    var = seg_all([x * x for x in yc])
    bon = seg_all([rp[p] * kp[p] * rk_ref[:, lns[p]] for p in prs])
    yield
    for i, (j, _) in enumerate(its):
        yn = yc[i] * lax.rsqrt(var[i] * (1.0 / RW_HEAD) + EPS) * ng_ref[:, lns[i]]
        o_ref[j, rows, lns[i]] = ((yn + bon[i] * vp[i]) * g[i]).astype(BF16)


def _mixers_kernel(*refs, n_chunks, ns):
    (gq, gk, gv, gg, sm, mq, mk, mv, mo, rr, rk, rv, rwa, rgd) = refs[0:14]
    aup, ab, gn, brow, bcol, mng = refs[14:20]
    mu_refs = refs[20:25]
    par_refs = refs[25:34]
    o_gla, o_ml, o_rw, gla_out, c_out, n_out, m_out, rw_out = refs[34:42]
    s_gla, c_s, n_s, m_s, s_rw = refs[42:47]
    prev_refs = refs[47:52]
    t = pl.program_id(1)

    @pl.when(t == 0)
    def _():
        for ref in (s_gla, c_s, n_s, m_s, s_rw) + tuple(prev_refs):
            ref[...] = jnp.zeros_like(ref)

    def chunk(ci, carry):
        rows = pl.ds(pl.multiple_of(ci * CHUNK, CHUNK), CHUNK)
        _interleave(
            {"r": _rw_chunk(rows, ns, rr, rk, rv, rwa, rgd, mu_refs, par_refs, o_rw, s_rw, prev_refs),
             "g": _gla_chunk(rows, ns, gq, gk, gv, gg, sm, aup, ab, gn, o_gla, s_gla),
             "m": _mlstm_chunk(rows, ns, mq, mk, mv, mo, sm, brow, bcol, mng, o_ml, c_s, n_s, m_s)},
            "rmrg")
        return carry

    lax.fori_loop(0, n_chunks, chunk, 0)

    @pl.when(t == pl.num_programs(1) - 1)
    def _():
        gla_out[...] = s_gla[...]
        c_out[...] = c_s[...]
        for j in range(ns):
            for h in range(ML_HEADS):
                n_out[j, h:h + 1, :] = n_s[j, h][0:1, :]
                m_out[j, h:h + 1, :] = m_s[j, h][0:1, :]
            for p in range(RW_PAIRS):
                s = s_rw[j, p]
                rw_out[j, 2 * p] = s[0:RW_HEAD, 0:RW_HEAD]
                rw_out[j, 2 * p + 1] = s[RW_HEAD:, RW_HEAD:]


def _mixers_prompt(z, lp, b_, t_, tt):
    ns = _seq_per_step(b_)
    nc = tt // CHUNK
    z3 = z.reshape(b_, t_, NP)
    zblk = lambda w, off: pl.BlockSpec((ns, tt, w), lambda b, t, c=off // w: (b, t, c))
    const = lambda b, t: (0, 0)
    full = lambda a: pl.BlockSpec(a.shape, const)
    seq = lambda *shape: pl.BlockSpec((ns,) + shape, lambda b, t, n=len(shape): (b,) + (0,) * n)
    params = [lp["aup_pad"], lp["ab"], lp["gla_ng"], lp["brow"], lp["bcol"], lp["ml_ng"], *lp["mus"], *lp["rw"]]
    bf = jax.ShapeDtypeStruct((b_, t_, D_MODEL), BF16)
    outs = pl.pallas_call(
        functools.partial(_mixers_kernel, n_chunks=nc, ns=ns),
        grid=(b_ // ns, t_ // tt),
        in_specs=[zblk(GLA_QK, OFF_GQ), zblk(GLA_QK, OFF_GK), zblk(GLA_V, OFF_GV), zblk(GLA_V, OFF_GG),
                  zblk(128, OFF_SM),
                  zblk(ML_W, OFF_MQ), zblk(ML_W, OFF_MK), zblk(ML_W, OFF_MV), zblk(ML_W, OFF_MO),
                  zblk(RW_W, OFF_RW), zblk(RW_W, OFF_RW + RW_W), zblk(RW_W, OFF_RW + 2 * RW_W),
                  zblk(128, OFF_RW + 3 * RW_W), zblk(128, OFF_RW + 3 * RW_W + 128)]
                 + [full(a) for a in params],
        out_specs=[pl.BlockSpec((ns, tt, D_MODEL), lambda b, t: (b, t, 0))] * 3
                  + [seq(GLA_HEADS, GLA_DK, GLA_DV), seq(ML_HEADS, ML_DH, ML_DH), seq(ML_HEADS, ML_DH),
                     seq(ML_HEADS, 128), seq(RW_HEADS, RW_HEAD, RW_HEAD)],
        out_shape=[bf, bf, bf,
                   jax.ShapeDtypeStruct((b_, GLA_HEADS, GLA_DK, GLA_DV), F32),
                   jax.ShapeDtypeStruct((b_, ML_HEADS, ML_DH, ML_DH), F32),
                   jax.ShapeDtypeStruct((b_, ML_HEADS, ML_DH), F32),
                   jax.ShapeDtypeStruct((b_, ML_HEADS, 128), F32),
                   jax.ShapeDtypeStruct((b_, RW_HEADS, RW_HEAD, RW_HEAD), F32)],
        scratch_shapes=[pltpu.VMEM((ns, GLA_HEADS, GLA_DK, GLA_DV), F32),
                        pltpu.VMEM((ns, ML_HEADS, ML_DH, ML_DH), F32),
                        pltpu.VMEM((ns, ML_HEADS, 8, ML_DH), F32),
                        pltpu.VMEM((ns, ML_HEADS, 8, 128), F32),
                        pltpu.VMEM((ns, RW_PAIRS, 128, 128), F32),
                        pltpu.VMEM((ns, 1, RW_W), F32), pltpu.VMEM((ns, 1, RW_W), F32),
                        pltpu.VMEM((ns, 1, RW_W), F32), pltpu.VMEM((ns, 1, 128), F32),
                        pltpu.VMEM((ns, 1, 128), F32)],
        compiler_params=pltpu.CompilerParams(
            dimension_semantics=("parallel", "arbitrary"), vmem_limit_bytes=MIXER_VMEM_LIMIT),
        name="mixers_prompt",
    )(*([z3] * 14), *params)
    o_gla, o_ml, o_rw, gla_new, c_new, n_new, m_new, rw_new = outs
    flat = lambda o: o.reshape(b_ * t_, D_MODEL)
    return flat(o_gla), flat(o_ml), flat(o_rw), (gla_new, c_new, n_new, m_new[:, :, 0], rw_new)


def _sample_prep_kernel(sm_ref, r_ref, k_ref, v_ref, wa_ref, gd_ref, m_ref, shift_ref,
                        aup_ref, ab_ref, ib_ref, fb_ref,
                        mur_ref, muk_ref, muv_ref, muwa_ref, mug_ref,
                        w0_ref, wup_ref, a0_ref, aup2_ref, gup_ref, kkw_ref, kaw_ref,
                        gla_la_ref, mnew_ref, wend_ref, ws_ref,
                        rr_ref, rk_ref, rv_ref, rg_ref, kkt_ref, bt_ref, kt_ref, rt_ref, wt_ref, vt_ref):
    sm = sm_ref[...]
    gla_la_ref[...] = _logsig(_dot(sm, aup_ref[...]) + ab_ref[...]) * (1.0 / GLA_TAU)
    i_pre = sm[:, SM_I:SM_I + ML_HEADS] + ib_ref[...]
    logf = _logsig(sm[:, SM_F:SM_F + ML_HEADS] + fb_ref[...])
    m = m_ref[0]
    m_new = jnp.maximum(logf + m, i_pre)
    mnew_ref[...] = m_new
    wend_ref[...] = jnp.exp(logf + m - m_new)
    ws_ref[...] = jnp.exp(i_pre - m_new)
    mixf = lambda x, off, mu: x[...] + (shift_ref[0, :, off:off + x.shape[1]] - x[...]) * mu[...]
    zr = mixf(r_ref, 0, mur_ref)
    zk = mixf(k_ref, RW_W, muk_ref)
    zv = mixf(v_ref, 2 * RW_W, muv_ref)
    zwa = mixf(wa_ref, 3 * RW_W, muwa_ref)
    zg = mixf(gd_ref, 3 * RW_W + 128, mug_ref)
    lw, a, g, kk0, kmod = _rw_tokens(zr, zk, zv, zwa, zg, w0_ref[...], wup_ref[...], a0_ref[...],
                                     aup2_ref[...], gup_ref[...], kkw_ref[...], kaw_ref[...])
    ones_bd = _ones_bd()
    for p in range(RW_PAIRS):
        ln = slice(p * 128, (p + 1) * 128)
        kkp = kk0[:, ln]
        kkn = kkp * lax.rsqrt(_seg(kkp * kkp, ones_bd) + 1e-12)
        kkt_ref[ln, :] = kkn.T
        bt_ref[ln, :] = (kkn * a[:, ln]).T
    rr_ref[...] = zr
    rk_ref[...] = kmod
    rv_ref[...] = zv
    rg_ref[...] = g
    kt_ref[...] = kmod.T
    rt_ref[...] = zr.T
    wt_ref[...] = jnp.exp(lw).T
    vt_ref[...] = zv.T


STEP_BB = 8


def _row_select(rows_list, rid):
    acc = rows_list[0]
    for b in range(1, len(rows_list)):
        acc = jnp.where(rid == b, rows_list[b], acc)
    return acc


def _gla_step_kernel(s_ref, q_ref, k_ref, la_ref, v_ref, *rest):
    sn_ref, o_ref = rest[-2:]
    rid_k = lax.broadcasted_iota(jnp.int32, (STEP_BB, GLA_DK), 0)
    rid_v = lax.broadcasted_iota(jnp.int32, (STEP_BB, GLA_DV), 0)
    bs = range(STEP_BB)
    onehot = [jnp.where(rid_v == b, 1.0, 0.0).astype(BF16) for b in bs]
    for h in range(GLA_HEADS):
        kl = slice(h * GLA_DK, (h + 1) * GLA_DK)
        vl = slice(h * GLA_DV, (h + 1) * GLA_DV)
        qt = q_ref[:, kl] * (GLA_DK ** -0.5)
        kt = k_ref[:, kl]
        lt = la_ref[:, kl]
        vt = v_ref[:, vl]
        dec = [jnp.exp(_dot2_tn(lt, onehot[b])) for b in bs]
        kv = [_dot3(jnp.where(rid_k == b, kt, 0.0), vt, _TN) for b in bs]
        sn = [dec[b] * s_ref[0, b, h] + kv[b] for b in bs]
        for b in bs:
            sn_ref[0, b, h] = sn[b]
        o_ref[:, vl] = _row_select([_dot3(qt, sn[b]) for b in bs], rid_v)


def _mlstm_step_kernel(c_ref, n_ref, q_ref, k_ref, v_ref, wend_ref, ws_ref, mnew_ref, *rest):
    cn_ref, nn_ref, h_ref = rest[-3:]
    rid = lax.broadcasted_iota(jnp.int32, (STEP_BB, ML_DH), 0)
    for h in range(ML_HEADS):
        hl = slice(h * ML_DH, (h + 1) * ML_DH)
        qt = q_ref[:, hl] * (ML_DH ** -0.5)
        kt = k_ref[:, hl]
        vt = v_ref[:, hl]
        wend = wend_ref[:, h:h + 1]
        ws = ws_ref[:, h:h + 1]
        nn = wend * n_ref[0, :, hl] + ws * kt
        nn_ref[0, :, hl] = nn
        den = jnp.sum(qt * nn, axis=1, keepdims=True)
        kws = kt * ws
        nums = []
        for half in range(2):
            bs = range(half * 4, half * 4 + 4)
            kwv = [_dot3(jnp.where(rid == b, kws, 0.0), vt, _TN) for b in bs]
            cn = [wend_ref[b:b + 1, h:h + 1] * c_ref[0, b, h] + kwv[i] for i, b in enumerate(bs)]
            for i, b in enumerate(bs):
                cn_ref[0, b, h] = cn[i]
            nums += [_dot3(qt, cn[i]) for i in range(4)]
        num = _row_select(nums, rid)
        h_ref[:, hl] = num / jnp.maximum(jnp.abs(den), jnp.exp(-mnew_ref[:, h:h + 1]))


RW_STEP_HB = 2


def _rw_step_kernel(s_ref, kk_ref, b_ref, k_ref, r_ref, w_ref, v_ref, *rest):
    sn_ref, y_ref = rest[-2:]
    n = RW_HEAD
    for hh in range(RW_STEP_HB):
        fl = slice(hh * n, (hh + 1) * n)
        nkk, bb, kt, rt, wt = -kk_ref[fl, :], b_ref[fl, :], k_ref[fl, :], r_ref[fl, :], w_ref[fl, :]

        def row(i, carry):
            s = s_ref[0, hh, i]
            sa = jnp.sum(s * nkk, axis=0, keepdims=True)
            sn = s * wt + sa * bb + v_ref[pl.ds(hh * n + i, 1), :] * kt
            sn_ref[0, hh, i] = sn
            y_ref[pl.ds(hh * n + i, 1), :] = jnp.sum(sn * rt, axis=0, keepdims=True)
            return carry

        lax.fori_loop(0, n, row, 0, unroll=4)


def _rw_state_call(l, state_t, vecs_t, prev_out):
    nb = state_t.shape[-1]
    st_spec = pl.BlockSpec((1, RW_STEP_HB, RW_HEAD, RW_HEAD, nb), lambda h: (l, h, 0, 0, 0))
    vec_spec = pl.BlockSpec((RW_STEP_HB * RW_HEAD, nb), lambda h: (h, 0))
    args = [state_t] + list(vecs_t)
    in_specs = [st_spec] + [vec_spec] * len(vecs_t)
    aliases = {}
    if prev_out is not None:
        aliases[len(args)] = 0
        in_specs.append(pl.BlockSpec(memory_space=pl.ANY))
        args.append(prev_out)
    return pl.pallas_call(
        _rw_step_kernel,
        grid=(RW_HEADS // RW_STEP_HB,),
        in_specs=in_specs,
        out_specs=[st_spec, vec_spec],
        out_shape=[jax.ShapeDtypeStruct(state_t.shape, F32), jax.ShapeDtypeStruct((RW_W, nb), F32)],
        input_output_aliases=aliases,
        compiler_params=pltpu.CompilerParams(
            dimension_semantics=("parallel",), vmem_limit_bytes=VMEM_LIMIT),
        name="rwkv_step",
    )(*args)


def _sample_post_kernel(og_ref, gg_ref, gn_ref, hm_ref, mo_ref, mn_ref,
                        yt_ref, r_ref, k_ref, v_ref, g_ref, rk_ref, ng_ref,
                        ogla_ref, oml_ref, orw_ref):
    for h in range(GLA_HEADS):
        vl = slice(h * GLA_DV, (h + 1) * GLA_DV)
        o = og_ref[:, vl]
        on = o * lax.rsqrt(jnp.mean(o * o, axis=-1, keepdims=True) + EPS) * gn_ref[...]
        gg = gg_ref[:, vl]
        ogla_ref[:, vl] = (on * (gg * _sigmoid(gg))).astype(BF16)
    for h in range(ML_HEADS):
        hl = slice(h * ML_DH, (h + 1) * ML_DH)
        hm = hm_ref[:, hl]
        y = hm - jnp.mean(hm, axis=-1, keepdims=True)
        y = y * lax.rsqrt(jnp.mean(y * y, axis=-1, keepdims=True) + EPS) * mn_ref[...]
        oml_ref[:, hl] = (y * _sigmoid(mo_ref[:, hl])).astype(BF16)
    ones_bd = _ones_bd()
    for p in range(RW_PAIRS):
        ln = slice(p * 128, (p + 1) * 128)
        y = yt_ref[ln, :].T
        orw_ref[:, ln] = _rw_epilogue(y, r_ref[:, ln], k_ref[:, ln], v_ref[:, ln],
                                      g_ref[:, ln], rk_ref[:, ln], ng_ref[:, ln], ones_bd)


def _full_call(kernel_fn, args, out_shapes, name):
    nd = lambda a: (lambda i: (0,) * len(a.shape))
    in_specs, arrays = [], []
    for a in args:
        if isinstance(a, tuple):
            arr, blk, idx = a
            in_specs.append(pl.BlockSpec(blk, lambda i, idx=idx: idx))
            arrays.append(arr)
        else:
            in_specs.append(pl.BlockSpec(a.shape, nd(a)))
            arrays.append(a)
    return pl.pallas_call(
        kernel_fn,
        grid=(1,),
        in_specs=in_specs,
        out_specs=[pl.BlockSpec(s.shape, nd(s)) for s in out_shapes],
        out_shape=out_shapes,
        compiler_params=pltpu.CompilerParams(vmem_limit_bytes=VMEM_LIMIT),
        name=name,
    )(*arrays)


def _state_call(kernel_fn, l, states, vecs, prev_outs, out_width, name):
    nb = states[0].shape[1]
    st_spec = lambda a: pl.BlockSpec((1, STEP_BB) + tuple(a.shape[2:]),
                                     lambda i, n=a.ndim: (l, i) + (0,) * (n - 2))
    in_specs = [st_spec(a) for a in states]
    in_specs += [pl.BlockSpec((STEP_BB, w), lambda i, c=c: (i, c)) for _, w, c in vecs]
    args = list(states) + [a for a, _, _ in vecs]
    aliases = {}
    for j, po in enumerate(prev_outs or ()):
        aliases[len(args)] = j
        in_specs.append(pl.BlockSpec(memory_space=pl.ANY))
        args.append(po)
    out_shape = [jax.ShapeDtypeStruct(a.shape, F32) for a in states]
    out_shape.append(jax.ShapeDtypeStruct((nb, out_width), F32))
    out_specs = [st_spec(a) for a in states] + [pl.BlockSpec((STEP_BB, out_width), lambda i: (i, 0))]
    return pl.pallas_call(
        kernel_fn,
        grid=(nb // STEP_BB,),
        in_specs=in_specs,
        out_specs=out_specs,
        out_shape=out_shape,
        input_output_aliases=aliases,
        compiler_params=pltpu.CompilerParams(
            dimension_semantics=("parallel",), vmem_limit_bytes=VMEM_LIMIT),
        name=name,
    )(*args)


W_IN_MOVES = ((0, 0, 2048), (2064, 2048, 1024), (3088, 3072, 3072), (6168, 6144, 1024), (7192, 7168, 6400))
D_IN = 13592
W_PREP_ROWS = 128


def _w_in_prep_kernel(w_ref, o_ref):
    for src, dst, width in W_IN_MOVES:
        o_ref[0, :, dst:dst + width] = w_ref[0, :, src:src + width].astype(BF16)
    lane = lax.broadcasted_iota(jnp.int32, (W_PREP_ROWS, 128), 1)
    ga = w_ref[0, :, 2048:2176]
    mif = w_ref[0, :, 6144:6272]
    small = jnp.where(lane < GLA_RANK, ga, jnp.where(lane < SM_F + ML_HEADS, mif, 0.0))
    o_ref[0, :, OFF_SM:OFF_SM + 128] = small.astype(BF16)
    o_ref[0, :, OFF_SM + 128:NP] = jnp.zeros((W_PREP_ROWS, NP - OFF_SM - 128), BF16)


def _w_in_prep(w_in):
    depth = w_in.shape[0]
    return pl.pallas_call(
        _w_in_prep_kernel,
        grid=(depth, D_MODEL // W_PREP_ROWS),
        in_specs=[pl.BlockSpec((1, W_PREP_ROWS, D_IN), lambda l, i: (l, i, 0))],
        out_specs=pl.BlockSpec((1, W_PREP_ROWS, NP), lambda l, i: (l, i, 0)),
        out_shape=jax.ShapeDtypeStruct((depth, D_MODEL, NP), BF16),
        compiler_params=pltpu.CompilerParams(
            dimension_semantics=("parallel", "parallel"), vmem_limit_bytes=VMEM_LIMIT),
        name="w_in_prep",
    )(w_in)


def _prep_layer_params(l, p):
    w_in_p = p["w_in_p"][l]
    aup_pad = jnp.zeros((128, GLA_QK), F32).at[0:GLA_RANK].set(p["gla_a_up"][l])
    brow = (jnp.zeros((1, 128), F32).at[0, SM_I:SM_I + 4].set(p["ml_i_bias"][l])
            .at[0, SM_F:SM_F + 4].set(p["ml_f_bias"][l]))
    bcol = jnp.broadcast_to(jnp.concatenate([p["ml_i_bias"][l], p["ml_f_bias"][l]])[:, None], (8, CHUNK))
    mu = p["rw_mu"][l][None, :]
    mus = (mu[:, 0:1024], mu[:, 1024:2048], mu[:, 2048:3072], mu[:, 3072:3200], mu[:, 3200:3328])
    wup_pad = jnp.zeros((128, RW_W), F32).at[0:64].set(p["rw_w_up"][l])
    aup2_pad = jnp.zeros((128, RW_W), F32).at[64:128].set(p["rw_a_up"][l])
    row = lambda a: a.reshape(1, -1)
    rw_params = (row(p["rw_w0"][l]), wup_pad, row(p["rw_a0"][l]), aup2_pad, p["rw_g_up"][l],
                 row(p["rw_k_k"][l]), row(p["rw_k_a"][l]), row(p["rw_r_k"][l]), row(p["rw_norm_g"][l]))
    return dict(
        norm1_g=row(p["norm1_g"][l]), w_in=w_in_p, aup_pad=aup_pad, ab=row(p["gla_a_bias"][l]),
        gla_ng=row(p["gla_norm_g"][l]), brow=brow, bcol=bcol, ib=row(p["ml_i_bias"][l]),
        fb=row(p["ml_f_bias"][l]), ml_ng=row(p["ml_norm_g"][l]), mus=mus, rw=rw_params,
        wb=p["w_branch"][l].astype(BF16), wo=p["w_out"][l].astype(BF16), norm2_g=row(p["norm2_g"][l]),
        w1=p["w_ff1"][l].astype(BF16), w2=p["w_ff2"][l].astype(BF16))


def _prompt_layer(x, lp, fg, final, b_, t_):
    rows = x.shape[0]
    tm = min(512, rows)
    tm_big = min(1024, rows)
    z = _proj_in(x, lp["norm1_g"], lp["w_in"], tm_big)
    o_gla, o_ml, o_rw, new_states = _mixers_prompt(z, lp, b_, t_, min(MIXER_TT, t_))
    x = _merge(o_gla, o_ml, o_rw, z, x, lp["wb"], lp["wo"], tm)
    x = _ffn(x, lp["norm2_g"], lp["w1"], lp["w2"], fg, final, tm_big)
    shift = z.reshape(b_, t_, NP)[:, t_ - 1, OFF_RW:OFF_RW + RW_COLS]
    return x, new_states + (shift,)


def _sample_layer(l, x, states, prev, lp, fg, final):
    st_gla, st_c, st_n, st_m, st_rw, st_shift = states
    nb = x.shape[0]
    z = _proj_in(x, lp["norm1_g"], lp["w_in"], nb)
    zw = lambda off, w: (z, (nb, w), (0, off // w))
    w0, wup, a0, aup2, gup, kkw, kaw, rk, ng = lp["rw"]
    sds = lambda *s: jax.ShapeDtypeStruct(s, F32)
    prep_args = [zw(OFF_SM, 128), zw(OFF_RW, RW_W), zw(OFF_RW + RW_W, RW_W), zw(OFF_RW + 2 * RW_W, RW_W),
                 zw(OFF_RW + 3 * RW_W, 128), zw(OFF_RW + 3 * RW_W + 128, 128),
                 (st_m, (1, nb, ML_HEADS), (l, 0, 0)), (st_shift, (1, nb, RW_COLS), (l, 0, 0)),
                 lp["aup_pad"], lp["ab"], lp["ib"], lp["fb"], *lp["mus"],
                 w0, wup, a0, aup2, gup, kkw, kaw]
    (la, m_new, wend, ws, rr, rkm, rv, rg, kkt, bt, kt, rt, wt, vt) = _full_call(
        _sample_prep_kernel, prep_args,
        [sds(nb, GLA_QK), sds(nb, 4), sds(nb, 4), sds(nb, 4)] + [sds(nb, RW_W)] * 4 + [sds(RW_W, nb)] * 6,
        "sample_prep")

    pv = lambda *idx: None if prev is None else [prev[i] for i in idx]
    gla_all, o_g = _state_call(
        _gla_step_kernel, l, [st_gla],
        [(z, GLA_QK, OFF_GQ // GLA_QK), (z, GLA_QK, OFF_GK // GLA_QK), (la, GLA_QK, 0),
         (z, GLA_V, OFF_GV // GLA_V)], pv(0), GLA_V, "gla_step")
    c_all, n_all, hm = _state_call(
        _mlstm_step_kernel, l, [st_c, st_n],
        [(z, ML_W, OFF_MQ // ML_W), (z, ML_W, OFF_MK // ML_W), (z, ML_W, OFF_MV // ML_W),
         (wend, 4, 0), (ws, 4, 0), (m_new, 4, 0)], pv(1, 2), ML_W, "mlstm_step")
    rw_all, y = _rw_state_call(l, st_rw, (kkt, bt, kt, rt, wt, vt), None if prev is None else prev[3])

    bf = lambda: jax.ShapeDtypeStruct((nb, D_MODEL), BF16)
    o_gla, o_ml, o_rw = _full_call(
        _sample_post_kernel,
        [o_g, zw(OFF_GG, GLA_V), lp["gla_ng"], hm, zw(OFF_MO, ML_W), lp["ml_ng"],
         y, rr, rkm, rv, rg, rk, ng],
        [bf(), bf(), bf()], "sample_post")
    x = _merge(o_gla, o_ml, o_rw, z, x, lp["wb"], lp["wo"], nb)
    x = _ffn(x, lp["norm2_g"], lp["w1"], lp["w2"], fg, final, nb)
    return x, (gla_all, c_all, n_all, rw_all), (m_new, z[:, OFF_RW:OFF_RW + RW_COLS])


def kernel(x_prompt, x_sample, state_gla, state_mlstm_c, state_mlstm_n, state_mlstm_m, state_rwkv, state_shift, norm1_g, w_in, gla_a_up, gla_a_bias, gla_norm_g, ml_i_bias, ml_f_bias, ml_norm_g, rw_mu, rw_w0, rw_w_up, rw_a0, rw_a_up, rw_g_up, rw_k_k, rw_k_a, rw_r_k, rw_norm_g, w_branch, w_out, norm2_g, w_ff1, w_ff2, final_g):
    p = dict(norm1_g=norm1_g, w_in=w_in, gla_a_up=gla_a_up, gla_a_bias=gla_a_bias, gla_norm_g=gla_norm_g,
             ml_i_bias=ml_i_bias, ml_f_bias=ml_f_bias, ml_norm_g=ml_norm_g, rw_mu=rw_mu, rw_w0=rw_w0,
             rw_w_up=rw_w_up, rw_a0=rw_a0, rw_a_up=rw_a_up, rw_g_up=rw_g_up, rw_k_k=rw_k_k, rw_k_a=rw_k_a,
             rw_r_k=rw_r_k, rw_norm_g=rw_norm_g, w_branch=w_branch, w_out=w_out, norm2_g=norm2_g,
             w_ff1=w_ff1, w_ff2=w_ff2)
    depth = w_in.shape[0]
    p["w_in_p"] = _w_in_prep(w_in)
    b_, t_, _ = x_prompt.shape
    nb = x_sample.shape[0]
    fg = final_g.reshape(1, D_MODEL)
    xp = x_prompt.reshape(b_ * t_, D_MODEL)
    xs = x_sample.reshape(nb, D_MODEL)
    s_in = (state_gla, state_mlstm_c, state_mlstm_n.reshape(depth, nb, ML_W), state_mlstm_m,
            jnp.transpose(state_rwkv, (0, 2, 3, 4, 1)), state_shift)
    p_states, s_big, s_small = [], None, []
    for l in range(depth):
        lp = _prep_layer_params(l, p)
        final = l == depth - 1
        xp, stp = _prompt_layer(xp, lp, fg, final, b_, t_)
        xs, s_big, small = _sample_layer(l, xs, s_in, s_big, lp, fg, final)
        p_states.append(stp)
        s_small.append(small)
    ps = [jnp.stack([s[i] for s in p_states]) for i in range(6)]
    gla_s, mlc_s, mln_s, rw_s = s_big
    rw_s = jnp.transpose(rw_s, (0, 4, 1, 2, 3))
    mln_s = mln_s.reshape(depth, nb, ML_HEADS, ML_DH)
    mlm_s = jnp.stack([s[0] for s in s_small])
    shift_s = jnp.stack([s[1] for s in s_small])
    return (xp.reshape(b_, t_, D_MODEL), xs.reshape(nb, 1, D_MODEL),
            ps[0], gla_s, ps[1], mlc_s, ps[2], mln_s, ps[3], mlm_s, ps[4], rw_s, ps[5], shift_s)
```

```python
import functools
import math

import jax
import jax.numpy as jnp
from jax import lax
from jax.experimental import pallas as pl
from jax.experimental.pallas import tpu as pltpu

F32 = jnp.float32
BF16 = jnp.bfloat16

D_MODEL = 1024
GLA_HEADS, GLA_DK, GLA_DV = 4, 128, 256
GLA_QK, GLA_V, GLA_RANK, GLA_TAU = 512, 1024, 16, 16.0
ML_HEADS, ML_DH, ML_W = 4, 256, 1024
RW_HEAD, RW_W, RW_HEADS = 64, 1024, 16
RW_PAIRS = RW_HEADS // 2
RW_COLS = 3 * RW_W + 64 + 64 + 128
D_FF = 4 * D_MODEL
EPS = 1e-6
CHUNK = 64

OFF_GQ, OFF_GK, OFF_GV, OFF_GG = 0, 512, 1024, 2048
OFF_MQ, OFF_MK, OFF_MV, OFF_MO = 3072, 4096, 5120, 6144
OFF_GATE = 7168
OFF_RW = 10240
OFF_SM = 13568
NP = 13824
SM_I, SM_F = 16, 20

VMEM_LIMIT = 48 * 1024 * 1024
MIXER_VMEM_LIMIT = 56 * 1024 * 1024
MIXER_TT = 128


def _dot(a, b):
    return jnp.dot(a.astype(BF16), b.astype(BF16), preferred_element_type=F32)


def _dot_nt(a, b):
    return lax.dot_general(a.astype(BF16), b.astype(BF16), (((1,), (1,)), ((), ())),
                           preferred_element_type=F32)


def _dot_tn(a, b):
    return lax.dot_general(a.astype(BF16), b.astype(BF16), (((0,), (0,)), ((), ())),
                           preferred_element_type=F32)


def _split(x):
    hi = x.astype(BF16)
    lo = (x - hi.astype(F32)).astype(BF16)
    return hi, lo


def _dot2(x, w_exact):
    hi, lo = _split(x)
    return (jnp.dot(hi, w_exact, preferred_element_type=F32)
            + jnp.dot(lo, w_exact, preferred_element_type=F32))


def _dot2_tn(x, w_exact):
    hi, lo = _split(x)
    dn = (((0,), (0,)), ((), ()))
    return (lax.dot_general(hi, w_exact, dn, preferred_element_type=F32)
            + lax.dot_general(lo, w_exact, dn, preferred_element_type=F32))


def _dot2_left(w_exact, x):
    hi, lo = _split(x)
    return (jnp.dot(w_exact, hi, preferred_element_type=F32)
            + jnp.dot(w_exact, lo, preferred_element_type=F32))


_NN = (((1,), (0,)), ((), ()))
_TN = (((0,), (0,)), ((), ()))
_NT = (((1,), (1,)), ((), ()))


def _dot3(a, b, dn=_NN):
    ah, al = _split(a)
    bh, bl = _split(b)
    f = lambda x, y: lax.dot_general(x, y, dn, preferred_element_type=F32)
    return f(ah, bh) + f(al, bh) + f(ah, bl)


def _logsig(x):
    return jnp.minimum(x, 0.0) - jnp.log(1.0 + jnp.exp(-jnp.abs(x)))


def _sigmoid(x):
    return jax.nn.sigmoid(x)


def _rms(x, g):
    return x * lax.rsqrt(jnp.mean(x * x, axis=-1, keepdims=True) + EPS) * g


def _tri(n, m=None):
    m = n if m is None else m
    r = lax.broadcasted_iota(jnp.int32, (n, m), 0)
    c = lax.broadcasted_iota(jnp.int32, (n, m), 1)
    return r, c


def _proj_in_kernel(x_ref, g_ref, w_ref, o_ref, h_ref):
    @pl.when(pl.program_id(1) == 0)
    def _():
        h_ref[...] = _rms(x_ref[...], g_ref[...]).astype(BF16)

    o_ref[...] = lax.dot_general(h_ref[...], w_ref[0], _NT, preferred_element_type=F32)


def _proj_in(x, g, w_layer, tm, tn=2304):
    w_t, l = w_layer
    m = x.shape[0]
    return pl.pallas_call(
        _proj_in_kernel,
        grid=(m // tm, NP // tn),
        in_specs=[pl.BlockSpec((tm, D_MODEL), lambda i, j: (i, 0)),
                  pl.BlockSpec((1, D_MODEL), lambda i, j: (0, 0)),
                  pl.BlockSpec((1, tn, D_MODEL), lambda i, j: (l, j, 0))],
        out_specs=pl.BlockSpec((tm, tn), lambda i, j: (i, j)),
        out_shape=jax.ShapeDtypeStruct((m, NP), F32),
        scratch_shapes=[pltpu.VMEM((tm, D_MODEL), BF16)],
        compiler_params=pltpu.CompilerParams(
            dimension_semantics=("parallel", "arbitrary"), vmem_limit_bytes=VMEM_LIMIT),
        name="proj_in",
    )(x, g, w_t)


def _merge_kernel(og_ref, om_ref, or_ref, g0_ref, g1_ref, g2_ref, x_ref, wb_ref, wo_ref, o_ref):
    acc = jnp.dot(og_ref[...], wb_ref[0], preferred_element_type=F32) * _sigmoid(g0_ref[...])
    acc += jnp.dot(om_ref[...], wb_ref[1], preferred_element_type=F32) * _sigmoid(g1_ref[...])
    acc += jnp.dot(or_ref[...], wb_ref[2], preferred_element_type=F32) * _sigmoid(g2_ref[...])
    o_ref[...] = x_ref[...] + jnp.dot(acc.astype(BF16), wo_ref[...], preferred_element_type=F32)


def _merge(o_gla, o_ml, o_rw, z, x, wb, wo, tm):
    m = x.shape[0]
    row = lambda i: (i, 0)
    gate_blk = OFF_GATE // D_MODEL
    return pl.pallas_call(
        _merge_kernel,
        grid=(m // tm,),
        in_specs=[pl.BlockSpec((tm, D_MODEL), row),
                  pl.BlockSpec((tm, D_MODEL), row),
                  pl.BlockSpec((tm, D_MODEL), row),
                  pl.BlockSpec((tm, D_MODEL), lambda i: (i, gate_blk)),
                  pl.BlockSpec((tm, D_MODEL), lambda i: (i, gate_blk + 1)),
                  pl.BlockSpec((tm, D_MODEL), lambda i: (i, gate_blk + 2)),
                  pl.BlockSpec((tm, D_MODEL), row),
                  pl.BlockSpec((3, D_MODEL, D_MODEL), lambda i: (0, 0, 0)),
                  pl.BlockSpec((D_MODEL, D_MODEL), lambda i: (0, 0))],
        out_specs=pl.BlockSpec((tm, D_MODEL), row),
        out_shape=jax.ShapeDtypeStruct((m, D_MODEL), F32),
        compiler_params=pltpu.CompilerParams(
            dimension_semantics=("parallel",), vmem_limit_bytes=VMEM_LIMIT),
        name="merge",
    )(o_gla, o_ml, o_rw, z, z, z, x, wb, wo)


def _ffn_kernel(x_ref, g_ref, w1_ref, w2_ref, fg_ref, o_ref, h_ref, acc_ref, *, final):
    f = pl.program_id(1)

    @pl.when(f == 0)
    def _():
        h_ref[...] = _rms(x_ref[...], g_ref[...]).astype(BF16)
        acc_ref[...] = jnp.zeros_like(acc_ref)

    a = jnp.dot(h_ref[...], w1_ref[...], preferred_element_type=F32)
    a = jnp.square(jnp.maximum(a, 0.0))
    acc_ref[...] += jnp.dot(a.astype(BF16), w2_ref[...], preferred_element_type=F32)

    @pl.when(f == pl.num_programs(1) - 1)
    def _():
        y = x_ref[...] + acc_ref[...]
        if final:
            y = _rms(y, fg_ref[...])
        o_ref[...] = y


def _ffn(x, g, w1, w2, fg, final, tm, tf=1024):
    m = x.shape[0]
    return pl.pallas_call(
        functools.partial(_ffn_kernel, final=final),
        grid=(m // tm, D_FF // tf),
        in_specs=[pl.BlockSpec((tm, D_MODEL), lambda i, f: (i, 0)),
                  pl.BlockSpec((1, D_MODEL), lambda i, f: (0, 0)),
                  pl.BlockSpec((D_MODEL, tf), lambda i, f: (0, f)),
                  pl.BlockSpec((tf, D_MODEL), lambda i, f: (f, 0)),
                  pl.BlockSpec((1, D_MODEL), lambda i, f: (0, 0))],
        out_specs=pl.BlockSpec((tm, D_MODEL), lambda i, f: (i, 0)),
        out_shape=jax.ShapeDtypeStruct((m, D_MODEL), F32),
        scratch_shapes=[pltpu.VMEM((tm, D_MODEL), BF16), pltpu.VMEM((tm, D_MODEL), F32)],
        compiler_params=pltpu.CompilerParams(
            dimension_semantics=("parallel", "arbitrary"), vmem_limit_bytes=VMEM_LIMIT),
        name="ffn",
    )(x, g, w1, w2, fg)


def _rw_tokens(zr, zk, zv, zwa, zg, w0, wup, a0, aup, gup, kkw, kaw):
    u = w0 + _dot(jnp.tanh(zwa), wup)
    lw = _sigmoid(u) * (-math.exp(-0.5))
    a = _sigmoid(a0 + _dot(zwa, aup))
    g = _dot(_sigmoid(zg), gup)
    kk0 = zk * kkw
    kmod = zk * (1.0 + (a - 1.0) * kaw)
    return lw, a, g, kk0, kmod


def _ones_bd():
    r, c = _tri(128)
    return jnp.where((r // RW_HEAD) == (c // RW_HEAD), 1.0, 0.0).astype(BF16)


def _seg(x, ones_bd):
    return _dot2(x, ones_bd)


def _rw_epilogue(y, r, kmod, v, g, rk, ng, ones_bd):
    mean = _seg(y, ones_bd) * (1.0 / RW_HEAD)
    yc = y - mean
    var = _seg(yc * yc, ones_bd) * (1.0 / RW_HEAD)
    yn = yc * lax.rsqrt(var + EPS) * ng
    bonus = _seg(r * kmod * rk, ones_bd) * v
    return ((yn + bonus) * g).astype(BF16)


def _seq_per_step(b_):
    return 2 if b_ % 2 == 0 else 1


def _interleave(gens, pattern):
    live = dict(gens)
    while live:
        for key in pattern:
            g = live.get(key)
            if g is None:
                continue
            try:
                next(g)
            except StopIteration:
                del live[key]


def _gla_chunk(rows, ns, q_ref, k_ref, v_ref, gg_ref, sm_ref, aup_ref, ab_ref, gn_ref, o_ref, s_ref):
    c_ = CHUNK
    r, c = _tri(c_)
    incl = r >= c
    tril = jnp.where(incl, 1.0, 0.0).astype(BF16)
    ones_c = jnp.ones((c_, GLA_DK), BF16)
    tokw = []
    for j in range(ns):
        x = _dot(sm_ref[j, rows, :], aup_ref[...]) + ab_ref[...]
        la = _logsig(x) * (1.0 / GLA_TAU)
        b = _dot2_left(tril, la)
        dec = jnp.exp(_dot2_tn(la, ones_c))
        q = q_ref[j, rows, :] * (GLA_DK ** -0.5)
        k = k_ref[j, rows, :]
        tokw.append((q * jnp.exp(b), k * jnp.exp(-b), k * jnp.exp(b[c_ - 1:c_, :] - b), dec))
    yield
    its = [(j, h) for j in range(ns) for h in range(GLA_HEADS)]
    hs = range(len(its))
    kls = [slice(h * GLA_DK, (h + 1) * GLA_DK) for _, h in its]
    vls = [slice(h * GLA_DV, (h + 1) * GLA_DV) for _, h in its]
    qe = [tokw[j][0][:, kls[i]] for i, (j, _) in enumerate(its)]
    ke = [tokw[j][1][:, kls[i]] for i, (j, _) in enumerate(its)]
    kd = [tokw[j][2][:, kls[i]] for i, (j, _) in enumerate(its)]
    v = [v_ref[j, rows, vls[i]] for i, (j, _) in enumerate(its)]
    att = [jnp.where(incl, _dot_nt(qe[i], ke[i]), 0.0) for i in hs]
    yield
    s = [s_ref[j, h] for j, h in its]
    oi = [_dot(qe[i], s[i]) for i in hs]
    yield
    kv = [_dot_tn(kd[i], v[i]) for i in hs]
    yield
    o = [oi[i] + _dot(att[i], v[i]) for i in hs]
    for i, (j, h) in enumerate(its):
        dh = tokw[j][3][kls[i], :]
        s_ref[j, h] = s[i] * jnp.concatenate([dh, dh], axis=1) + kv[i]
    yield
    msq = [jnp.mean(o[i] * o[i], axis=-1, keepdims=True) for i in hs]
    yield
    for i, (j, _) in enumerate(its):
        on = o[i] * lax.rsqrt(msq[i] + EPS) * gn_ref[...]
        gg = gg_ref[j, rows, vls[i]]
        o_ref[j, rows, vls[i]] = (on * (gg * _sigmoid(gg))).astype(BF16)


def _mlstm_chunk(rows, ns, q_ref, k_ref, v_ref, mo_ref, sm_ref, brow_ref, bcol_ref, ng_ref,
                 o_ref, c_ref, n_ref, m_ref):
    c_ = CHUNK
    r, c = _tri(c_)
    incl = r >= c
    tril = jnp.where(incl, 1.0, 0.0).astype(BF16)
    triu = jnp.where(r <= c, 1.0, 0.0).astype(BF16)
    sub8 = lax.broadcasted_iota(jnp.int32, (8, c_), 0)
    eye = jnp.where(r == c, 1.0, 0.0).astype(BF16)
    sm, fcol, smt, frow = [], [], [], []
    for j in range(ns):
        raw = sm_ref[j, rows, :]
        sm.append(raw + brow_ref[...])
        fcol.append(_dot2_left(tril, _logsig(sm[j])))
        hi, lo = _split(raw)
        lo2 = (raw - hi.astype(F32) - lo.astype(F32)).astype(BF16)
        rawt = sum(lax.dot_general(piece, eye, _TN, preferred_element_type=F32) for piece in (hi, lo, lo2))
        smt.append(rawt[SM_I:SM_I + 8, :] + bcol_ref[...])
        frow.append(_dot2(jnp.where(sub8 >= 4, _logsig(smt[j]), 0.0), triu))
    yield
    its = [(j, h) for j in range(ns) for h in range(ML_HEADS)]
    hs = range(len(its))
    hls = [slice(h * ML_DH, (h + 1) * ML_DH) for _, h in its]
    f_t = [fcol[j][:, SM_F + h:SM_F + h + 1] for j, h in its]
    i_t = [sm[j][:, SM_I + h:SM_I + h + 1] for j, h in its]
    f_s = [frow[j][4 + h:5 + h, :] for j, h in its]
    i_s = [smt[j][h:h + 1, :] for j, h in its]
    m = [m_ref[j, h][0:1, 0:1] for j, h in its]
    q = [q_ref[j, rows, hls[i]] * (ML_DH ** -0.5) for i, (j, _) in enumerate(its)]
    k = [k_ref[j, rows, hls[i]] for i, (j, _) in enumerate(its)]
    v = [v_ref[j, rows, hls[i]] for i, (j, _) in enumerate(its)]
    qk0 = [_dot_nt(q[i], k[i]) for i in hs]
    gmax = [jnp.max(jnp.where(incl, i_s[i] - f_s[i], -jnp.inf), axis=1, keepdims=True) for i in hs]
    yield
    cm = [c_ref[j, h] for j, h in its]
    n8 = [n_ref[j, h] for j, h in its]
    qc = [_dot(q[i], cm[i]) for i in hs]
    qn = [_dot_nt(q[i], n8[i])[:, 0:1] for i in hs]
    yield
    mt = [jnp.maximum(f_t[i] + m[i], f_t[i] + gmax[i]) for i in hs]
    w_inter = [jnp.exp(f_t[i] + m[i] - mt[i]) for i in hs]
    d = [jnp.exp(jnp.where(incl, (f_t[i] - mt[i]) + (i_s[i] - f_s[i]), -jnp.inf)) for i in hs]
    qk = [qk0[i] * d[i] for i in hs]
    qkv = [_dot(qk[i], v[i]) for i in hs]
    yield
    m_new = [mt[i][c_ - 1:c_, :] for i in hs]
    f_last = [f_t[i][c_ - 1:c_, :] for i in hs]
    w_end = [jnp.exp(f_last[i] + m[i] - m_new[i]) for i in hs]
    kws = [k[i] * jnp.exp(f_last[i] - f_t[i] + i_t[i] - m_new[i]) for i in hs]
    kwv = [_dot_tn(kws[i], v[i]) for i in hs]
    yield
    for i, (j, h) in enumerate(its):
        c_ref[j, h] = w_end[i] * cm[i] + kwv[i]
        n_ref[j, h] = w_end[i] * n8[i] + jnp.broadcast_to(jnp.sum(kws[i], axis=0, keepdims=True),
                                                            (8, ML_DH))
        m_ref[j, h] = jnp.broadcast_to(m_new[i], (8, 128))
    den = [w_inter[i] * qn[i] + jnp.sum(qk[i], axis=1, keepdims=True) for i in hs]
    yield
    hm = [(w_inter[i] * qc[i] + qkv[i]) / jnp.maximum(jnp.abs(den[i]), jnp.exp(-mt[i])) for i in hs]
    mu = [jnp.mean(hm[i], axis=-1, keepdims=True) for i in hs]
    yield
    yc = [hm[i] - mu[i] for i in hs]
    var = [jnp.mean(yc[i] * yc[i], axis=-1, keepdims=True) for i in hs]
    yield
    for i, (j, _) in enumerate(its):
        y = yc[i] * lax.rsqrt(var[i] + EPS) * ng_ref[...]
        o_ref[j, rows, hls[i]] = (y * _sigmoid(mo_ref[j, rows, hls[i]])).astype(BF16)


def _rw_chunk(rows, ns, r_ref, k_ref, v_ref, wa_ref, gd_ref, mu_refs, par_refs, o_ref, s_ref, prev_refs):
    mur_ref, muk_ref, muv_ref, muwa_ref, mug_ref = mu_refs
    w0_ref, wup_ref, a0_ref, aup_ref, gup_ref, kkw_ref, kaw_ref, rk_ref, ng_ref = par_refs
    pr_s, pk_s, pv_s, pwa_s, pg_s = prev_refs
    c_ = CHUNK
    r_, c2 = _tri(c_, 128)
    s_idx = c2 % c_
    strict_pk = s_idx < r_
    incl_pk = s_idx <= r_
    eye_pk = jnp.where(s_idx == r_, 1.0, 0.0)
    rr, cc = _tri(c_)
    tril = jnp.where(rr >= cc, 1.0, 0.0).astype(BF16)
    lo_half = lax.broadcasted_iota(jnp.int32, (c_, 128), 1) < RW_HEAD
    r128, c128 = _tri(128)
    bdmask = (r128 // RW_HEAD) == (c128 // RW_HEAD)
    ones_bd = jnp.where(bdmask, 1.0, 0.0).astype(BF16)
    first_w = lax.broadcasted_iota(jnp.int32, (c_, RW_W), 0) == 0
    first_n = lax.broadcasted_iota(jnp.int32, (c_, 128), 0) == 0

    def mix(x_ref, prev_ref, mu_ref, j):
        x = x_ref[j, rows, :]
        first = first_w if x.shape[1] == RW_W else first_n
        prev = jnp.where(first, prev_ref[j], pltpu.roll(x, 1, 0))
        prev_ref[j] = x[c_ - 1:c_, :]
        return x + (prev - x) * mu_ref[...]

    def bd(y):
        yb = y.astype(BF16)
        zero = jnp.zeros_like(yb)
        return jnp.concatenate([jnp.where(lo_half, yb, zero), jnp.where(lo_half, zero, yb)], axis=0)

    def pk_mul(x, y):
        return jnp.dot(x.astype(BF16), bd(y), preferred_element_type=F32)

    def seg_all(xs):
        st = jnp.dot(jnp.concatenate(xs, axis=0).astype(BF16), ones_bd, preferred_element_type=F32)
        return [st[i * c_:(i + 1) * c_] for i in range(len(xs))]

    zr = [mix(r_ref, pr_s, mur_ref, j) for j in range(ns)]
    zv = [mix(v_ref, pv_s, muv_ref, j) for j in range(ns)]
    tok = [_rw_tokens(zr[j], mix(k_ref, pk_s, muk_ref, j), zv[j],
                      mix(wa_ref, pwa_s, muwa_ref, j), mix(gd_ref, pg_s, mug_ref, j),
                      w0_ref[...], wup_ref[...], a0_ref[...], aup_ref[...],
                      gup_ref[...], kkw_ref[...], kaw_ref[...]) for j in range(ns)]
    yield
    its = [(j, p) for j in range(ns) for p in range(RW_PAIRS)]
    prs = range(len(its))
    lns = [slice(p * 128, (p + 1) * 128) for _, p in its]
    lw = [tok[j][0][:, lns[i]] for i, (j, _) in enumerate(its)]
    a = [tok[j][1][:, lns[i]] for i, (j, _) in enumerate(its)]
    g = [tok[j][2][:, lns[i]] for i, (j, _) in enumerate(its)]
    kkp = [tok[j][3][:, lns[i]] for i, (j, _) in enumerate(its)]
    kp = [tok[j][4][:, lns[i]] for i, (j, _) in enumerate(its)]
    rp = [zr[j][:, lns[i]] for i, (j, _) in enumerate(its)]
    vp = [zv[j][:, lns[i]] for i, (j, _) in enumerate(its)]
    ssq = seg_all([x * x for x in kkp])
    cumf = [_dot2_left(tril, tok[j][0]) for j in range(ns)]
    yield
    cum = [cumf[j][:, lns[i]] for i, (j, _) in enumerate(its)]
    kkn = [kkp[p] * lax.rsqrt(ssq[p] + 1e-12) for p in prs]
    bp = [kkn[p] * a[p] for p in prs]
    cl = [cum[p][c_ - 1:c_, :] for p in prs]
    e_neg = [jnp.exp(-cum[p]) for p in prs]
    e_end = [jnp.exp(cl[p] - cum[p]) for p in prs]
    rh = [rp[p] * jnp.exp(cum[p]) for p in prs]
    kh = [kkn[p] * jnp.exp(cum[p] - lw[p]) for p in prs]
    kt = [kp[p] * e_neg[p] for p in prs]
    bt = [bp[p] * e_neg[p] for p in prs]
    kg = [kp[p] * e_end[p] for p in prs]
    bg = [bp[p] * e_end[p] for p in prs]
    kr = [jnp.concatenate([kh[p], rh[p]], axis=0).astype(BF16) for p in prs]
    ktb = [jnp.concatenate([bd(kt[p]), bd(bt[p])], axis=0) for p in prs]
    pkb = [lax.dot_general(kr[p], ktb[p], _NT, preferred_element_type=F32) for p in prs]
    yield
    ak = [jnp.where(strict_pk, pkb[p][0:c_, 0:128], 0.0) for p in prs]
    gk = [jnp.where(incl_pk, pkb[p][c_:2 * c_, 0:128], 0.0) for p in prs]
    gb = [jnp.where(incl_pk, pkb[p][c_:2 * c_, 128:256], 0.0) for p in prs]
    pw = [jnp.where(strict_pk, -pkb[p][0:c_, 128:256], 0.0) for p in prs]
    tm = [eye_pk + pw[p] for p in prs]
    agv = [pk_mul(jnp.concatenate([ak[p], gk[p]], axis=0), vp[p]) for p in prs]
    pw = [pk_mul(pw[p], pw[p]) for p in prs]
    yield
    for _ in range(4):
        x = [pk_mul(jnp.concatenate([pw[p], tm[p]], axis=0), pw[p]) for p in prs]
        tm = [tm[p] + x[p][c_:2 * c_] for p in prs]
        pw = [x[p][0:c_] for p in prs]
        yield
    tm = [tm[p] + pk_mul(tm[p], pw[p]) for p in prs]
    yield
    ku = [jnp.dot(tm[p].astype(BF16), jnp.concatenate([bd(kh[p]), bd(agv[p][0:c_])], axis=1),
                  preferred_element_type=F32) for p in prs]
    yield
    gku = [jnp.dot(gb[p].astype(BF16),
                   jnp.concatenate([bd(ku[p][:, 0:128]), bd(ku[p][:, 128:256])], axis=1),
                   preferred_element_type=F32) for p in prs]
    d = [_dot_tn(jnp.concatenate([vp[p], -ku[p][:, 128:256]], axis=0),
                 jnp.concatenate([kg[p], bg[p]], axis=0)) for p in prs]
    yield
    rhp = [rh[p] - gku[p][:, 0:128] for p in prs]
    y0 = [agv[p][c_:2 * c_] - gku[p][:, 128:256] for p in prs]
    s = [s_ref[j, p] for j, p in its]
    uy = [_dot_nt(jnp.concatenate([ku[p][:, 0:128], rhp[p]], axis=0), s[p]) for p in prs]
    yield
    ub = [_dot_tn(uy[p][0:c_], bg[p]) for p in prs]
    y = [uy[p][c_:2 * c_] + y0[p] for p in prs]
    mean = seg_all(y)
    yield
    for i, (j, p) in enumerate(its):
        s_ref[j, p] = s[i] * jnp.exp(cl[i]) + jnp.where(bdmask, d[i] - ub[i], 0.0)
    yc = [y[p] - mean[p] * (1.0 / RW_HEAD) for p in prs]
    var = seg_all([x * x for x in yc])
    bon = seg_all([rp[p] * kp[p] * rk_ref[:, lns[p]] for p in prs])
    yield
    for i, (j, _) in enumerate(its):
        yn = yc[i] * lax.rsqrt(var[i] * (1.0 / RW_HEAD) + EPS) * ng_ref[:, lns[i]]
        o_ref[j, rows, lns[i]] = ((yn + bon[i] * vp[i]) * g[i]).astype(BF16)


def _mixers_kernel(*refs, n_chunks, ns):
    (gq, gk, gv, gg, sm, mq, mk, mv, mo, rr, rk, rv, rwa, rgd) = refs[0:14]
    aup, ab, gn, brow, bcol, mng = refs[14:20]
    mu_refs = refs[20:25]
    par_refs = refs[25:34]
    o_gla, o_ml, o_rw, gla_out, c_out, n_out, m_out, rw_out = refs[34:42]
    s_gla, c_s, n_s, m_s, s_rw = refs[42:47]
    prev_refs = refs[47:52]
    t = pl.program_id(1)

    @pl.when(t == 0)
    def _():
        for ref in (s_gla, c_s, n_s, m_s, s_rw) + tuple(prev_refs):
            ref[...] = jnp.zeros_like(ref)

    def chunk(ci, carry):
        rows = pl.ds(pl.multiple_of(ci * CHUNK, CHUNK), CHUNK)
        _interleave(
            {"r": _rw_chunk(rows, ns, rr, rk, rv, rwa, rgd, mu_refs, par_refs, o_rw, s_rw, prev_refs),
             "g": _gla_chunk(rows, ns, gq, gk, gv, gg, sm, aup, ab, gn, o_gla, s_gla),
             "m": _mlstm_chunk(rows, ns, mq, mk, mv, mo, sm, brow, bcol, mng, o_ml, c_s, n_s, m_s)},
            "rmrg")
        return carry

    lax.fori_loop(0, n_chunks, chunk, 0)

    @pl.when(t == pl.num_programs(1) - 1)
    def _():
        gla_out[...] = s_gla[...]
        c_out[...] = c_s[...]
        for j in range(ns):
            for h in range(ML_HEADS):
                n_out[j, h:h + 1, :] = n_s[j, h][0:1, :]
                m_out[j, h:h + 1, :] = m_s[j, h][0:1, :]
            for p in range(RW_PAIRS):
                s = s_rw[j, p]
                rw_out[j, 2 * p] = s[0:RW_HEAD, 0:RW_HEAD]
                rw_out[j, 2 * p + 1] = s[RW_HEAD:, RW_HEAD:]


def _mixers_prompt(z, lp, b_, t_, tt):
    ns = _seq_per_step(b_)
    nc = tt // CHUNK
    z3 = z.reshape(b_, t_, NP)
    zblk = lambda w, off: pl.BlockSpec((ns, tt, w), lambda b, t, c=off // w: (b, t, c))
    const = lambda b, t: (0, 0)
    full = lambda a: pl.BlockSpec(a.shape, const)
    seq = lambda *shape: pl.BlockSpec((ns,) + shape, lambda b, t, n=len(shape): (b,) + (0,) * n)
    params = [lp["aup_pad"], lp["ab"], lp["gla_ng"], lp["brow"], lp["bcol"], lp["ml_ng"], *lp["mus"], *lp["rw"]]
    bf = jax.ShapeDtypeStruct((b_, t_, D_MODEL), BF16)
    outs = pl.pallas_call(
        functools.partial(_mixers_kernel, n_chunks=nc, ns=ns),
        grid=(b_ // ns, t_ // tt),
        in_specs=[zblk(GLA_QK, OFF_GQ), zblk(GLA_QK, OFF_GK), zblk(GLA_V, OFF_GV), zblk(GLA_V, OFF_GG),
                  zblk(128, OFF_SM),
                  zblk(ML_W, OFF_MQ), zblk(ML_W, OFF_MK), zblk(ML_W, OFF_MV), zblk(ML_W, OFF_MO),
                  zblk(RW_W, OFF_RW), zblk(RW_W, OFF_RW + RW_W), zblk(RW_W, OFF_RW + 2 * RW_W),
                  zblk(128, OFF_RW + 3 * RW_W), zblk(128, OFF_RW + 3 * RW_W + 128)]
                 + [full(a) for a in params],
        out_specs=[pl.BlockSpec((ns, tt, D_MODEL), lambda b, t: (b, t, 0))] * 3
                  + [seq(GLA_HEADS, GLA_DK, GLA_DV), seq(ML_HEADS, ML_DH, ML_DH), seq(ML_HEADS, ML_DH),
                     seq(ML_HEADS, 128), seq(RW_HEADS, RW_HEAD, RW_HEAD)],
        out_shape=[bf, bf, bf,
                   jax.ShapeDtypeStruct((b_, GLA_HEADS, GLA_DK, GLA_DV), F32),
                   jax.ShapeDtypeStruct((b_, ML_HEADS, ML_DH, ML_DH), F32),
                   jax.ShapeDtypeStruct((b_, ML_HEADS, ML_DH), F32),
                   jax.ShapeDtypeStruct((b_, ML_HEADS, 128), F32),
                   jax.ShapeDtypeStruct((b_, RW_HEADS, RW_HEAD, RW_HEAD), F32)],
        scratch_shapes=[pltpu.VMEM((ns, GLA_HEADS, GLA_DK, GLA_DV), F32),
                        pltpu.VMEM((ns, ML_HEADS, ML_DH, ML_DH), F32),
                        pltpu.VMEM((ns, ML_HEADS, 8, ML_DH), F32),
                        pltpu.VMEM((ns, ML_HEADS, 8, 128), F32),
                        pltpu.VMEM((ns, RW_PAIRS, 128, 128), F32),
                        pltpu.VMEM((ns, 1, RW_W), F32), pltpu.VMEM((ns, 1, RW_W), F32),
                        pltpu.VMEM((ns, 1, RW_W), F32), pltpu.VMEM((ns, 1, 128), F32),
                        pltpu.VMEM((ns, 1, 128), F32)],
        compiler_params=pltpu.CompilerParams(
            dimension_semantics=("parallel", "arbitrary"), vmem_limit_bytes=MIXER_VMEM_LIMIT),
        name="mixers_prompt",
    )(*([z3] * 14), *params)
    o_gla, o_ml, o_rw, gla_new, c_new, n_new, m_new, rw_new = outs
    flat = lambda o: o.reshape(b_ * t_, D_MODEL)
    return flat(o_gla), flat(o_ml), flat(o_rw), (gla_new, c_new, n_new, m_new[:, :, 0], rw_new)


def _sample_prep_kernel(sm_ref, r_ref, k_ref, v_ref, wa_ref, gd_ref, m_ref, shift_ref,
                        aup_ref, ab_ref, ib_ref, fb_ref,
                        mur_ref, muk_ref, muv_ref, muwa_ref, mug_ref,
                        w0_ref, wup_ref, a0_ref, aup2_ref, gup_ref, kkw_ref, kaw_ref,
                        gla_la_ref, mnew_ref, wend_ref, ws_ref,
                        rr_ref, rk_ref, rv_ref, rg_ref, kkt_ref, bt_ref, kt_ref, rt_ref, wt_ref, vt_ref):
    sm = sm_ref[...]
    gla_la_ref[...] = _logsig(_dot(sm, aup_ref[...]) + ab_ref[...]) * (1.0 / GLA_TAU)
    i_pre = sm[:, SM_I:SM_I + ML_HEADS] + ib_ref[...]
    logf = _logsig(sm[:, SM_F:SM_F + ML_HEADS] + fb_ref[...])
    m = m_ref[0]
    m_new = jnp.maximum(logf + m, i_pre)
    mnew_ref[...] = m_new
    wend_ref[...] = jnp.exp(logf + m - m_new)
    ws_ref[...] = jnp.exp(i_pre - m_new)
    mixf = lambda x, off, mu: x[...] + (shift_ref[0, :, off:off + x.shape[1]] - x[...]) * mu[...]
    zr = mixf(r_ref, 0, mur_ref)
    zk = mixf(k_ref, RW_W, muk_ref)
    zv = mixf(v_ref, 2 * RW_W, muv_ref)
    zwa = mixf(wa_ref, 3 * RW_W, muwa_ref)
    zg = mixf(gd_ref, 3 * RW_W + 128, mug_ref)
    lw, a, g, kk0, kmod = _rw_tokens(zr, zk, zv, zwa, zg, w0_ref[...], wup_ref[...], a0_ref[...],
                                     aup2_ref[...], gup_ref[...], kkw_ref[...], kaw_ref[...])
    ones_bd = _ones_bd()
    for p in range(RW_PAIRS):
        ln = slice(p * 128, (p + 1) * 128)
        kkp = kk0[:, ln]
        kkn = kkp * lax.rsqrt(_seg(kkp * kkp, ones_bd) + 1e-12)
        kkt_ref[ln, :] = kkn.T
        bt_ref[ln, :] = (kkn * a[:, ln]).T
    rr_ref[...] = zr
    rk_ref[...] = kmod
    rv_ref[...] = zv
    rg_ref[...] = g
    kt_ref[...] = kmod.T
    rt_ref[...] = zr.T
    wt_ref[...] = jnp.exp(lw).T
    vt_ref[...] = zv.T


STEP_BB = 8


def _row_select(rows_list, rid):
    acc = rows_list[0]
    for b in range(1, len(rows_list)):
        acc = jnp.where(rid == b, rows_list[b], acc)
    return acc


def _gla_step_kernel(s_ref, q_ref, k_ref, la_ref, v_ref, *rest):
    sn_ref, o_ref = rest[-2:]
    rid_k = lax.broadcasted_iota(jnp.int32, (STEP_BB, GLA_DK), 0)
    rid_v = lax.broadcasted_iota(jnp.int32, (STEP_BB, GLA_DV), 0)
    bs = range(STEP_BB)
    onehot = [jnp.where(rid_v == b, 1.0, 0.0).astype(BF16) for b in bs]
    for h in range(GLA_HEADS):
        kl = slice(h * GLA_DK, (h + 1) * GLA_DK)
        vl = slice(h * GLA_DV, (h + 1) * GLA_DV)
        qt = q_ref[:, kl] * (GLA_DK ** -0.5)
        kt = k_ref[:, kl]
        lt = la_ref[:, kl]
        vt = v_ref[:, vl]
        dec = [jnp.exp(_dot2_tn(lt, onehot[b])) for b in bs]
        kv = [_dot3(jnp.where(rid_k == b, kt, 0.0), vt, _TN) for b in bs]
        sn = [dec[b] * s_ref[0, b, h] + kv[b] for b in bs]
        for b in bs:
            sn_ref[0, b, h] = sn[b]
        o_ref[:, vl] = _row_select([_dot3(qt, sn[b]) for b in bs], rid_v)


def _mlstm_step_kernel(c_ref, n_ref, q_ref, k_ref, v_ref, wend_ref, ws_ref, mnew_ref, *rest):
    cn_ref, nn_ref, h_ref = rest[-3:]
    rid = lax.broadcasted_iota(jnp.int32, (STEP_BB, ML_DH), 0)
    for h in range(ML_HEADS):
        hl = slice(h * ML_DH, (h + 1) * ML_DH)
        qt = q_ref[:, hl] * (ML_DH ** -0.5)
        kt = k_ref[:, hl]
        vt = v_ref[:, hl]
        wend = wend_ref[:, h:h + 1]
        ws = ws_ref[:, h:h + 1]
        nn = wend * n_ref[0, :, hl] + ws * kt
        nn_ref[0, :, hl] = nn
        den = jnp.sum(qt * nn, axis=1, keepdims=True)
        kws = kt * ws
        nums = []
        for half in range(2):
            bs = range(half * 4, half * 4 + 4)
            kwv = [_dot3(jnp.where(rid == b, kws, 0.0), vt, _TN) for b in bs]
            cn = [wend_ref[b:b + 1, h:h + 1] * c_ref[0, b, h] + kwv[i] for i, b in enumerate(bs)]
            for i, b in enumerate(bs):
                cn_ref[0, b, h] = cn[i]
            nums += [_dot3(qt, cn[i]) for i in range(4)]
        num = _row_select(nums, rid)
        h_ref[:, hl] = num / jnp.maximum(jnp.abs(den), jnp.exp(-mnew_ref[:, h:h + 1]))


RW_STEP_HB = 2


def _rw_step_kernel(s_ref, kk_ref, b_ref, k_ref, r_ref, w_ref, v_ref, *rest):
    sn_ref, y_ref = rest[-2:]
    n = RW_HEAD
    for hh in range(RW_STEP_HB):
        fl = slice(hh * n, (hh + 1) * n)
        nkk, bb, kt, rt, wt = -kk_ref[fl, :], b_ref[fl, :], k_ref[fl, :], r_ref[fl, :], w_ref[fl, :]

        def row(i, carry):
            s = s_ref[0, hh, i]
            sa = jnp.sum(s * nkk, axis=0, keepdims=True)
            sn = s * wt + sa * bb + v_ref[pl.ds(hh * n + i, 1), :] * kt
            sn_ref[0, hh, i] = sn
            y_ref[pl.ds(hh * n + i, 1), :] = jnp.sum(sn * rt, axis=0, keepdims=True)
            return carry

        lax.fori_loop(0, n, row, 0, unroll=4)


def _rw_state_call(l, state_t, vecs_t, prev_out):
    nb = state_t.shape[-1]
    st_spec = pl.BlockSpec((1, RW_STEP_HB, RW_HEAD, RW_HEAD, nb), lambda h: (l, h, 0, 0, 0))
    vec_spec = pl.BlockSpec((RW_STEP_HB * RW_HEAD, nb), lambda h: (h, 0))
    args = [state_t] + list(vecs_t)
    in_specs = [st_spec] + [vec_spec] * len(vecs_t)
    aliases = {}
    if prev_out is not None:
        aliases[len(args)] = 0
        in_specs.append(pl.BlockSpec(memory_space=pl.ANY))
        args.append(prev_out)
    return pl.pallas_call(
        _rw_step_kernel,
        grid=(RW_HEADS // RW_STEP_HB,),
        in_specs=in_specs,
        out_specs=[st_spec, vec_spec],
        out_shape=[jax.ShapeDtypeStruct(state_t.shape, F32), jax.ShapeDtypeStruct((RW_W, nb), F32)],
        input_output_aliases=aliases,
        compiler_params=pltpu.CompilerParams(
            dimension_semantics=("parallel",), vmem_limit_bytes=VMEM_LIMIT),
        name="rwkv_step",
    )(*args)


def _sample_post_kernel(og_ref, gg_ref, gn_ref, hm_ref, mo_ref, mn_ref,
                        yt_ref, r_ref, k_ref, v_ref, g_ref, rk_ref, ng_ref,
                        ogla_ref, oml_ref, orw_ref):
    for h in range(GLA_HEADS):
        vl = slice(h * GLA_DV, (h + 1) * GLA_DV)
        o = og_ref[:, vl]
        on = o * lax.rsqrt(jnp.mean(o * o, axis=-1, keepdims=True) + EPS) * gn_ref[...]
        gg = gg_ref[:, vl]
        ogla_ref[:, vl] = (on * (gg * _sigmoid(gg))).astype(BF16)
    for h in range(ML_HEADS):
        hl = slice(h * ML_DH, (h + 1) * ML_DH)
        hm = hm_ref[:, hl]
        y = hm - jnp.mean(hm, axis=-1, keepdims=True)
        y = y * lax.rsqrt(jnp.mean(y * y, axis=-1, keepdims=True) + EPS) * mn_ref[...]
        oml_ref[:, hl] = (y * _sigmoid(mo_ref[:, hl])).astype(BF16)
    ones_bd = _ones_bd()
    for p in range(RW_PAIRS):
        ln = slice(p * 128, (p + 1) * 128)
        y = yt_ref[ln, :].T
        orw_ref[:, ln] = _rw_epilogue(y, r_ref[:, ln], k_ref[:, ln], v_ref[:, ln],
                                      g_ref[:, ln], rk_ref[:, ln], ng_ref[:, ln], ones_bd)


def _full_call(kernel_fn, args, out_shapes, name):
    nd = lambda a: (lambda i: (0,) * len(a.shape))
    in_specs, arrays = [], []
    for a in args:
        if isinstance(a, tuple):
            arr, blk, idx = a
            in_specs.append(pl.BlockSpec(blk, lambda i, idx=idx: idx))
            arrays.append(arr)
        else:
            in_specs.append(pl.BlockSpec(a.shape, nd(a)))
            arrays.append(a)
    return pl.pallas_call(
        kernel_fn,
        grid=(1,),
        in_specs=in_specs,
        out_specs=[pl.BlockSpec(s.shape, nd(s)) for s in out_shapes],
        out_shape=out_shapes,
        compiler_params=pltpu.CompilerParams(vmem_limit_bytes=VMEM_LIMIT),
        name=name,
    )(*arrays)


def _state_call(kernel_fn, l, states, vecs, prev_outs, out_width, name):
    nb = states[0].shape[1]
    st_spec = lambda a: pl.BlockSpec((1, STEP_BB) + tuple(a.shape[2:]),
                                     lambda i, n=a.ndim: (l, i) + (0,) * (n - 2))
    in_specs = [st_spec(a) for a in states]
    in_specs += [pl.BlockSpec((STEP_BB, w), lambda i, c=c: (i, c)) for _, w, c in vecs]
    args = list(states) + [a for a, _, _ in vecs]
    aliases = {}
    for j, po in enumerate(prev_outs or ()):
        aliases[len(args)] = j
        in_specs.append(pl.BlockSpec(memory_space=pl.ANY))
        args.append(po)
    out_shape = [jax.ShapeDtypeStruct(a.shape, F32) for a in states]
    out_shape.append(jax.ShapeDtypeStruct((nb, out_width), F32))
    out_specs = [st_spec(a) for a in states] + [pl.BlockSpec((STEP_BB, out_width), lambda i: (i, 0))]
    return pl.pallas_call(
        kernel_fn,
        grid=(nb // STEP_BB,),
        in_specs=in_specs,
        out_specs=out_specs,
        out_shape=out_shape,
        input_output_aliases=aliases,
        compiler_params=pltpu.CompilerParams(
            dimension_semantics=("parallel",), vmem_limit_bytes=VMEM_LIMIT),
        name=name,
    )(*args)


W_IN_MOVES = ((0, 0, 2048), (2064, 2048, 1024), (3088, 3072, 3072), (6168, 6144, 1024), (7192, 7168, 6400))
D_IN = 13592
W_PREP_LANES = 128


def _w_in_prep_kernel(w_ref, o_ref):
    for src, dst, width in W_IN_MOVES:
        o_ref[0, dst:dst + width, :] = w_ref[0, src:src + width, :].astype(BF16)
    small = jnp.concatenate([w_ref[0, 2048:2064, :], w_ref[0, 6160:6168, :],
                             jnp.zeros((NP - OFF_SM - 24, W_PREP_LANES), F32)], axis=0)
    o_ref[0, OFF_SM:NP, :] = small.astype(BF16)


def _w_in_prep(w_in):
    depth = w_in.shape[0]
    w_t = jnp.transpose(w_in, (0, 2, 1))
    return pl.pallas_call(
        _w_in_prep_kernel,
        grid=(depth, D_MODEL // W_PREP_LANES),
        in_specs=[pl.BlockSpec((1, D_IN, W_PREP_LANES), lambda l, i: (l, 0, i))],
        out_specs=pl.BlockSpec((1, NP, W_PREP_LANES), lambda l, i: (l, 0, i)),
        out_shape=jax.ShapeDtypeStruct((depth, NP, D_MODEL), BF16),
        compiler_params=pltpu.CompilerParams(
            dimension_semantics=("parallel", "parallel"), vmem_limit_bytes=VMEM_LIMIT),
        name="w_in_prep",
    )(w_t)


def _prep_layer_params(l, p):
    w_in_p = (p["w_in_t"], l)
    aup_pad = jnp.zeros((128, GLA_QK), F32).at[0:GLA_RANK].set(p["gla_a_up"][l])
    brow = (jnp.zeros((1, 128), F32).at[0, SM_I:SM_I + 4].set(p["ml_i_bias"][l])
            .at[0, SM_F:SM_F + 4].set(p["ml_f_bias"][l]))
    bcol = jnp.broadcast_to(jnp.concatenate([p["ml_i_bias"][l], p["ml_f_bias"][l]])[:, None], (8, CHUNK))
    mu = p["rw_mu"][l][None, :]
    mus = (mu[:, 0:1024], mu[:, 1024:2048], mu[:, 2048:3072], mu[:, 3072:3200], mu[:, 3200:3328])
    wup_pad = jnp.zeros((128, RW_W), F32).at[0:64].set(p["rw_w_up"][l])
    aup2_pad = jnp.zeros((128, RW_W), F32).at[64:128].set(p["rw_a_up"][l])
    row = lambda a: a.reshape(1, -1)
    rw_params = (row(p["rw_w0"][l]), wup_pad, row(p["rw_a0"][l]), aup2_pad, p["rw_g_up"][l],
                 row(p["rw_k_k"][l]), row(p["rw_k_a"][l]), row(p["rw_r_k"][l]), row(p["rw_norm_g"][l]))
    return dict(
        norm1_g=row(p["norm1_g"][l]), w_in=w_in_p, aup_pad=aup_pad, ab=row(p["gla_a_bias"][l]),
        gla_ng=row(p["gla_norm_g"][l]), brow=brow, bcol=bcol, ib=row(p["ml_i_bias"][l]),
        fb=row(p["ml_f_bias"][l]), ml_ng=row(p["ml_norm_g"][l]), mus=mus, rw=rw_params,
        wb=p["w_branch"][l].astype(BF16), wo=p["w_out"][l].astype(BF16), norm2_g=row(p["norm2_g"][l]),
        w1=p["w_ff1"][l].astype(BF16), w2=p["w_ff2"][l].astype(BF16))


def _prompt_layer(x, lp, fg, final, b_, t_):
    rows = x.shape[0]
    tm = min(512, rows)
    tm_big = min(1024, rows)
    z = _proj_in(x, lp["norm1_g"], lp["w_in"], tm_big)
    o_gla, o_ml, o_rw, new_states = _mixers_prompt(z, lp, b_, t_, min(MIXER_TT, t_))
    x = _merge(o_gla, o_ml, o_rw, z, x, lp["wb"], lp["wo"], tm)
    x = _ffn(x, lp["norm2_g"], lp["w1"], lp["w2"], fg, final, tm_big)
    shift = z.reshape(b_, t_, NP)[:, t_ - 1, OFF_RW:OFF_RW + RW_COLS]
    return x, new_states + (shift,)


def _sample_layer(l, x, states, prev, lp, fg, final):
    st_gla, st_c, st_n, st_m, st_rw, st_shift = states
    nb = x.shape[0]
    z = _proj_in(x, lp["norm1_g"], lp["w_in"], nb)
    zw = lambda off, w: (z, (nb, w), (0, off // w))
    w0, wup, a0, aup2, gup, kkw, kaw, rk, ng = lp["rw"]
    sds = lambda *s: jax.ShapeDtypeStruct(s, F32)
    prep_args = [zw(OFF_SM, 128), zw(OFF_RW, RW_W), zw(OFF_RW + RW_W, RW_W), zw(OFF_RW + 2 * RW_W, RW_W),
                 zw(OFF_RW + 3 * RW_W, 128), zw(OFF_RW + 3 * RW_W + 128, 128),
                 (st_m, (1, nb, ML_HEADS), (l, 0, 0)), (st_shift, (1, nb, RW_COLS), (l, 0, 0)),
                 lp["aup_pad"], lp["ab"], lp["ib"], lp["fb"], *lp["mus"],
                 w0, wup, a0, aup2, gup, kkw, kaw]
    (la, m_new, wend, ws, rr, rkm, rv, rg, kkt, bt, kt, rt, wt, vt) = _full_call(
        _sample_prep_kernel, prep_args,
        [sds(nb, GLA_QK), sds(nb, 4), sds(nb, 4), sds(nb, 4)] + [sds(nb, RW_W)] * 4 + [sds(RW_W, nb)] * 6,
        "sample_prep")

    pv = lambda *idx: None if prev is None else [prev[i] for i in idx]
    gla_all, o_g = _state_call(
        _gla_step_kernel, l, [st_gla],
        [(z, GLA_QK, OFF_GQ // GLA_QK), (z, GLA_QK, OFF_GK // GLA_QK), (la, GLA_QK, 0),
         (z, GLA_V, OFF_GV // GLA_V)], pv(0), GLA_V, "gla_step")
    c_all, n_all, hm = _state_call(
        _mlstm_step_kernel, l, [st_c, st_n],
        [(z, ML_W, OFF_MQ // ML_W), (z, ML_W, OFF_MK // ML_W), (z, ML_W, OFF_MV // ML_W),
         (wend, 4, 0), (ws, 4, 0), (m_new, 4, 0)], pv(1, 2), ML_W, "mlstm_step")
    rw_all, y = _rw_state_call(l, st_rw, (kkt, bt, kt, rt, wt, vt), None if prev is None else prev[3])

    bf = lambda: jax.ShapeDtypeStruct((nb, D_MODEL), BF16)
    o_gla, o_ml, o_rw = _full_call(
        _sample_post_kernel,
        [o_g, zw(OFF_GG, GLA_V), lp["gla_ng"], hm, zw(OFF_MO, ML_W), lp["ml_ng"],
         y, rr, rkm, rv, rg, rk, ng],
        [bf(), bf(), bf()], "sample_post")
    x = _merge(o_gla, o_ml, o_rw, z, x, lp["wb"], lp["wo"], nb)
    x = _ffn(x, lp["norm2_g"], lp["w1"], lp["w2"], fg, final, nb)
    return x, (gla_all, c_all, n_all, rw_all), (m_new, z[:, OFF_RW:OFF_RW + RW_COLS])


def kernel(x_prompt, x_sample, state_gla, state_mlstm_c, state_mlstm_n, state_mlstm_m, state_rwkv, state_shift, norm1_g, w_in, gla_a_up, gla_a_bias, gla_norm_g, ml_i_bias, ml_f_bias, ml_norm_g, rw_mu, rw_w0, rw_w_up, rw_a0, rw_a_up, rw_g_up, rw_k_k, rw_k_a, rw_r_k, rw_norm_g, w_branch, w_out, norm2_g, w_ff1, w_ff2, final_g):
    p = dict(norm1_g=norm1_g, w_in=w_in, gla_a_up=gla_a_up, gla_a_bias=gla_a_bias, gla_norm_g=gla_norm_g,
             ml_i_bias=ml_i_bias, ml_f_bias=ml_f_bias, ml_norm_g=ml_norm_g, rw_mu=rw_mu, rw_w0=rw_w0,
             rw_w_up=rw_w_up, rw_a0=rw_a0, rw_a_up=rw_a_up, rw_g_up=rw_g_up, rw_k_k=rw_k_k, rw_k_a=rw_k_a,
             rw_r_k=rw_r_k, rw_norm_g=rw_norm_g, w_branch=w_branch, w_out=w_out, norm2_g=norm2_g,
             w_ff1=w_ff1, w_ff2=w_ff2)
    depth = w_in.shape[0]
    p["w_in_t"] = _w_in_prep(w_in)
    b_, t_, _ = x_prompt.shape
    nb = x_sample.shape[0]
    fg = final_g.reshape(1, D_MODEL)
    xp = x_prompt.reshape(b_ * t_, D_MODEL)
    xs = x_sample.reshape(nb, D_MODEL)
    s_in = (state_gla, state_mlstm_c, state_mlstm_n.reshape(depth, nb, ML_W), state_mlstm_m,
            jnp.transpose(state_rwkv, (0, 2, 3, 4, 1)), state_shift)
    p_states, s_big, s_small = [], None, []
    for l in range(depth):
        lp = _prep_layer_params(l, p)
        final = l == depth - 1
        xp, stp = _prompt_layer(xp, lp, fg, final, b_, t_)
        xs, s_big, small = _sample_layer(l, xs, s_in, s_big, lp, fg, final)
        p_states.append(stp)
        s_small.append(small)
    ps = [jnp.stack([s[i] for s in p_states]) for i in range(6)]
    gla_s, mlc_s, mln_s, rw_s = s_big
    rw_s = jnp.transpose(rw_s, (0, 4, 1, 2, 3))
    mln_s = mln_s.reshape(depth, nb, ML_HEADS, ML_DH)
    mlm_s = jnp.stack([s[0] for s in s_small])
    shift_s = jnp.stack([s[1] for s in s_small])
    return (xp.reshape(b_, t_, D_MODEL), xs.reshape(nb, 1, D_MODEL),
            ps[0], gla_s, ps[1], mlc_s, ps[2], mln_s, ps[3], mlm_s, ps[4], rw_s, ps[5], shift_s)
```

```python
import functools
import math

import jax
import jax.numpy as jnp
from jax import lax
from jax.experimental import pallas as pl
from jax.experimental.pallas import tpu as pltpu

F32 = jnp.float32
BF16 = jnp.bfloat16

D_MODEL = 1024
GLA_HEADS, GLA_DK, GLA_DV = 4, 128, 256
GLA_QK, GLA_V, GLA_RANK, GLA_TAU = 512, 1024, 16, 16.0
ML_HEADS, ML_DH, ML_W = 4, 256, 1024
RW_HEAD, RW_W, RW_HEADS = 64, 1024, 16
RW_PAIRS = RW_HEADS // 2
RW_COLS = 3 * RW_W + 64 + 64 + 128
D_FF = 4 * D_MODEL
EPS = 1e-6
CHUNK = 64

OFF_GQ, OFF_GK, OFF_GV, OFF_GG = 0, 512, 1024, 2048
OFF_MQ, OFF_MK, OFF_MV, OFF_MO = 3072, 4096, 5120, 6144
OFF_GATE = 7168
OFF_RW = 10240
OFF_SM = 13568
NP = 13824
SM_I, SM_F = 16, 20

VMEM_LIMIT = 48 * 1024 * 1024
MIXER_VMEM_LIMIT = 56 * 1024 * 1024
MIXER_TT = 128


def _dot(a, b):
    return jnp.dot(a.astype(BF16), b.astype(BF16), preferred_element_type=F32)


def _dot_nt(a, b):
    return lax.dot_general(a.astype(BF16), b.astype(BF16), (((1,), (1,)), ((), ())),
                           preferred_element_type=F32)


def _dot_tn(a, b):
    return lax.dot_general(a.astype(BF16), b.astype(BF16), (((0,), (0,)), ((), ())),
                           preferred_element_type=F32)


def _split(x):
    hi = x.astype(BF16)
    lo = (x - hi.astype(F32)).astype(BF16)
    return hi, lo


def _dot2(x, w_exact):
    hi, lo = _split(x)
    return (jnp.dot(hi, w_exact, preferred_element_type=F32)
            + jnp.dot(lo, w_exact, preferred_element_type=F32))


def _dot2_tn(x, w_exact):
    hi, lo = _split(x)
    dn = (((0,), (0,)), ((), ()))
    return (lax.dot_general(hi, w_exact, dn, preferred_element_type=F32)
            + lax.dot_general(lo, w_exact, dn, preferred_element_type=F32))


def _dot2_left(w_exact, x):
    hi, lo = _split(x)
    return (jnp.dot(w_exact, hi, preferred_element_type=F32)
            + jnp.dot(w_exact, lo, preferred_element_type=F32))


_NN = (((1,), (0,)), ((), ()))
_TN = (((0,), (0,)), ((), ()))
_NT = (((1,), (1,)), ((), ()))


def _dot3(a, b, dn=_NN):
    ah, al = _split(a)
    bh, bl = _split(b)
    f = lambda x, y: lax.dot_general(x, y, dn, preferred_element_type=F32)
    return f(ah, bh) + f(al, bh) + f(ah, bl)


def _logsig(x):
    return jnp.minimum(x, 0.0) - jnp.log(1.0 + jnp.exp(-jnp.abs(x)))


def _sigmoid(x):
    return jax.nn.sigmoid(x)


def _rms(x, g):
    return x * lax.rsqrt(jnp.mean(x * x, axis=-1, keepdims=True) + EPS) * g


def _tri(n, m=None):
    m = n if m is None else m
    r = lax.broadcasted_iota(jnp.int32, (n, m), 0)
    c = lax.broadcasted_iota(jnp.int32, (n, m), 1)
    return r, c


def _proj_in_kernel(x_ref, g_ref, w_ref, o_ref, h_ref):
    @pl.when(pl.program_id(1) == 0)
    def _():
        h_ref[...] = _rms(x_ref[...], g_ref[...]).astype(BF16)

    o_ref[...] = lax.dot_general(h_ref[...], w_ref[0], _NT, preferred_element_type=F32)


def _proj_in(x, g, w_layer, tm, tn=2304):
    w_t, l = w_layer
    m = x.shape[0]
    return pl.pallas_call(
        _proj_in_kernel,
        grid=(m // tm, NP // tn),
        in_specs=[pl.BlockSpec((tm, D_MODEL), lambda i, j: (i, 0)),
                  pl.BlockSpec((1, D_MODEL), lambda i, j: (0, 0)),
                  pl.BlockSpec((1, tn, D_MODEL), lambda i, j: (l, j, 0))],
        out_specs=pl.BlockSpec((tm, tn), lambda i, j: (i, j)),
        out_shape=jax.ShapeDtypeStruct((m, NP), F32),
        scratch_shapes=[pltpu.VMEM((tm, D_MODEL), BF16)],
        compiler_params=pltpu.CompilerParams(
            dimension_semantics=("parallel", "arbitrary"), vmem_limit_bytes=VMEM_LIMIT),
        name="proj_in",
    )(x, g, w_t)


def _merge_kernel(og_ref, om_ref, or_ref, g0_ref, g1_ref, g2_ref, x_ref, wb_ref, wo_ref, o_ref):
    acc = jnp.dot(og_ref[...], wb_ref[0], preferred_element_type=F32) * _sigmoid(g0_ref[...])
    acc += jnp.dot(om_ref[...], wb_ref[1], preferred_element_type=F32) * _sigmoid(g1_ref[...])
    acc += jnp.dot(or_ref[...], wb_ref[2], preferred_element_type=F32) * _sigmoid(g2_ref[...])
    o_ref[...] = x_ref[...] + jnp.dot(acc.astype(BF16), wo_ref[...], preferred_element_type=F32)


def _merge(o_gla, o_ml, o_rw, z, x, wb, wo, tm):
    m = x.shape[0]
    row = lambda i: (i, 0)
    gate_blk = OFF_GATE // D_MODEL
    return pl.pallas_call(
        _merge_kernel,
        grid=(m // tm,),
        in_specs=[pl.BlockSpec((tm, D_MODEL), row),
                  pl.BlockSpec((tm, D_MODEL), row),
                  pl.BlockSpec((tm, D_MODEL), row),
                  pl.BlockSpec((tm, D_MODEL), lambda i: (i, gate_blk)),
                  pl.BlockSpec((tm, D_MODEL), lambda i: (i, gate_blk + 1)),
                  pl.BlockSpec((tm, D_MODEL), lambda i: (i, gate_blk + 2)),
                  pl.BlockSpec((tm, D_MODEL), row),
                  pl.BlockSpec((3, D_MODEL, D_MODEL), lambda i: (0, 0, 0)),
                  pl.BlockSpec((D_MODEL, D_MODEL), lambda i: (0, 0))],
        out_specs=pl.BlockSpec((tm, D_MODEL), row),
        out_shape=jax.ShapeDtypeStruct((m, D_MODEL), F32),
        compiler_params=pltpu.CompilerParams(
            dimension_semantics=("parallel",), vmem_limit_bytes=VMEM_LIMIT),
        name="merge",
    )(o_gla, o_ml, o_rw, z, z, z, x, wb, wo)


def _ffn_kernel(x_ref, g_ref, w1_ref, w2_ref, fg_ref, o_ref, h_ref, acc_ref, *, final):
    f = pl.program_id(1)

    @pl.when(f == 0)
    def _():
        h_ref[...] = _rms(x_ref[...], g_ref[...]).astype(BF16)
        acc_ref[...] = jnp.zeros_like(acc_ref)

    a = jnp.dot(h_ref[...], w1_ref[...], preferred_element_type=F32)
    a = jnp.square(jnp.maximum(a, 0.0))
    acc_ref[...] += jnp.dot(a.astype(BF16), w2_ref[...], preferred_element_type=F32)

    @pl.when(f == pl.num_programs(1) - 1)
    def _():
        y = x_ref[...] + acc_ref[...]
        if final:
            y = _rms(y, fg_ref[...])
        o_ref[...] = y


def _ffn(x, g, w1, w2, fg, final, tm, tf=1024):
    m = x.shape[0]
    return pl.pallas_call(
        functools.partial(_ffn_kernel, final=final),
        grid=(m // tm, D_FF // tf),
        in_specs=[pl.BlockSpec((tm, D_MODEL), lambda i, f: (i, 0)),
                  pl.BlockSpec((1, D_MODEL), lambda i, f: (0, 0)),
                  pl.BlockSpec((D_MODEL, tf), lambda i, f: (0, f)),
                  pl.BlockSpec((tf, D_MODEL), lambda i, f: (f, 0)),
                  pl.BlockSpec((1, D_MODEL), lambda i, f: (0, 0))],
        out_specs=pl.BlockSpec((tm, D_MODEL), lambda i, f: (i, 0)),
        out_shape=jax.ShapeDtypeStruct((m, D_MODEL), F32),
        scratch_shapes=[pltpu.VMEM((tm, D_MODEL), BF16), pltpu.VMEM((tm, D_MODEL), F32)],
        compiler_params=pltpu.CompilerParams(
            dimension_semantics=("parallel", "arbitrary"), vmem_limit_bytes=VMEM_LIMIT),
        name="ffn",
    )(x, g, w1, w2, fg)


def _rw_tokens(zr, zk, zv, zwa, zg, w0, wup, a0, aup, gup, kkw, kaw):
    u = w0 + _dot(jnp.tanh(zwa), wup)
    lw = _sigmoid(u) * (-math.exp(-0.5))
    a = _sigmoid(a0 + _dot(zwa, aup))
    g = _dot(_sigmoid(zg), gup)
    kk0 = zk * kkw
    kmod = zk * (1.0 + (a - 1.0) * kaw)
    return lw, a, g, kk0, kmod


def _ones_bd():
    r, c = _tri(128)
    return jnp.where((r // RW_HEAD) == (c // RW_HEAD), 1.0, 0.0).astype(BF16)


def _seg(x, ones_bd):
    return _dot2(x, ones_bd)


def _rw_epilogue(y, r, kmod, v, g, rk, ng, ones_bd):
    mean = _seg(y, ones_bd) * (1.0 / RW_HEAD)
    yc = y - mean
    var = _seg(yc * yc, ones_bd) * (1.0 / RW_HEAD)
    yn = yc * lax.rsqrt(var + EPS) * ng
    bonus = _seg(r * kmod * rk, ones_bd) * v
    return ((yn + bonus) * g).astype(BF16)


def _seq_per_step(b_):
    return 2 if b_ % 2 == 0 else 1


def _interleave(gens, pattern):
    live = dict(gens)
    while live:
        for key in pattern:
            g = live.get(key)
            if g is None:
                continue
            try:
                next(g)
            except StopIteration:
                del live[key]


def _gla_chunk(rows, ns, q_ref, k_ref, v_ref, gg_ref, sm_ref, aup_ref, ab_ref, gn_ref, o_ref, s_ref):
    c_ = CHUNK
    r, c = _tri(c_)
    incl = r >= c
    tril = jnp.where(incl, 1.0, 0.0).astype(BF16)
    ones_c = jnp.ones((c_, GLA_DK), BF16)
    tokw = []
    for j in range(ns):
        x = _dot(sm_ref[j, rows, :], aup_ref[...]) + ab_ref[...]
        la = _logsig(x) * (1.0 / GLA_TAU)
        b = _dot2_left(tril, la)
        dec = jnp.exp(_dot2_tn(la, ones_c))
        q = q_ref[j, rows, :] * (GLA_DK ** -0.5)
        k = k_ref[j, rows, :]
        tokw.append((q * jnp.exp(b), k * jnp.exp(-b), k * jnp.exp(b[c_ - 1:c_, :] - b), dec))
    yield
    its = [(j, h) for j in range(ns) for h in range(GLA_HEADS)]
    hs = range(len(its))
    kls = [slice(h * GLA_DK, (h + 1) * GLA_DK) for _, h in its]
    vls = [slice(h * GLA_DV, (h + 1) * GLA_DV) for _, h in its]
    qe = [tokw[j][0][:, kls[i]] for i, (j, _) in enumerate(its)]
    ke = [tokw[j][1][:, kls[i]] for i, (j, _) in enumerate(its)]
    kd = [tokw[j][2][:, kls[i]] for i, (j, _) in enumerate(its)]
    v = [v_ref[j, rows, vls[i]] for i, (j, _) in enumerate(its)]
    att = [jnp.where(incl, _dot_nt(qe[i], ke[i]), 0.0) for i in hs]
    yield
    s = [s_ref[j, h] for j, h in its]
    oi = [_dot(qe[i], s[i]) for i in hs]
    yield
    kv = [_dot_tn(kd[i], v[i]) for i in hs]
    yield
    o = [oi[i] + _dot(att[i], v[i]) for i in hs]
    for i, (j, h) in enumerate(its):
        dh = tokw[j][3][kls[i], :]
        s_ref[j, h] = s[i] * jnp.concatenate([dh, dh], axis=1) + kv[i]
    yield
    msq = [jnp.mean(o[i] * o[i], axis=-1, keepdims=True) for i in hs]
    yield
    for i, (j, _) in enumerate(its):
        on = o[i] * lax.rsqrt(msq[i] + EPS) * gn_ref[...]
        gg = gg_ref[j, rows, vls[i]]
        o_ref[j, rows, vls[i]] = (on * (gg * _sigmoid(gg))).astype(BF16)


def _mlstm_chunk(rows, ns, q_ref, k_ref, v_ref, mo_ref, sm_ref, brow_ref, bcol_ref, ng_ref,
                 o_ref, c_ref, n_ref, m_ref):
    c_ = CHUNK
    r, c = _tri(c_)
    incl = r >= c
    tril = jnp.where(incl, 1.0, 0.0).astype(BF16)
    triu = jnp.where(r <= c, 1.0, 0.0).astype(BF16)
    sub8 = lax.broadcasted_iota(jnp.int32, (8, c_), 0)
    eye = jnp.where(r == c, 1.0, 0.0).astype(BF16)
    sm, fcol, smt, frow = [], [], [], []
    for j in range(ns):
        raw = sm_ref[j, rows, :]
        sm.append(raw + brow_ref[...])
        fcol.append(_dot2_left(tril, _logsig(sm[j])))
        hi, lo = _split(raw)
        lo2 = (raw - hi.astype(F32) - lo.astype(F32)).astype(BF16)
        rawt = sum(lax.dot_general(piece, eye, _TN, preferred_element_type=F32) for piece in (hi, lo, lo2))
        smt.append(rawt[SM_I:SM_I + 8, :] + bcol_ref[...])
        frow.append(_dot2(jnp.where(sub8 >= 4, _logsig(smt[j]), 0.0), triu))
    yield
    its = [(j, h) for j in range(ns) for h in range(ML_HEADS)]
    hs = range(len(its))
    hls = [slice(h * ML_DH, (h + 1) * ML_DH) for _, h in its]
    f_t = [fcol[j][:, SM_F + h:SM_F + h + 1] for j, h in its]
    i_t = [sm[j][:, SM_I + h:SM_I + h + 1] for j, h in its]
    f_s = [frow[j][4 + h:5 + h, :] for j, h in its]
    i_s = [smt[j][h:h + 1, :] for j, h in its]
    m = [m_ref[j, h][0:1, 0:1] for j, h in its]
    q = [q_ref[j, rows, hls[i]] * (ML_DH ** -0.5) for i, (j, _) in enumerate(its)]
    k = [k_ref[j, rows, hls[i]] for i, (j, _) in enumerate(its)]
    v = [v_ref[j, rows, hls[i]] for i, (j, _) in enumerate(its)]
    qk0 = [_dot_nt(q[i], k[i]) for i in hs]
    gmax = [jnp.max(jnp.where(incl, i_s[i] - f_s[i], -jnp.inf), axis=1, keepdims=True) for i in hs]
    yield
    cm = [c_ref[j, h] for j, h in its]
    n8 = [n_ref[j, h] for j, h in its]
    qc = [_dot(q[i], cm[i]) for i in hs]
    qn = [_dot_nt(q[i], n8[i])[:, 0:1] for i in hs]
    yield
    mt = [jnp.maximum(f_t[i] + m[i], f_t[i] + gmax[i]) for i in hs]
    w_inter = [jnp.exp(f_t[i] + m[i] - mt[i]) for i in hs]
    d = [jnp.exp(jnp.where(incl, (f_t[i] - mt[i]) + (i_s[i] - f_s[i]), -jnp.inf)) for i in hs]
    qk = [qk0[i] * d[i] for i in hs]
    qkv = [_dot(qk[i], v[i]) for i in hs]
    yield
    m_new = [mt[i][c_ - 1:c_, :] for i in hs]
    f_last = [f_t[i][c_ - 1:c_, :] for i in hs]
    w_end = [jnp.exp(f_last[i] + m[i] - m_new[i]) for i in hs]
    kws = [k[i] * jnp.exp(f_last[i] - f_t[i] + i_t[i] - m_new[i]) for i in hs]
    kwv = [_dot_tn(kws[i], v[i]) for i in hs]
    yield
    for i, (j, h) in enumerate(its):
        c_ref[j, h] = w_end[i] * cm[i] + kwv[i]
        n_ref[j, h] = w_end[i] * n8[i] + jnp.broadcast_to(jnp.sum(kws[i], axis=0, keepdims=True),
                                                            (8, ML_DH))
        m_ref[j, h] = jnp.broadcast_to(m_new[i], (8, 128))
    den = [w_inter[i] * qn[i] + jnp.sum(qk[i], axis=1, keepdims=True) for i in hs]
    yield
    hm = [(w_inter[i] * qc[i] + qkv[i]) / jnp.maximum(jnp.abs(den[i]), jnp.exp(-mt[i])) for i in hs]
    mu = [jnp.mean(hm[i], axis=-1, keepdims=True) for i in hs]
    yield
    yc = [hm[i] - mu[i] for i in hs]
    var = [jnp.mean(yc[i] * yc[i], axis=-1, keepdims=True) for i in hs]
    yield
    for i, (j, _) in enumerate(its):
        y = yc[i] * lax.rsqrt(var[i] + EPS) * ng_ref[...]
        o_ref[j, rows, hls[i]] = (y * _sigmoid(mo_ref[j, rows, hls[i]])).astype(BF16)


def _rw_chunk(rows, seqs, r_ref, k_ref, v_ref, wa_ref, gd_ref, mu_refs, par_refs, o_ref, s_ref, prev_refs):
    mur_ref, muk_ref, muv_ref, muwa_ref, mug_ref = mu_refs
    w0_ref, wup_ref, a0_ref, aup_ref, gup_ref, kkw_ref, kaw_ref, rk_ref, ng_ref = par_refs
    pr_s, pk_s, pv_s, pwa_s, pg_s = prev_refs
    c_ = CHUNK
    r_, c2 = _tri(c_, 128)
    s_idx = c2 % c_
    strict_pk = s_idx < r_
    incl_pk = s_idx <= r_
    eye_pk = jnp.where(s_idx == r_, 1.0, 0.0)
    rr, cc = _tri(c_)
    tril = jnp.where(rr >= cc, 1.0, 0.0).astype(BF16)
    lo_half = lax.broadcasted_iota(jnp.int32, (c_, 128), 1) < RW_HEAD
    r128, c128 = _tri(128)
    bdmask = (r128 // RW_HEAD) == (c128 // RW_HEAD)
    ones_bd = jnp.where(bdmask, 1.0, 0.0).astype(BF16)
    first_w = lax.broadcasted_iota(jnp.int32, (c_, RW_W), 0) == 0
    first_n = lax.broadcasted_iota(jnp.int32, (c_, 128), 0) == 0

    def mix(x_ref, prev_ref, mu_ref, j):
        x = x_ref[j, rows, :]
        first = first_w if x.shape[1] == RW_W else first_n
        prev = jnp.where(first, prev_ref[j], pltpu.roll(x, 1, 0))
        prev_ref[j] = x[c_ - 1:c_, :]
        return x + (prev - x) * mu_ref[...]

    def bd(y):
        yb = y.astype(BF16)
        zero = jnp.zeros_like(yb)
        return jnp.concatenate([jnp.where(lo_half, yb, zero), jnp.where(lo_half, zero, yb)], axis=0)

    def pk_mul(x, y):
        return jnp.dot(x.astype(BF16), bd(y), preferred_element_type=F32)

    def seg_all(xs):
        st = jnp.dot(jnp.concatenate(xs, axis=0).astype(BF16), ones_bd, preferred_element_type=F32)
        return [st[i * c_:(i + 1) * c_] for i in range(len(xs))]

    ns = len(seqs)
    zr = [mix(r_ref, pr_s, mur_ref, j) for j in seqs]
    zv = [mix(v_ref, pv_s, muv_ref, j) for j in seqs]
    tok = [_rw_tokens(zr[q], mix(k_ref, pk_s, muk_ref, j), zv[q],
                      mix(wa_ref, pwa_s, muwa_ref, j), mix(gd_ref, pg_s, mug_ref, j),
                      w0_ref[...], wup_ref[...], a0_ref[...], aup_ref[...],
                      gup_ref[...], kkw_ref[...], kaw_ref[...]) for q, j in enumerate(seqs)]
    yield
    its = [(j, p) for j in range(ns) for p in range(RW_PAIRS)]
    prs = range(len(its))
    lns = [slice(p * 128, (p + 1) * 128) for _, p in its]
    lw = [tok[j][0][:, lns[i]] for i, (j, _) in enumerate(its)]
    a = [tok[j][1][:, lns[i]] for i, (j, _) in enumerate(its)]
    g = [tok[j][2][:, lns[i]] for i, (j, _) in enumerate(its)]
    kkp = [tok[j][3][:, lns[i]] for i, (j, _) in enumerate(its)]
    kp = [tok[j][4][:, lns[i]] for i, (j, _) in enumerate(its)]
    rp = [zr[j][:, lns[i]] for i, (j, _) in enumerate(its)]
    vp = [zv[j][:, lns[i]] for i, (j, _) in enumerate(its)]
    ssq = seg_all([x * x for x in kkp])
    cumf = [_dot2_left(tril, tok[j][0]) for j in range(ns)]
    yield
    cum = [cumf[j][:, lns[i]] for i, (j, _) in enumerate(its)]
    kkn = [kkp[p] * lax.rsqrt(ssq[p] + 1e-12) for p in prs]
    bp = [kkn[p] * a[p] for p in prs]
    cl = [cum[p][c_ - 1:c_, :] for p in prs]
    e_neg = [jnp.exp(-cum[p]) for p in prs]
    e_end = [jnp.exp(cl[p] - cum[p]) for p in prs]
    rh = [rp[p] * jnp.exp(cum[p]) for p in prs]
    kh = [kkn[p] * jnp.exp(cum[p] - lw[p]) for p in prs]
    kt = [kp[p] * e_neg[p] for p in prs]
    bt = [bp[p] * e_neg[p] for p in prs]
    kg = [kp[p] * e_end[p] for p in prs]
    bg = [bp[p] * e_end[p] for p in prs]
    kr = [jnp.concatenate([kh[p], rh[p]], axis=0).astype(BF16) for p in prs]
    ktb = [jnp.concatenate([bd(kt[p]), bd(bt[p])], axis=0) for p in prs]
    pkb = [lax.dot_general(kr[p], ktb[p], _NT, preferred_element_type=F32) for p in prs]
    yield
    ak = [jnp.where(strict_pk, pkb[p][0:c_, 0:128], 0.0) for p in prs]
    gk = [jnp.where(incl_pk, pkb[p][c_:2 * c_, 0:128], 0.0) for p in prs]
    gb = [jnp.where(incl_pk, pkb[p][c_:2 * c_, 128:256], 0.0) for p in prs]
    pw = [jnp.where(strict_pk, -pkb[p][0:c_, 128:256], 0.0) for p in prs]
    tm = [eye_pk + pw[p] for p in prs]
    agv = [pk_mul(jnp.concatenate([ak[p], gk[p]], axis=0), vp[p]) for p in prs]
    pw = [pk_mul(pw[p], pw[p]) for p in prs]
    yield
    for _ in range(4):
        x = [pk_mul(jnp.concatenate([pw[p], tm[p]], axis=0), pw[p]) for p in prs]
        tm = [tm[p] + x[p][c_:2 * c_] for p in prs]
        pw = [x[p][0:c_] for p in prs]
        yield
    tm = [tm[p] + pk_mul(tm[p], pw[p]) for p in prs]
    yield
    ku = [jnp.dot(tm[p].astype(BF16), jnp.concatenate([bd(kh[p]), bd(agv[p][0:c_])], axis=1),
                  preferred_element_type=F32) for p in prs]
    yield
    gku = [jnp.dot(gb[p].astype(BF16),
                   jnp.concatenate([bd(ku[p][:, 0:128]), bd(ku[p][:, 128:256])], axis=1),
                   preferred_element_type=F32) for p in prs]
    d = [_dot_tn(jnp.concatenate([vp[p], -ku[p][:, 128:256]], axis=0),
                 jnp.concatenate([kg[p], bg[p]], axis=0)) for p in prs]
    yield
    rhp = [rh[p] - gku[p][:, 0:128] for p in prs]
    y0 = [agv[p][c_:2 * c_] - gku[p][:, 128:256] for p in prs]
    s = [s_ref[seqs[j], p] for j, p in its]
    uy = [_dot_nt(jnp.concatenate([ku[p][:, 0:128], rhp[p]], axis=0), s[p]) for p in prs]
    yield
    ub = [_dot_tn(uy[p][0:c_], bg[p]) for p in prs]
    y = [uy[p][c_:2 * c_] + y0[p] for p in prs]
    mean = seg_all(y)
    yield
    for i, (j, p) in enumerate(its):
        s_ref[seqs[j], p] = s[i] * jnp.exp(cl[i]) + jnp.where(bdmask, d[i] - ub[i], 0.0)
    yc = [y[p] - mean[p] * (1.0 / RW_HEAD) for p in prs]
    var = seg_all([x * x for x in yc])
    bon = seg_all([rp[p] * kp[p] * rk_ref[:, lns[p]] for p in prs])
    yield
    for i, (j, _) in enumerate(its):
        yn = yc[i] * lax.rsqrt(var[i] * (1.0 / RW_HEAD) + EPS) * ng_ref[:, lns[i]]
        o_ref[seqs[j], rows, lns[i]] = ((yn + bon[i] * vp[i]) * g[i]).astype(BF16)


def _mixers_kernel(*refs, n_chunks, ns):
    (gq, gk, gv, gg, sm, mq, mk, mv, mo, rr, rk, rv, rwa, rgd) = refs[0:14]
    aup, ab, gn, brow, bcol, mng = refs[14:20]
    mu_refs = refs[20:25]
    par_refs = refs[25:34]
    o_gla, o_ml, o_rw, gla_out, c_out, n_out, m_out, rw_out = refs[34:42]
    s_gla, c_s, n_s, m_s, s_rw = refs[42:47]
    prev_refs = refs[47:52]
    t = pl.program_id(1)

    @pl.when(t == 0)
    def _():
        for ref in (s_gla, c_s, n_s, m_s, s_rw) + tuple(prev_refs):
            ref[...] = jnp.zeros_like(ref)

    def chunk(ci, carry):
        rows = pl.ds(pl.multiple_of(ci * CHUNK, CHUNK), CHUNK)
        half = max(ns // 2, 1)
        rw = lambda seqs: _rw_chunk(rows, seqs, rr, rk, rv, rwa, rgd, mu_refs, par_refs, o_rw, s_rw, prev_refs)
        gla = _gla_chunk(rows, ns, gq, gk, gv, gg, sm, aup, ab, gn, o_gla, s_gla)
        mls = _mlstm_chunk(rows, ns, mq, mk, mv, mo, sm, brow, bcol, mng, o_ml, c_s, n_s, m_s)
        if ns == 1:
            _interleave({"r": rw((0,)), "g": gla, "m": mls}, "rmrg")
        else:
            _interleave({"r": rw(tuple(range(half))), "g": gla}, "rrg")
            _interleave({"r": rw(tuple(range(half, ns))), "m": mls}, "rrm")
        return carry

    lax.fori_loop(0, n_chunks, chunk, 0)

    @pl.when(t == pl.num_programs(1) - 1)
    def _():
        gla_out[...] = s_gla[...]
        c_out[...] = c_s[...]
        for j in range(ns):
            for h in range(ML_HEADS):
                n_out[j, h:h + 1, :] = n_s[j, h][0:1, :]
                m_out[j, h:h + 1, :] = m_s[j, h][0:1, :]
            for p in range(RW_PAIRS):
                s = s_rw[j, p]
                rw_out[j, 2 * p] = s[0:RW_HEAD, 0:RW_HEAD]
                rw_out[j, 2 * p + 1] = s[RW_HEAD:, RW_HEAD:]


def _mixers_prompt(z, lp, b_, t_, tt):
    ns = _seq_per_step(b_)
    nc = tt // CHUNK
    z3 = z.reshape(b_, t_, NP)
    zblk = lambda w, off: pl.BlockSpec((ns, tt, w), lambda b, t, c=off // w: (b, t, c))
    const = lambda b, t: (0, 0)
    full = lambda a: pl.BlockSpec(a.shape, const)
    seq = lambda *shape: pl.BlockSpec((ns,) + shape, lambda b, t, n=len(shape): (b,) + (0,) * n)
    params = [lp["aup_pad"], lp["ab"], lp["gla_ng"], lp["brow"], lp["bcol"], lp["ml_ng"], *lp["mus"], *lp["rw"]]
    bf = jax.ShapeDtypeStruct((b_, t_, D_MODEL), BF16)
    outs = pl.pallas_call(
        functools.partial(_mixers_kernel, n_chunks=nc, ns=ns),
        grid=(b_ // ns, t_ // tt),
        in_specs=[zblk(GLA_QK, OFF_GQ), zblk(GLA_QK, OFF_GK), zblk(GLA_V, OFF_GV), zblk(GLA_V, OFF_GG),
                  zblk(128, OFF_SM),
                  zblk(ML_W, OFF_MQ), zblk(ML_W, OFF_MK), zblk(ML_W, OFF_MV), zblk(ML_W, OFF_MO),
                  zblk(RW_W, OFF_RW), zblk(RW_W, OFF_RW + RW_W), zblk(RW_W, OFF_RW + 2 * RW_W),
                  zblk(128, OFF_RW + 3 * RW_W), zblk(128, OFF_RW + 3 * RW_W + 128)]
                 + [full(a) for a in params],
        out_specs=[pl.BlockSpec((ns, tt, D_MODEL), lambda b, t: (b, t, 0))] * 3
                  + [seq(GLA_HEADS, GLA_DK, GLA_DV), seq(ML_HEADS, ML_DH, ML_DH), seq(ML_HEADS, ML_DH),
                     seq(ML_HEADS, 128), seq(RW_HEADS, RW_HEAD, RW_HEAD)],
        out_shape=[bf, bf, bf,
                   jax.ShapeDtypeStruct((b_, GLA_HEADS, GLA_DK, GLA_DV), F32),
                   jax.ShapeDtypeStruct((b_, ML_HEADS, ML_DH, ML_DH), F32),
                   jax.ShapeDtypeStruct((b_, ML_HEADS, ML_DH), F32),
                   jax.ShapeDtypeStruct((b_, ML_HEADS, 128), F32),
                   jax.ShapeDtypeStruct((b_, RW_HEADS, RW_HEAD, RW_HEAD), F32)],
        scratch_shapes=[pltpu.VMEM((ns, GLA_HEADS, GLA_DK, GLA_DV), F32),
                        pltpu.VMEM((ns, ML_HEADS, ML_DH, ML_DH), F32),
                        pltpu.VMEM((ns, ML_HEADS, 8, ML_DH), F32),
                        pltpu.VMEM((ns, ML_HEADS, 8, 128), F32),
                        pltpu.VMEM((ns, RW_PAIRS, 128, 128), F32),
                        pltpu.VMEM((ns, 1, RW_W), F32), pltpu.VMEM((ns, 1, RW_W), F32),
                        pltpu.VMEM((ns, 1, RW_W), F32), pltpu.VMEM((ns, 1, 128), F32),
                        pltpu.VMEM((ns, 1, 128), F32)],
        compiler_params=pltpu.CompilerParams(
            dimension_semantics=("parallel", "arbitrary"), vmem_limit_bytes=MIXER_VMEM_LIMIT),
        name="mixers_prompt",
    )(*([z3] * 14), *params)
    o_gla, o_ml, o_rw, gla_new, c_new, n_new, m_new, rw_new = outs
    flat = lambda o: o.reshape(b_ * t_, D_MODEL)
    return flat(o_gla), flat(o_ml), flat(o_rw), (gla_new, c_new, n_new, m_new[:, :, 0], rw_new)


def _sample_prep_kernel(sm_ref, r_ref, k_ref, v_ref, wa_ref, gd_ref, m_ref, shift_ref,
                        aup_ref, ab_ref, ib_ref, fb_ref,
                        mur_ref, muk_ref, muv_ref, muwa_ref, mug_ref,
                        w0_ref, wup_ref, a0_ref, aup2_ref, gup_ref, kkw_ref, kaw_ref,
                        gla_la_ref, mnew_ref, wend_ref, ws_ref,
                        rr_ref, rk_ref, rv_ref, rg_ref, kkt_ref, bt_ref, kt_ref, rt_ref, wt_ref, vt_ref):
    sm = sm_ref[...]
    gla_la_ref[...] = _logsig(_dot(sm, aup_ref[...]) + ab_ref[...]) * (1.0 / GLA_TAU)
    i_pre = sm[:, SM_I:SM_I + ML_HEADS] + ib_ref[...]
    logf = _logsig(sm[:, SM_F:SM_F + ML_HEADS] + fb_ref[...])
    m = m_ref[0]
    m_new = jnp.maximum(logf + m, i_pre)
    mnew_ref[...] = m_new
    wend_ref[...] = jnp.exp(logf + m - m_new)
    ws_ref[...] = jnp.exp(i_pre - m_new)
    mixf = lambda x, off, mu: x[...] + (shift_ref[0, :, off:off + x.shape[1]] - x[...]) * mu[...]
    zr = mixf(r_ref, 0, mur_ref)
    zk = mixf(k_ref, RW_W, muk_ref)
    zv = mixf(v_ref, 2 * RW_W, muv_ref)
    zwa = mixf(wa_ref, 3 * RW_W, muwa_ref)
    zg = mixf(gd_ref, 3 * RW_W + 128, mug_ref)
    lw, a, g, kk0, kmod = _rw_tokens(zr, zk, zv, zwa, zg, w0_ref[...], wup_ref[...], a0_ref[...],
                                     aup2_ref[...], gup_ref[...], kkw_ref[...], kaw_ref[...])
    ones_bd = _ones_bd()
    for p in range(RW_PAIRS):
        ln = slice(p * 128, (p + 1) * 128)
        kkp = kk0[:, ln]
        kkn = kkp * lax.rsqrt(_seg(kkp * kkp, ones_bd) + 1e-12)
        kkt_ref[ln, :] = kkn.T
        bt_ref[ln, :] = (kkn * a[:, ln]).T
    rr_ref[...] = zr
    rk_ref[...] = kmod
    rv_ref[...] = zv
    rg_ref[...] = g
    kt_ref[...] = kmod.T
    rt_ref[...] = zr.T
    wt_ref[...] = jnp.exp(lw).T
    vt_ref[...] = zv.T


STEP_BB = 8


def _row_select(rows_list, rid):
    acc = rows_list[0]
    for b in range(1, len(rows_list)):
        acc = jnp.where(rid == b, rows_list[b], acc)
    return acc


def _gla_step_kernel(s_ref, q_ref, k_ref, la_ref, v_ref, *rest):
    sn_ref, o_ref = rest[-2:]
    rid_k = lax.broadcasted_iota(jnp.int32, (STEP_BB, GLA_DK), 0)
    rid_v = lax.broadcasted_iota(jnp.int32, (STEP_BB, GLA_DV), 0)
    bs = range(STEP_BB)
    onehot = [jnp.where(rid_v == b, 1.0, 0.0).astype(BF16) for b in bs]
    for h in range(GLA_HEADS):
        kl = slice(h * GLA_DK, (h + 1) * GLA_DK)
        vl = slice(h * GLA_DV, (h + 1) * GLA_DV)
        qt = q_ref[:, kl] * (GLA_DK ** -0.5)
        kt = k_ref[:, kl]
        lt = la_ref[:, kl]
        vt = v_ref[:, vl]
        dec = [jnp.exp(_dot2_tn(lt, onehot[b])) for b in bs]
        kv = [_dot3(jnp.where(rid_k == b, kt, 0.0), vt, _TN) for b in bs]
        sn = [dec[b] * s_ref[0, b, h] + kv[b] for b in bs]
        for b in bs:
            sn_ref[0, b, h] = sn[b]
        o_ref[:, vl] = _row_select([_dot3(qt, sn[b]) for b in bs], rid_v)


def _mlstm_step_kernel(c_ref, n_ref, q_ref, k_ref, v_ref, wend_ref, ws_ref, mnew_ref, *rest):
    cn_ref, nn_ref, h_ref = rest[-3:]
    rid = lax.broadcasted_iota(jnp.int32, (STEP_BB, ML_DH), 0)
    for h in range(ML_HEADS):
        hl = slice(h * ML_DH, (h + 1) * ML_DH)
        qt = q_ref[:, hl] * (ML_DH ** -0.5)
        kt = k_ref[:, hl]
        vt = v_ref[:, hl]
        wend = wend_ref[:, h:h + 1]
        ws = ws_ref[:, h:h + 1]
        nn = wend * n_ref[0, :, hl] + ws * kt
        nn_ref[0, :, hl] = nn
        den = jnp.sum(qt * nn, axis=1, keepdims=True)
        kws = kt * ws
        nums = []
        for half in range(2):
            bs = range(half * 4, half * 4 + 4)
            kwv = [_dot3(jnp.where(rid == b, kws, 0.0), vt, _TN) for b in bs]
            cn = [wend_ref[b:b + 1, h:h + 1] * c_ref[0, b, h] + kwv[i] for i, b in enumerate(bs)]
            for i, b in enumerate(bs):
                cn_ref[0, b, h] = cn[i]
            nums += [_dot3(qt, cn[i]) for i in range(4)]
        num = _row_select(nums, rid)
        h_ref[:, hl] = num / jnp.maximum(jnp.abs(den), jnp.exp(-mnew_ref[:, h:h + 1]))


RW_STEP_HB = 2


def _rw_step_kernel(s_ref, kk_ref, b_ref, k_ref, r_ref, w_ref, v_ref, *rest):
    sn_ref, y_ref = rest[-2:]
    n = RW_HEAD
    for hh in range(RW_STEP_HB):
        fl = slice(hh * n, (hh + 1) * n)
        nkk, bb, kt, rt, wt = -kk_ref[fl, :], b_ref[fl, :], k_ref[fl, :], r_ref[fl, :], w_ref[fl, :]

        def row(i, carry):
            s = s_ref[0, hh, i]
            sa = jnp.sum(s * nkk, axis=0, keepdims=True)
            sn = s * wt + sa * bb + v_ref[pl.ds(hh * n + i, 1), :] * kt
            sn_ref[0, hh, i] = sn
            y_ref[pl.ds(hh * n + i, 1), :] = jnp.sum(sn * rt, axis=0, keepdims=True)
            return carry

        lax.fori_loop(0, n, row, 0, unroll=4)


def _rw_state_call(l, state_t, vecs_t, prev_out):
    nb = state_t.shape[-1]
    st_spec = pl.BlockSpec((1, RW_STEP_HB, RW_HEAD, RW_HEAD, nb), lambda h: (l, h, 0, 0, 0))
    vec_spec = pl.BlockSpec((RW_STEP_HB * RW_HEAD, nb), lambda h: (h, 0))
    args = [state_t] + list(vecs_t)
    in_specs = [st_spec] + [vec_spec] * len(vecs_t)
    aliases = {}
    if prev_out is not None:
        aliases[len(args)] = 0
        in_specs.append(pl.BlockSpec(memory_space=pl.ANY))
        args.append(prev_out)
    return pl.pallas_call(
        _rw_step_kernel,
        grid=(RW_HEADS // RW_STEP_HB,),
        in_specs=in_specs,
        out_specs=[st_spec, vec_spec],
        out_shape=[jax.ShapeDtypeStruct(state_t.shape, F32), jax.ShapeDtypeStruct((RW_W, nb), F32)],
        input_output_aliases=aliases,
        compiler_params=pltpu.CompilerParams(
            dimension_semantics=("parallel",), vmem_limit_bytes=VMEM_LIMIT),
        name="rwkv_step",
    )(*args)


def _sample_post_kernel(og_ref, gg_ref, gn_ref, hm_ref, mo_ref, mn_ref,
                        yt_ref, r_ref, k_ref, v_ref, g_ref, rk_ref, ng_ref,
                        ogla_ref, oml_ref, orw_ref):
    for h in range(GLA_HEADS):
        vl = slice(h * GLA_DV, (h + 1) * GLA_DV)
        o = og_ref[:, vl]
        on = o * lax.rsqrt(jnp.mean(o * o, axis=-1, keepdims=True) + EPS) * gn_ref[...]
        gg = gg_ref[:, vl]
        ogla_ref[:, vl] = (on * (gg * _sigmoid(gg))).astype(BF16)
    for h in range(ML_HEADS):
        hl = slice(h * ML_DH, (h + 1) * ML_DH)
        hm = hm_ref[:, hl]
        y = hm - jnp.mean(hm, axis=-1, keepdims=True)
        y = y * lax.rsqrt(jnp.mean(y * y, axis=-1, keepdims=True) + EPS) * mn_ref[...]
        oml_ref[:, hl] = (y * _sigmoid(mo_ref[:, hl])).astype(BF16)
    ones_bd = _ones_bd()
    for p in range(RW_PAIRS):
        ln = slice(p * 128, (p + 1) * 128)
        y = yt_ref[ln, :].T
        orw_ref[:, ln] = _rw_epilogue(y, r_ref[:, ln], k_ref[:, ln], v_ref[:, ln],
                                      g_ref[:, ln], rk_ref[:, ln], ng_ref[:, ln], ones_bd)


def _full_call(kernel_fn, args, out_shapes, name):
    nd = lambda a: (lambda i: (0,) * len(a.shape))
    in_specs, arrays = [], []
    for a in args:
        if isinstance(a, tuple):
            arr, blk, idx = a
            in_specs.append(pl.BlockSpec(blk, lambda i, idx=idx: idx))
            arrays.append(arr)
        else:
            in_specs.append(pl.BlockSpec(a.shape, nd(a)))
            arrays.append(a)
    return pl.pallas_call(
        kernel_fn,
        grid=(1,),
        in_specs=in_specs,
        out_specs=[pl.BlockSpec(s.shape, nd(s)) for s in out_shapes],
        out_shape=out_shapes,
        compiler_params=pltpu.CompilerParams(vmem_limit_bytes=VMEM_LIMIT),
        name=name,
    )(*arrays)


def _state_call(kernel_fn, l, states, vecs, prev_outs, out_width, name):
    nb = states[0].shape[1]
    st_spec = lambda a: pl.BlockSpec((1, STEP_BB) + tuple(a.shape[2:]),
                                     lambda i, n=a.ndim: (l, i) + (0,) * (n - 2))
    in_specs = [st_spec(a) for a in states]
    in_specs += [pl.BlockSpec((STEP_BB, w), lambda i, c=c: (i, c)) for _, w, c in vecs]
    args = list(states) + [a for a, _, _ in vecs]
    aliases = {}
    for j, po in enumerate(prev_outs or ()):
        aliases[len(args)] = j
        in_specs.append(pl.BlockSpec(memory_space=pl.ANY))
        args.append(po)
    out_shape = [jax.ShapeDtypeStruct(a.shape, F32) for a in states]
    out_shape.append(jax.ShapeDtypeStruct((nb, out_width), F32))
    out_specs = [st_spec(a) for a in states] + [pl.BlockSpec((STEP_BB, out_width), lambda i: (i, 0))]
    return pl.pallas_call(
        kernel_fn,
        grid=(nb // STEP_BB,),
        in_specs=in_specs,
        out_specs=out_specs,
        out_shape=out_shape,
        input_output_aliases=aliases,
        compiler_params=pltpu.CompilerParams(
            dimension_semantics=("parallel",), vmem_limit_bytes=VMEM_LIMIT),
        name=name,
    )(*args)


W_IN_MOVES = ((0, 0, 2048), (2064, 2048, 1024), (3088, 3072, 3072), (6168, 6144, 1024), (7192, 7168, 6400))
D_IN = 13592
W_PREP_LANES = 128


def _w_in_prep_kernel(w_ref, o_ref):
    for src, dst, width in W_IN_MOVES:
        o_ref[0, dst:dst + width, :] = w_ref[0, src:src + width, :].astype(BF16)
    small = jnp.concatenate([w_ref[0, 2048:2064, :], w_ref[0, 6160:6168, :],
                             jnp.zeros((NP - OFF_SM - 24, W_PREP_LANES), F32)], axis=0)
    o_ref[0, OFF_SM:NP, :] = small.astype(BF16)


def _w_in_prep(w_in):
    depth = w_in.shape[0]
    w_t = jnp.transpose(w_in, (0, 2, 1))
    return pl.pallas_call(
        _w_in_prep_kernel,
        grid=(depth, D_MODEL // W_PREP_LANES),
        in_specs=[pl.BlockSpec((1, D_IN, W_PREP_LANES), lambda l, i: (l, 0, i))],
        out_specs=pl.BlockSpec((1, NP, W_PREP_LANES), lambda l, i: (l, 0, i)),
        out_shape=jax.ShapeDtypeStruct((depth, NP, D_MODEL), BF16),
        compiler_params=pltpu.CompilerParams(
            dimension_semantics=("parallel", "parallel"), vmem_limit_bytes=VMEM_LIMIT),
        name="w_in_prep",
    )(w_t)


def _prep_layer_params(l, p):
    w_in_p = (p["w_in_t"], l)
    aup_pad = jnp.zeros((128, GLA_QK), F32).at[0:GLA_RANK].set(p["gla_a_up"][l])
    brow = (jnp.zeros((1, 128), F32).at[0, SM_I:SM_I + 4].set(p["ml_i_bias"][l])
            .at[0, SM_F:SM_F + 4].set(p["ml_f_bias"][l]))
    bcol = jnp.broadcast_to(jnp.concatenate([p["ml_i_bias"][l], p["ml_f_bias"][l]])[:, None], (8, CHUNK))
    mu = p["rw_mu"][l][None, :]
    mus = (mu[:, 0:1024], mu[:, 1024:2048], mu[:, 2048:3072], mu[:, 3072:3200], mu[:, 3200:3328])
    wup_pad = jnp.zeros((128, RW_W), F32).at[0:64].set(p["rw_w_up"][l])
    aup2_pad = jnp.zeros((128, RW_W), F32).at[64:128].set(p["rw_a_up"][l])
    row = lambda a: a.reshape(1, -1)
    rw_params = (row(p["rw_w0"][l]), wup_pad, row(p["rw_a0"][l]), aup2_pad, p["rw_g_up"][l],
                 row(p["rw_k_k"][l]), row(p["rw_k_a"][l]), row(p["rw_r_k"][l]), row(p["rw_norm_g"][l]))
    return dict(
        norm1_g=row(p["norm1_g"][l]), w_in=w_in_p, aup_pad=aup_pad, ab=row(p["gla_a_bias"][l]),
        gla_ng=row(p["gla_norm_g"][l]), brow=brow, bcol=bcol, ib=row(p["ml_i_bias"][l]),
        fb=row(p["ml_f_bias"][l]), ml_ng=row(p["ml_norm_g"][l]), mus=mus, rw=rw_params,
        wb=p["w_branch"][l].astype(BF16), wo=p["w_out"][l].astype(BF16), norm2_g=row(p["norm2_g"][l]),
        w1=p["w_ff1"][l].astype(BF16), w2=p["w_ff2"][l].astype(BF16))


def _prompt_layer(x, lp, fg, final, b_, t_):
    rows = x.shape[0]
    tm = min(512, rows)
    tm_big = min(1024, rows)
    z = _proj_in(x, lp["norm1_g"], lp["w_in"], tm_big)
    o_gla, o_ml, o_rw, new_states = _mixers_prompt(z, lp, b_, t_, min(MIXER_TT, t_))
    x = _merge(o_gla, o_ml, o_rw, z, x, lp["wb"], lp["wo"], tm)
    x = _ffn(x, lp["norm2_g"], lp["w1"], lp["w2"], fg, final, tm_big)
    shift = z.reshape(b_, t_, NP)[:, t_ - 1, OFF_RW:OFF_RW + RW_COLS]
    return x, new_states + (shift,)


def _sample_layer(l, x, states, prev, lp, fg, final):
    st_gla, st_c, st_n, st_m, st_rw, st_shift = states
    nb = x.shape[0]
    z = _proj_in(x, lp["norm1_g"], lp["w_in"], nb)
    zw = lambda off, w: (z, (nb, w), (0, off // w))
    w0, wup, a0, aup2, gup, kkw, kaw, rk, ng = lp["rw"]
    sds = lambda *s: jax.ShapeDtypeStruct(s, F32)
    prep_args = [zw(OFF_SM, 128), zw(OFF_RW, RW_W), zw(OFF_RW + RW_W, RW_W), zw(OFF_RW + 2 * RW_W, RW_W),
                 zw(OFF_RW + 3 * RW_W, 128), zw(OFF_RW + 3 * RW_W + 128, 128),
                 (st_m, (1, nb, ML_HEADS), (l, 0, 0)), (st_shift, (1, nb, RW_COLS), (l, 0, 0)),
                 lp["aup_pad"], lp["ab"], lp["ib"], lp["fb"], *lp["mus"],
                 w0, wup, a0, aup2, gup, kkw, kaw]
    (la, m_new, wend, ws, rr, rkm, rv, rg, kkt, bt, kt, rt, wt, vt) = _full_call(
        _sample_prep_kernel, prep_args,
        [sds(nb, GLA_QK), sds(nb, 4), sds(nb, 4), sds(nb, 4)] + [sds(nb, RW_W)] * 4 + [sds(RW_W, nb)] * 6,
        "sample_prep")

    pv = lambda *idx: None if prev is None else [prev[i] for i in idx]
    gla_all, o_g = _state_call(
        _gla_step_kernel, l, [st_gla],
        [(z, GLA_QK, OFF_GQ // GLA_QK), (z, GLA_QK, OFF_GK // GLA_QK), (la, GLA_QK, 0),
         (z, GLA_V, OFF_GV // GLA_V)], pv(0), GLA_V, "gla_step")
    c_all, n_all, hm = _state_call(
        _mlstm_step_kernel, l, [st_c, st_n],
        [(z, ML_W, OFF_MQ // ML_W), (z, ML_W, OFF_MK // ML_W), (z, ML_W, OFF_MV // ML_W),
         (wend, 4, 0), (ws, 4, 0), (m_new, 4, 0)], pv(1, 2), ML_W, "mlstm_step")
    rw_all, y = _rw_state_call(l, st_rw, (kkt, bt, kt, rt, wt, vt), None if prev is None else prev[3])

    bf = lambda: jax.ShapeDtypeStruct((nb, D_MODEL), BF16)
    o_gla, o_ml, o_rw = _full_call(
        _sample_post_kernel,
        [o_g, zw(OFF_GG, GLA_V), lp["gla_ng"], hm, zw(OFF_MO, ML_W), lp["ml_ng"],
         y, rr, rkm, rv, rg, rk, ng],
        [bf(), bf(), bf()], "sample_post")
    x = _merge(o_gla, o_ml, o_rw, z, x, lp["wb"], lp["wo"], nb)
    x = _ffn(x, lp["norm2_g"], lp["w1"], lp["w2"], fg, final, nb)
    return x, (gla_all, c_all, n_all, rw_all), (m_new, z[:, OFF_RW:OFF_RW + RW_COLS])


def kernel(x_prompt, x_sample, state_gla, state_mlstm_c, state_mlstm_n, state_mlstm_m, state_rwkv, state_shift, norm1_g, w_in, gla_a_up, gla_a_bias, gla_norm_g, ml_i_bias, ml_f_bias, ml_norm_g, rw_mu, rw_w0, rw_w_up, rw_a0, rw_a_up, rw_g_up, rw_k_k, rw_k_a, rw_r_k, rw_norm_g, w_branch, w_out, norm2_g, w_ff1, w_ff2, final_g):
    p = dict(norm1_g=norm1_g, w_in=w_in, gla_a_up=gla_a_up, gla_a_bias=gla_a_bias, gla_norm_g=gla_norm_g,
             ml_i_bias=ml_i_bias, ml_f_bias=ml_f_bias, ml_norm_g=ml_norm_g, rw_mu=rw_mu, rw_w0=rw_w0,
             rw_w_up=rw_w_up, rw_a0=rw_a0, rw_a_up=rw_a_up, rw_g_up=rw_g_up, rw_k_k=rw_k_k, rw_k_a=rw_k_a,
             rw_r_k=rw_r_k, rw_norm_g=rw_norm_g, w_branch=w_branch, w_out=w_out, norm2_g=norm2_g,
             w_ff1=w_ff1, w_ff2=w_ff2)
    depth = w_in.shape[0]
    p["w_in_t"] = _w_in_prep(w_in)
    b_, t_, _ = x_prompt.shape
    nb = x_sample.shape[0]
    fg = final_g.reshape(1, D_MODEL)
    xp = x_prompt.reshape(b_ * t_, D_MODEL)
    xs = x_sample.reshape(nb, D_MODEL)
    s_in = (state_gla, state_mlstm_c, state_mlstm_n.reshape(depth, nb, ML_W), state_mlstm_m,
            jnp.transpose(state_rwkv, (0, 2, 3, 4, 1)), state_shift)
    p_states, s_big, s_small = [], None, []
    for l in range(depth):
        lp = _prep_layer_params(l, p)
        final = l == depth - 1
        xp, stp = _prompt_layer(xp, lp, fg, final, b_, t_)
        xs, s_big, small = _sample_layer(l, xs, s_in, s_big, lp, fg, final)
        p_states.append(stp)
        s_small.append(small)
    ps = [jnp.stack([s[i] for s in p_states]) for i in range(6)]
    gla_s, mlc_s, mln_s, rw_s = s_big
    rw_s = jnp.transpose(rw_s, (0, 4, 1, 2, 3))
    mln_s = mln_s.reshape(depth, nb, ML_HEADS, ML_DH)
    mlm_s = jnp.stack([s[0] for s in s_small])
    shift_s = jnp.stack([s[1] for s in s_small])
    return (xp.reshape(b_, t_, D_MODEL), xs.reshape(nb, 1, D_MODEL),
            ps[0], gla_s, ps[1], mlc_s, ps[2], mln_s, ps[3], mlm_s, ps[4], rw_s, ps[5], shift_s)
```
